```python
import math
import jax, jax.numpy as jnp
from jax import lax
import numpy as np

D_MODEL = 1024
BATCH = 8
SEQ = 2048
DEPTH = 1
DEC_BATCH = 128
DEC_SEQ = 8
PAST_LEN = 16384
PAGE_SIZE = 128

N_MEM = 256
MIX_WIDTH = 2 * D_MODEL
ML_WIDTH = D_MODEL
ML_HEADS = 4
ML_HEAD_DIM = ML_WIDTH // ML_HEADS
CONV_CH = D_MODEL // 2
CONV_K = 3
XA_WIDTH = D_MODEL // 2
XA_HEADS = 4
XA_HEAD_DIM = XA_WIDTH // XA_HEADS
CHUNK = 64
EPS = 1e-6
IN_SIZES = [ML_WIDTH] * 5 + [ML_HEADS, ML_HEADS] + [CONV_CH] * 4 + [XA_WIDTH] * 2
IN_COLS = sum(IN_SIZES)

kernel_name = "hybrid_mlstm_shortconv_memxattn_step"


def rms_norm(x, g):
    xf = x.astype(jnp.float32)
    r = lax.rsqrt(jnp.mean(xf * xf, axis=-1, keepdims=True) + EPS)
    return (xf * r).astype(x.dtype) * g


def split_in(p):
    idx = [int(c) for c in np.cumsum(IN_SIZES)[:-1]]
    return jnp.split(p, idx, axis=-1)


def mlstm_chunkwise(q, k, v, i_pre, logf, C0, n0, m0):
    out_dtype = q.dtype
    f32 = jnp.float32
    B, T, H, d = q.shape
    L = math.gcd(T, CHUNK)
    nc = T // L

    def to_chunks(a):
        a = a.astype(f32).reshape((B, nc, L) + a.shape[2:])
        return jnp.moveaxis(a, 1, 0)

    xs = tuple(to_chunks(a) for a in (q, k, v, i_pre, logf))
    causal = jnp.tril(jnp.ones((L, L), dtype=bool))

    def step(carry, inp):
        C, n, m = carry
        qb, kb, vb, ib, fb = inp
        F = jnp.moveaxis(jnp.cumsum(fb, axis=1), 1, 2)
        ig = jnp.moveaxis(ib, 1, 2)
        Dm = ig[:, :, None, :] + F[:, :, :, None] - F[:, :, None, :]
        Dm = jnp.where(causal, Dm, -jnp.inf)
        inter = m[:, :, None] + F
        m_t = jnp.maximum(inter, jnp.max(Dm, axis=-1))
        S = jnp.einsum('bthd,bshd->bhts', qb, kb) * jnp.exp(Dm - m_t[..., None])
        w_inter = jnp.exp(inter - m_t)
        num = (w_inter[..., None] * jnp.einsum('bthd,bhde->bhte', qb, C)
               + jnp.einsum('bhts,bshe->bhte', S, vb))
        den = w_inter * jnp.einsum('bthd,bhd->bht', qb, n) + jnp.sum(S, axis=-1)
        h = num / jnp.maximum(jnp.abs(den), jnp.exp(-m_t))[..., None]
        m_new = m_t[:, :, -1]
        F_L = F[:, :, -1]
        w_C = jnp.exp(m + F_L - m_new)
        w_s = jnp.exp(ig + F_L[..., None] - F - m_new[..., None])
        C_new = w_C[..., None, None] * C + jnp.einsum('bhs,bshd,bshe->bhde', w_s, kb, vb)
        n_new = w_C[..., None] * n + jnp.einsum('bhs,bshd->bhd', w_s, kb)
        return (C_new, n_new, m_new), jnp.moveaxis(h, 1, 2)

    (C, n, m), hs = lax.scan(step, (C0.astype(f32), n0.astype(f32), m0.astype(f32)), xs)
    h = jnp.moveaxis(hs, 0, 1).reshape(B, T, H, d)
    return (h.astype(out_dtype), C.astype(out_dtype), n.astype(out_dtype), m.astype(out_dtype))


def memory_kv(mem, g_mem, w_mem_kv):
    B = mem.shape[0]
    kv = rms_norm(mem, g_mem) @ w_mem_kv
    mk, mv = jnp.split(kv, 2, axis=-1)
    return (mk.reshape(B, N_MEM, XA_HEADS, XA_HEAD_DIM), mv.reshape(B, N_MEM, XA_HEADS, XA_HEAD_DIM))


def hybrid_layer(x, mem_k, mem_v, C0, n0, m0, conv0,
                 g_norm, w_in, b_if, g_head, conv_w, conv_b, w_out):
    B, T, _ = x.shape
    xn = rms_norm(x, g_norm)
    proj = xn @ w_in
    (q, k, v, o_pre, z_ml, i_pre, f_pre,
     b_g, c_g, xc, z_cv, q_x, z_x) = split_in(proj)

    heads = lambda a: a.reshape(B, T, ML_HEADS, ML_HEAD_DIM)
    i_pre = i_pre + b_if[:ML_HEADS]
    logf = jax.nn.log_sigmoid(f_pre + b_if[ML_HEADS:])
    h, C, n, m = mlstm_chunkwise(heads(q), heads(k) * (ML_HEAD_DIM ** -0.5), heads(v),
                                 i_pre, logf, C0, n0, m0)
    h = jax.nn.sigmoid(heads(o_pre)) * h
    h = rms_norm(h, g_head.reshape(ML_HEADS, ML_HEAD_DIM))
    y_ml = h.reshape(B, T, ML_WIDTH) * jax.nn.silu(z_ml)

    u = c_g * xc
    ext = jnp.concatenate([conv0.astype(u.dtype), u], axis=1)
    yc = conv_b + sum(conv_w[j] * ext[:, j:j + T] for j in range(CONV_K))
    y_cv = b_g * yc * jax.nn.silu(z_cv)
    conv_new = ext[:, T:]

    qx = q_x.reshape(B, T, XA_HEADS, XA_HEAD_DIM)
    s = jnp.einsum('bthd,bshd->bhts', qx, mem_k).astype(jnp.float32) * (XA_HEAD_DIM ** -0.5)
    p = jax.nn.softmax(s, axis=-1).astype(x.dtype)
    y_xa = jnp.einsum('bhts,bshd->bthd', p, mem_v).reshape(B, T, XA_WIDTH) * jax.nn.silu(z_x)

    y = jnp.concatenate([y_ml, y_cv, y_xa], axis=-1) @ w_out
    return (x + y, C, n, m, conv_new)


def setup_inputs(seed: int = 0) -> dict:
    key = jax.random.key(seed)
    ks = jax.random.split(key, 20)
    nrm = jax.random.normal
    f32 = jnp.float32
    x_prompt = nrm(ks[0], (BATCH, SEQ, D_MODEL), f32)
    x_sample = nrm(ks[1], (DEC_BATCH, DEC_SEQ, D_MODEL), f32)
    state_mlstm_C = 0.05 * nrm(ks[2], (DEPTH, DEC_BATCH, ML_HEADS, ML_HEAD_DIM, ML_HEAD_DIM), f32)
    state_mlstm_n = 0.5 * nrm(ks[3], (DEPTH, DEC_BATCH, ML_HEADS, ML_HEAD_DIM), f32)
    state_mlstm_m = jax.random.uniform(ks[4], (DEPTH, DEC_BATCH, ML_HEADS), f32, 0.0, 4.0)
    state_conv = 0.5 * nrm(ks[5], (DEPTH, DEC_BATCH, CONV_K - 1, CONV_CH), f32)
    cache_mem_k = nrm(ks[6], (DEPTH, DEC_BATCH, N_MEM, XA_HEADS, XA_HEAD_DIM), f32)
    cache_mem_v = nrm(ks[7], (DEPTH, DEC_BATCH, N_MEM, XA_HEADS, XA_HEAD_DIM), f32)
    mem_prompt = nrm(ks[8], (BATCH, N_MEM, D_MODEL), f32)
    g_norm = 1.0 + 0.05 * nrm(ks[9], (DEPTH, D_MODEL), f32)
    w_in = nrm(ks[10], (DEPTH, D_MODEL, IN_COLS), f32) * D_MODEL ** -0.5
    forget_bias = jnp.tile(jnp.linspace(3.0, 6.0, ML_HEADS, dtype=f32), (DEPTH, 1))
    b_if = jnp.concatenate([0.1 * nrm(ks[11], (DEPTH, ML_HEADS), f32),
                            forget_bias + 0.1 * nrm(ks[12], (DEPTH, ML_HEADS), f32)], axis=-1)
    g_head = 1.0 + 0.05 * nrm(ks[13], (DEPTH, ML_WIDTH), f32)
    conv_w = nrm(ks[14], (DEPTH, CONV_K, CONV_CH), f32) * CONV_K ** -0.5
    conv_b = 0.02 * nrm(ks[15], (DEPTH, CONV_CH), f32)
    g_mem = 1.0 + 0.05 * nrm(ks[16], (DEPTH, D_MODEL), f32)
    w_mem_kv = nrm(ks[17], (DEPTH, D_MODEL, 2 * XA_WIDTH), f32) * D_MODEL ** -0.5
    w_out = nrm(ks[18], (DEPTH, MIX_WIDTH, D_MODEL), f32) * MIX_WIDTH ** -0.5
    g_final = 1.0 + 0.05 * nrm(ks[19], (D_MODEL,), f32)
    return {"x_prompt": x_prompt, "x_sample": x_sample,
            "state_mlstm_C": state_mlstm_C, "state_mlstm_n": state_mlstm_n,
            "state_mlstm_m": state_mlstm_m, "state_conv": state_conv,
            "cache_mem_k": cache_mem_k, "cache_mem_v": cache_mem_v,
            "mem_prompt": mem_prompt,
            "g_norm": g_norm, "w_in": w_in, "b_if": b_if, "g_head": g_head,
            "conv_w": conv_w, "conv_b": conv_b, "g_mem": g_mem, "w_mem_kv": w_mem_kv,
            "w_out": w_out, "g_final": g_final}


def reference(x_prompt, x_sample, state_mlstm_C, state_mlstm_n, state_mlstm_m, state_conv,
              cache_mem_k, cache_mem_v, mem_prompt,
              g_norm, w_in, b_if, g_head, conv_w, conv_b, g_mem, w_mem_kv, w_out, g_final):
    Bp = x_prompt.shape[0]
    dt = x_prompt.dtype
    hp, hs = x_prompt, x_sample
    Cp_l, np_l, mp_l, cvp_l, mkp_l, mvp_l = [], [], [], [], [], []
    Cs_l, ns_l, ms_l, cvs_l = [], [], [], []
    for l in range(DEPTH):
        lw = (g_norm[l], w_in[l], b_if[l], g_head[l], conv_w[l], conv_b[l], w_out[l])
        mk_p, mv_p = memory_kv(mem_prompt, g_mem[l], w_mem_kv[l])
        C0 = jnp.zeros((Bp, ML_HEADS, ML_HEAD_DIM, ML_HEAD_DIM), dt)
        n0 = jnp.zeros((Bp, ML_HEADS, ML_HEAD_DIM), dt)
        m0 = jnp.zeros((Bp, ML_HEADS), dt)
        cv0 = jnp.zeros((Bp, CONV_K - 1, CONV_CH), dt)
        hp, Cp, np_, mp, cvp = hybrid_layer(hp, mk_p, mv_p, C0, n0, m0, cv0, *lw)
        hs, Cs, ns, ms, cvs = hybrid_layer(hs, cache_mem_k[l], cache_mem_v[l],
                                           state_mlstm_C[l], state_mlstm_n[l],
                                           state_mlstm_m[l], state_conv[l], *lw)
        Cp_l.append(Cp); np_l.append(np_); mp_l.append(mp); cvp_l.append(cvp)
        mkp_l.append(mk_p); mvp_l.append(mv_p)
        Cs_l.append(Cs); ns_l.append(ns); ms_l.append(ms); cvs_l.append(cvs)
    y_prompt = rms_norm(hp, g_final)
    y_sample = rms_norm(hs, g_final)
    return (y_prompt, y_sample,
            jnp.stack(Cp_l), jnp.stack(np_l), jnp.stack(mp_l), jnp.stack(cvp_l),
            jnp.stack(mkp_l), jnp.stack(mvp_l),
            jnp.stack(Cs_l), jnp.stack(ns_l), jnp.stack(ms_l), jnp.stack(cvs_l))
```

```python
import functools

import jax
import jax.numpy as jnp
from jax import lax
from jax.experimental import pallas as pl
from jax.experimental.pallas import tpu as pltpu

D_MODEL = 1024
ML_HEADS = 4
ML_HD = 256
ML_WIDTH = ML_HEADS * ML_HD
CONV_CH = 512
CONV_K = 3
XA_HEADS = 4
XA_HD = 128
XA_WIDTH = XA_HEADS * XA_HD
N_MEM = 256
DEC_SEQ_LEN = 8
MIX_WIDTH = ML_WIDTH + CONV_CH + XA_WIDTH
EPS = 1e-6

SUBLANES = 8
LANES = 128

COL_Q = 0
COL_K = ML_WIDTH
COL_V = 2 * ML_WIDTH
COL_O = 3 * ML_WIDTH
COL_ZML = 4 * ML_WIDTH
COL_IF = 5 * ML_WIDTH
COL_BG = COL_IF + LANES
COL_CG = COL_BG + CONV_CH
COL_XC = COL_CG + CONV_CH
COL_ZCV = COL_XC + CONV_CH
COL_QX = COL_ZCV + CONV_CH
COL_ZX = COL_QX + XA_WIDTH
PROJ_COLS = COL_ZX + XA_WIDTH
N_IF = 2 * ML_HEADS

GC_R = 0 * SUBLANES
GC_WI = 1 * SUBLANES
GC_EN = 2 * SUBLANES
GC_WS = 3 * SUBLANES
GC_WC = 4 * SUBLANES
GC_USED = 5 * SUBLANES

PROMPT_BLOCK = 256
SAMPLE_BLOCK_SEQS = 16
SAMPLE_STEP_SEQS = 4
INPROJ_TILE = 640
VMEM_LIMIT = 56 * 1024 * 1024

_F32 = jnp.float32
_BF16 = jnp.bfloat16


def _dot(a, b):
    return jnp.dot(a, b, preferred_element_type=_F32)


def _dot_nt(a, b):
    return lax.dot_general(a, b, (((1,), (1,)), ((), ())), preferred_element_type=_F32)


def _dot_tn(a, b):
    return lax.dot_general(a, b, (((0,), (0,)), ((), ())), preferred_element_type=_F32)


def _rms(x, g):
    r = lax.rsqrt(jnp.mean(x * x, axis=-1, keepdims=True) + EPS)
    return (x * r) * g


def _silu(x):
    return x * jax.nn.sigmoid(x)


def _log_sigmoid(x):
    return jnp.minimum(x, 0.0) - jnp.log1p(jnp.exp(-jnp.abs(x)))


def _gate_rows(slab, m_prev, seg):
    n = slab.shape[1]
    pos = lax.broadcasted_iota(jnp.int32, slab.shape, 1) & (seg - 1)

    def scan(x, op, fill, reverse=False):
        k = 1
        while k < seg:
            if reverse:
                shifted, ok = pltpu.roll(x, n - k, axis=1), pos < seg - k
            else:
                shifted, ok = pltpu.roll(x, k, axis=1), pos >= k
            x = op(x, jnp.where(ok, shifted, fill))
            k *= 2
        return x

    logf = _log_sigmoid(pltpu.roll(slab, ML_HEADS, axis=0))
    f_cum = scan(logf, jnp.add, 0.0)
    a = slab - f_cum
    r = jnp.maximum(scan(a, jnp.maximum, -jnp.inf), m_prev)
    r_last = scan(r, jnp.maximum, -jnp.inf, reverse=True)
    m_t = f_cum + r
    w_inter = jnp.exp(m_prev - r)
    e_negm = jnp.exp(-m_t)
    w_s = jnp.exp(a - r_last)
    w_c = jnp.exp(m_prev - r_last)
    return a, r, m_t, w_inter, e_negm, w_s, w_c


def _gate_cols(r, w_inter, e_negm, w_s, w_c):
    n = r.shape[1]
    pad = jnp.zeros((LANES - GC_USED, n), _F32)
    return jnp.concatenate([r, w_inter, e_negm, w_s, w_c, pad], axis=0).T


def _ml_head_out(hh, o_pre, z, g_head):
    hh = jax.nn.sigmoid(o_pre) * hh
    return _rms(hh, g_head) * _silu(z)


def _out_tail(ymix_bf16, w_out, x, g_final):
    y = _dot(ymix_bf16, w_out) + x
    return _rms(y, g_final)


def _memkv_kernel(mem_ref, g_ref, w_ref, mk_ref, mv_ref):
    xn = _rms(mem_ref[0], g_ref[...]).astype(_BF16)
    kv = _dot(xn, w_ref[...])
    mk_ref[0] = kv[:, :XA_WIDTH]
    mv_ref[0] = kv[:, XA_WIDTH:]


def _memkv_call(mem, g_mem, w_kv):
    b = mem.shape[0]
    const = lambda i: (0, 0)
    return pl.pallas_call(
        _memkv_kernel,
        grid=(b,),
        in_specs=[
            pl.BlockSpec((1, N_MEM, D_MODEL), lambda i: (i, 0, 0)),
            pl.BlockSpec((1, D_MODEL), const),
            pl.BlockSpec((D_MODEL, 2 * XA_WIDTH), const),
        ],
        out_specs=[
            pl.BlockSpec((1, N_MEM, XA_WIDTH), lambda i: (i, 0, 0)),
            pl.BlockSpec((1, N_MEM, XA_WIDTH), lambda i: (i, 0, 0)),
        ],
        out_shape=[jax.ShapeDtypeStruct((b, N_MEM, XA_WIDTH), _F32)] * 2,
        compiler_params=pltpu.CompilerParams(dimension_semantics=("arbitrary",)),
        name="memkv",
    )(mem, g_mem, w_kv)


def _prompt_kernel(x_ref, mk_ref, mv_ref, gnorm_ref, w_ref, bif_ref, ghead_ref, convw_ref, convb_ref,
                   wout_ref, gfinal_ref,
                   y_ref, c_ref, n_ref, m_ref, conv_ref,
                   ext_ref, ymix_ref):
    tb = PROMPT_BLOCK
    t_blk = pl.program_id(1)

    @pl.when(t_blk == 0)
    def _():
        c_ref[...] = jnp.zeros_like(c_ref)
        n_ref[...] = jnp.zeros_like(n_ref)
        m_ref[...] = jnp.zeros_like(m_ref)
        ext_ref[0:SUBLANES, :] = jnp.zeros((SUBLANES, CONV_CH), _F32)

    x = x_ref[0]
    xb = _rms(x, gnorm_ref[...]).astype(_BF16)

    def proj(col, width):
        return _dot(xb, w_ref[:, col:col + width])

    slab = (proj(COL_IF, LANES) + bif_ref[...]).T[0:SUBLANES, :]
    m_prev8 = m_ref[0]
    m_prev = jnp.concatenate([m_prev8] * (tb // LANES), axis=1)
    a, r, m_t, w_inter, e_negm, w_s, w_c = _gate_rows(slab, m_prev, tb)
    cols = _gate_cols(r, w_inter, e_negm, w_s, w_c)
    m_ref[0] = jnp.broadcast_to(m_t[:, tb - 1:tb], (SUBLANES, LANES))

    row = lax.broadcasted_iota(jnp.int32, (tb, tb), 0)
    col = lax.broadcasted_iota(jnp.int32, (tb, tb), 1)
    causal = col <= row

    for h in range(ML_HEADS):
        hs = h * ML_HD
        q = proj(COL_Q + hs, ML_HD)
        k = proj(COL_K + hs, ML_HD) * (ML_HD ** -0.5)
        vb = proj(COL_V + hs, ML_HD).astype(_BF16)
        qb = q.astype(_BF16)
        r_col = cols[:, GC_R + h:GC_R + h + 1]
        wi_col = cols[:, GC_WI + h:GC_WI + h + 1]
        en_col = cols[:, GC_EN + h:GC_EN + h + 1]
        ws_col = cols[:, GC_WS + h:GC_WS + h + 1]
        wc = w_c[h:h + 1, 0:1]

        decay = jnp.exp(jnp.where(causal, a[h:h + 1, :] - r_col, -jnp.inf))
        s_mat = _dot_nt(qb, k.astype(_BF16)) * decay
        c_old = c_ref[0, h]
        n_old = n_ref[0, h:h + 1, :]
        num = wi_col * _dot(qb, c_old.astype(_BF16)) + _dot(s_mat.astype(_BF16), vb)
        den = wi_col * jnp.sum(q * n_old, axis=1, keepdims=True) + jnp.sum(s_mat, axis=1, keepdims=True)
        hh = num * (1.0 / jnp.maximum(jnp.abs(den), en_col))

        kw = k * ws_col
        c_ref[0, h] = wc * c_old + _dot_tn(kw.astype(_BF16), vb)
        n_ref[0, h:h + 1, :] = wc * n_old + jnp.sum(kw, axis=0, keepdims=True)

        y_ml = _ml_head_out(hh, proj(COL_O + hs, ML_HD), proj(COL_ZML + hs, ML_HD), ghead_ref[:, hs:hs + ML_HD])
        ymix_ref[:, hs:hs + ML_HD] = y_ml.astype(_BF16)

    u = proj(COL_CG, CONV_CH) * proj(COL_XC, CONV_CH)
    ext_ref[SUBLANES:SUBLANES + tb, :] = u
    u1 = ext_ref[SUBLANES - 1:SUBLANES - 1 + tb, :]
    u2 = ext_ref[SUBLANES - 2:SUBLANES - 2 + tb, :]
    yc = convb_ref[...] + convw_ref[0:1, :] * u2 + convw_ref[1:2, :] * u1 + convw_ref[2:3, :] * u
    y_cv = proj(COL_BG, CONV_CH) * yc * _silu(proj(COL_ZCV, CONV_CH))
    ymix_ref[:, ML_WIDTH:ML_WIDTH + CONV_CH] = y_cv.astype(_BF16)
    tail = u[tb - (CONV_K - 1):tb, :]
    ext_ref[SUBLANES - (CONV_K - 1):SUBLANES, :] = tail
    conv_ref[0] = tail

    qx = proj(COL_QX, XA_WIDTH)
    zx = proj(COL_ZX, XA_WIDTH)
    for h in range(XA_HEADS):
        hs = h * XA_HD
        kh = mk_ref[0, :, hs:hs + XA_HD].astype(_BF16)
        vh = mv_ref[0, :, hs:hs + XA_HD].astype(_BF16)
        s = _dot_nt(qx[:, hs:hs + XA_HD].astype(_BF16), kh) * (XA_HD ** -0.5)
        e = jnp.exp(s - jnp.max(s, axis=1, keepdims=True))
        p = e * (1.0 / jnp.sum(e, axis=1, keepdims=True))
        y_xa = _dot(p.astype(_BF16), vh) * _silu(zx[:, hs:hs + XA_HD])
        ymix_ref[:, ML_WIDTH + CONV_CH + hs:ML_WIDTH + CONV_CH + hs + XA_HD] = y_xa.astype(_BF16)

    y_ref[0] = _out_tail(ymix_ref[...], wout_ref[...], x, gfinal_ref[...])


def _prompt_call(x, mk, mv, g_norm, w_all, b_if, g_head, conv_w, conv_b, w_out, g_final):
    b, t, _ = x.shape
    tb = PROMPT_BLOCK
    const2 = lambda i, j: (0, 0)
    resident = functools.partial(pl.BlockSpec, index_map=const2, pipeline_mode=pl.Buffered(1))
    per_batch3 = lambda i, j: (i, 0, 0)
    return pl.pallas_call(
        _prompt_kernel,
        grid=(b, t // tb),
        in_specs=[
            pl.BlockSpec((1, tb, D_MODEL), lambda i, j: (i, j, 0)),
            pl.BlockSpec((1, N_MEM, XA_WIDTH), per_batch3),
            pl.BlockSpec((1, N_MEM, XA_WIDTH), per_batch3),
            pl.BlockSpec((1, D_MODEL), const2),
            resident((D_MODEL, PROJ_COLS)),
            pl.BlockSpec((1, LANES), const2),
            pl.BlockSpec((1, ML_WIDTH), const2),
            pl.BlockSpec((CONV_K, CONV_CH), const2),
            pl.BlockSpec((1, CONV_CH), const2),
            resident((MIX_WIDTH, D_MODEL)),
            pl.BlockSpec((1, D_MODEL), const2),
        ],
        out_specs=[
            pl.BlockSpec((1, tb, D_MODEL), lambda i, j: (i, j, 0)),
            pl.BlockSpec((1, ML_HEADS, ML_HD, ML_HD), lambda i, j: (i, 0, 0, 0)),
            pl.BlockSpec((1, ML_HEADS, ML_HD), per_batch3),
            pl.BlockSpec((1, SUBLANES, LANES), per_batch3),
            pl.BlockSpec((1, CONV_K - 1, CONV_CH), per_batch3),
        ],
        out_shape=[
            jax.ShapeDtypeStruct((b, t, D_MODEL), _F32),
            jax.ShapeDtypeStruct((b, ML_HEADS, ML_HD, ML_HD), _F32),
            jax.ShapeDtypeStruct((b, ML_HEADS, ML_HD), _F32),
            jax.ShapeDtypeStruct((b, SUBLANES, LANES), _F32),
            jax.ShapeDtypeStruct((b, CONV_K - 1, CONV_CH), _F32),
        ],
        scratch_shapes=[
            pltpu.VMEM((SUBLANES + tb, CONV_CH), _F32),
            pltpu.VMEM((tb, MIX_WIDTH), _BF16),
        ],
        compiler_params=pltpu.CompilerParams(
            dimension_semantics=("arbitrary", "arbitrary"), vmem_limit_bytes=VMEM_LIMIT),
        name="prompt_layer",
    )(x, mk, mv, g_norm, w_all, b_if, g_head, conv_w, conv_b, w_out, g_final)


def _inproj_kernel(x_ref, gnorm_ref, w_ref, bif_ref, m0_ref, proj_ref, cols_ref, rows_ref, xb_ref):
    j = pl.program_id(0)

    @pl.when(j == 0)
    def _():
        xb_ref[...] = _rms(x_ref[...], gnorm_ref[...]).astype(_BF16)

    p = _dot(xb_ref[...], w_ref[...])
    proj_ref[...] = p

    @pl.when(j == COL_IF // INPROJ_TILE)
    def _():
        off = COL_IF % INPROJ_TILE
        slab = (p[:, off:off + LANES] + bif_ref[...]).T[0:SUBLANES, :]
        a, r, m_t, w_inter, e_negm, w_s, w_c = _gate_rows(slab, m0_ref[...], DEC_SEQ_LEN)
        cols_ref[...] = _gate_cols(r, w_inter, e_negm, w_s, w_c)
        rows_ref[0:SUBLANES, :] = a
        rows_ref[SUBLANES:2 * SUBLANES, :] = m_t


def _inproj_call(x, g_norm, w_all, b_if, m0_rows):
    n_tok = x.shape[0]
    const = lambda j: (0, 0)
    return pl.pallas_call(
        _inproj_kernel,
        grid=(PROJ_COLS // INPROJ_TILE,),
        in_specs=[
            pl.BlockSpec((n_tok, D_MODEL), const),
            pl.BlockSpec((1, D_MODEL), const),
            pl.BlockSpec((D_MODEL, INPROJ_TILE), lambda j: (0, j)),
            pl.BlockSpec((1, LANES), const),
            pl.BlockSpec((SUBLANES, n_tok), const),
        ],
        out_specs=[
            pl.BlockSpec((n_tok, INPROJ_TILE), lambda j: (0, j)),
            pl.BlockSpec((n_tok, LANES), const),
            pl.BlockSpec((2 * SUBLANES, n_tok), const),
        ],
        out_shape=[
            jax.ShapeDtypeStruct((n_tok, PROJ_COLS), _F32),
            jax.ShapeDtypeStruct((n_tok, LANES), _F32),
            jax.ShapeDtypeStruct((2 * SUBLANES, n_tok), _F32),
        ],
        scratch_shapes=[pltpu.VMEM((n_tok, D_MODEL), _BF16)],
        compiler_params=pltpu.CompilerParams(
            dimension_semantics=("arbitrary",), vmem_limit_bytes=VMEM_LIMIT),
        name="sample_inproj",
    )(x, g_norm, w_all, b_if, m0_rows)


def _sample_kernel(proj_ref, cols_ref, rows_ref, c0_ref, n0_ref, conv0_ref, mk_ref, mv_ref,
                   ghead_ref, convw_ref, convb_ref,
                   ymix_ref, c_ref, n_ref, conv_ref,
                   sv_ref, rs_ref, ext_ref):
    nt = SAMPLE_BLOCK_SEQS * DEC_SEQ_LEN
    j = pl.program_id(1)

    @pl.when(j == 0)
    def _():
        row = lax.broadcasted_iota(jnp.int32, (nt, nt), 0)
        col = lax.broadcasted_iota(jnp.int32, (nt, nt), 1)
        mask = (col <= row) & ((row // DEC_SEQ_LEN) == (col // DEC_SEQ_LEN))
        a = rows_ref[0:SUBLANES, :]
        for h in range(ML_HEADS):
            hs = h * ML_HD
            qb = proj_ref[:, COL_Q + hs:COL_Q + hs + ML_HD].astype(_BF16)
            kb = (proj_ref[:, COL_K + hs:COL_K + hs + ML_HD] * (ML_HD ** -0.5)).astype(_BF16)
            vb = proj_ref[:, COL_V + hs:COL_V + hs + ML_HD].astype(_BF16)
            r_col = cols_ref[:, GC_R + h:GC_R + h + 1]
            decay = jnp.exp(jnp.where(mask, a[h:h + 1, :] - r_col, -jnp.inf))
            s_mat = _dot_nt(qb, kb) * decay
            sv_ref[:, hs:hs + ML_HD] = _dot(s_mat.astype(_BF16), vb)
            rs_ref[:, h:h + 1] = jnp.sum(s_mat, axis=1, keepdims=True)

    for sl in range(SAMPLE_STEP_SEQS):
        ro = pl.multiple_of((j * SAMPLE_STEP_SEQS + sl) * DEC_SEQ_LEN, DEC_SEQ_LEN)
        rows = pl.ds(ro, DEC_SEQ_LEN)
        gc = cols_ref[rows, :]
        for h in range(ML_HEADS):
            hs = h * ML_HD
            q = proj_ref[rows, COL_Q + hs:COL_Q + hs + ML_HD]
            k = proj_ref[rows, COL_K + hs:COL_K + hs + ML_HD] * (ML_HD ** -0.5)
            vb = proj_ref[rows, COL_V + hs:COL_V + hs + ML_HD].astype(_BF16)
            wi_col = gc[:, GC_WI + h:GC_WI + h + 1]
            en_col = gc[:, GC_EN + h:GC_EN + h + 1]
            ws_col = gc[:, GC_WS + h:GC_WS + h + 1]
            wc = gc[0:1, GC_WC + h:GC_WC + h + 1]
            c_old = c0_ref[sl, h]
            n_old = n0_ref[sl, h:h + 1, :]
            num = wi_col * _dot(q.astype(_BF16), c_old.astype(_BF16)) + sv_ref[rows, hs:hs + ML_HD]
            den = wi_col * jnp.sum(q * n_old, axis=1, keepdims=True) + rs_ref[rows, h:h + 1]
            hh = num * (1.0 / jnp.maximum(jnp.abs(den), en_col))
            kw = k * ws_col
            c_ref[sl, h] = wc * c_old + _dot_tn(kw.astype(_BF16), vb)
            n_ref[sl, h:h + 1, :] = wc * n_old + jnp.sum(kw, axis=0, keepdims=True)
            y_ml = _ml_head_out(hh, proj_ref[rows, COL_O + hs:COL_O + hs + ML_HD],
                                proj_ref[rows, COL_ZML + hs:COL_ZML + hs + ML_HD], ghead_ref[:, hs:hs + ML_HD])
            ymix_ref[rows, hs:hs + ML_HD] = y_ml

        u = proj_ref[rows, COL_CG:COL_CG + CONV_CH] * proj_ref[rows, COL_XC:COL_XC + CONV_CH]
        ext_ref[sl, SUBLANES - (CONV_K - 1):SUBLANES, :] = conv0_ref[sl]
        ext_ref[sl, SUBLANES:2 * SUBLANES, :] = u
        u1 = ext_ref[sl, SUBLANES - 1:2 * SUBLANES - 1, :]
        u2 = ext_ref[sl, SUBLANES - 2:2 * SUBLANES - 2, :]
        yc = convb_ref[...] + convw_ref[0:1, :] * u2 + convw_ref[1:2, :] * u1 + convw_ref[2:3, :] * u
        y_cv = (proj_ref[rows, COL_BG:COL_BG + CONV_CH] * yc
                * _silu(proj_ref[rows, COL_ZCV:COL_ZCV + CONV_CH]))
        ymix_ref[rows, ML_WIDTH:ML_WIDTH + CONV_CH] = y_cv
        conv_ref[sl] = u[DEC_SEQ_LEN - (CONV_K - 1):DEC_SEQ_LEN, :]

        for h in range(XA_HEADS):
            hs = h * XA_HD
            qh = proj_ref[rows, COL_QX + hs:COL_QX + hs + XA_HD].astype(_BF16)
            kh = mk_ref[sl, :, hs:hs + XA_HD].astype(_BF16)
            vh = mv_ref[sl, :, hs:hs + XA_HD].astype(_BF16)
            s = _dot_nt(qh, kh) * (XA_HD ** -0.5)
            e = jnp.exp(s - jnp.max(s, axis=1, keepdims=True))
            p = e * (1.0 / jnp.sum(e, axis=1, keepdims=True))
            y_xa = _dot(p.astype(_BF16), vh) * _silu(proj_ref[rows, COL_ZX + hs:COL_ZX + hs + XA_HD])
            c0 = ML_WIDTH + CONV_CH + hs
            ymix_ref[rows, c0:c0 + XA_HD] = y_xa


def _sample_call(proj, cols, rows, c0, n0, conv0, mk, mv, g_head, conv_w, conv_b):
    n_seq = c0.shape[0]
    nt = SAMPLE_BLOCK_SEQS * DEC_SEQ_LEN
    steps = SAMPLE_BLOCK_SEQS // SAMPLE_STEP_SEQS
    ss = SAMPLE_STEP_SEQS
    const2 = lambda i, j: (0, 0)
    blk2 = lambda i, j: (i, 0)
    seq3 = lambda i, j: (i * steps + j, 0, 0)
    seq4 = lambda i, j: (i * steps + j, 0, 0, 0)
    return pl.pallas_call(
        _sample_kernel,
        grid=(n_seq // SAMPLE_BLOCK_SEQS, steps),
        in_specs=[
            pl.BlockSpec((nt, PROJ_COLS), blk2),
            pl.BlockSpec((nt, LANES), blk2),
            pl.BlockSpec((2 * SUBLANES, nt), lambda i, j: (0, i)),
            pl.BlockSpec((ss, ML_HEADS, ML_HD, ML_HD), seq4),
            pl.BlockSpec((ss, ML_HEADS, ML_HD), seq3),
            pl.BlockSpec((ss, CONV_K - 1, CONV_CH), seq3),
            pl.BlockSpec((ss, N_MEM, XA_WIDTH), seq3),
            pl.BlockSpec((ss, N_MEM, XA_WIDTH), seq3),
            pl.BlockSpec((1, ML_WIDTH), const2),
            pl.BlockSpec((CONV_K, CONV_CH), const2),
            pl.BlockSpec((1, CONV_CH), const2),
        ],
        out_specs=[
            pl.BlockSpec((nt, MIX_WIDTH), blk2),
            pl.BlockSpec((ss, ML_HEADS, ML_HD, ML_HD), seq4),
            pl.BlockSpec((ss, ML_HEADS, ML_HD), seq3),
            pl.BlockSpec((ss, CONV_K - 1, CONV_CH), seq3),
        ],
        out_shape=[
            jax.ShapeDtypeStruct((n_seq * DEC_SEQ_LEN, MIX_WIDTH), _F32),
            jax.ShapeDtypeStruct(c0.shape, _F32),
            jax.ShapeDtypeStruct(n0.shape, _F32),
            jax.ShapeDtypeStruct(conv0.shape, _F32),
        ],
        scratch_shapes=[
            pltpu.VMEM((nt, ML_WIDTH), _F32),
            pltpu.VMEM((nt, LANES), _F32),
            pltpu.VMEM((ss, 2 * SUBLANES, CONV_CH), _F32),
        ],
        compiler_params=pltpu.CompilerParams(
            dimension_semantics=("arbitrary", "arbitrary"), vmem_limit_bytes=VMEM_LIMIT),
        name="sample_state",
    )(proj, cols, rows, c0, n0, conv0, mk, mv, g_head, conv_w, conv_b)


def _outproj_kernel(ymix_ref, x_ref, wout_ref, gfinal_ref, y_ref):
    y_ref[...] = _out_tail(ymix_ref[...].astype(_BF16), wout_ref[...], x_ref[...], gfinal_ref[...])


def _outproj_call(ymix, x, w_out, g_final):
    n_tok = x.shape[0]
    rb = 256
    const = lambda i: (0, 0)
    return pl.pallas_call(
        _outproj_kernel,
        grid=(n_tok // rb,),
        in_specs=[
            pl.BlockSpec((rb, MIX_WIDTH), lambda i: (i, 0)),
            pl.BlockSpec((rb, D_MODEL), lambda i: (i, 0)),
            pl.BlockSpec((MIX_WIDTH, D_MODEL), const),
            pl.BlockSpec((1, D_MODEL), const),
        ],
        out_specs=pl.BlockSpec((rb, D_MODEL), lambda i: (i, 0)),
        out_shape=jax.ShapeDtypeStruct((n_tok, D_MODEL), _F32),
        compiler_params=pltpu.CompilerParams(dimension_semantics=("arbitrary",)),
        name="sample_outproj",
    )(ymix, x, w_out, g_final)


def kernel(x_prompt, x_sample, state_mlstm_C, state_mlstm_n, state_mlstm_m, state_conv, cache_mem_k, cache_mem_v, mem_prompt, g_norm, w_in, b_if, g_head, conv_w, conv_b, g_mem, w_mem_kv, w_out, g_final):
    assert w_in.shape[0] == 1, "single-layer trunk"
    bp, tp, _ = x_prompt.shape
    bs, ts, _ = x_sample.shape
    assert ts == DEC_SEQ_LEN and tp % PROMPT_BLOCK == 0 and bs % SAMPLE_BLOCK_SEQS == 0

    w = w_in[0]
    w_if = jnp.pad(w[:, COL_IF:COL_IF + N_IF], ((0, 0), (0, LANES - N_IF)))
    w_all = jnp.concatenate([w[:, :COL_IF], w_if, w[:, COL_IF + N_IF:]], axis=1).astype(_BF16)
    w_out_b = w_out[0].astype(_BF16)
    w_kv_b = w_mem_kv[0].astype(_BF16)
    b_if_p = jnp.pad(b_if[0][None, :], ((0, 0), (0, LANES - N_IF)))
    row = lambda v: v.reshape(1, -1)

    mk_p, mv_p = _memkv_call(mem_prompt, row(g_mem[0]), w_kv_b)
    y_p, c_p, n_p, m_p, conv_p = _prompt_call(
        x_prompt, mk_p, mv_p, row(g_norm[0]), w_all, b_if_p, row(g_head[0]), conv_w[0], row(conv_b[0]),
        w_out_b, row(g_final))

    n_tok = bs * ts
    xs = x_sample.reshape(n_tok, D_MODEL)
    m0_rows = jnp.pad(jnp.repeat(state_mlstm_m[0].T, ts, axis=1), ((0, SUBLANES - ML_HEADS), (0, 0)))
    proj, cols, rows = _inproj_call(xs, row(g_norm[0]), w_all, b_if_p, m0_rows)
    ymix, c_s, n_s, conv_s = _sample_call(
        proj, cols, rows, state_mlstm_C[0], state_mlstm_n[0], state_conv[0],
        cache_mem_k[0].reshape(bs, N_MEM, XA_WIDTH), cache_mem_v[0].reshape(bs, N_MEM, XA_WIDTH),
        row(g_head[0]), conv_w[0], row(conv_b[0]))
    y_s = _outproj_call(ymix, xs, w_out_b, row(g_final))
    m_s = rows[SUBLANES:SUBLANES + ML_HEADS, ts - 1::ts].T

    kv_shape = (1, bp, N_MEM, XA_HEADS, XA_HD)
    return (y_p, y_s.reshape(bs, ts, D_MODEL),
            c_p[None], n_p[None], m_p[:, :ML_HEADS, 0][None], conv_p[None],
            mk_p.reshape(kv_shape), mv_p.reshape(kv_shape),
            c_s[None], n_s[None], m_s[None], conv_s[None])
```

```python
import functools

import jax
import jax.numpy as jnp
from jax import lax
from jax.experimental import pallas as pl
from jax.experimental.pallas import tpu as pltpu

D_MODEL = 1024
ML_HEADS = 4
ML_HD = 256
ML_WIDTH = ML_HEADS * ML_HD
CONV_CH = 512
CONV_K = 3
XA_HEADS = 4
XA_HD = 128
XA_WIDTH = XA_HEADS * XA_HD
N_MEM = 256
XA_KV_ROWS = N_MEM * XA_HEADS
DEC_SEQ_LEN = 8
MIX_WIDTH = ML_WIDTH + CONV_CH + XA_WIDTH
EPS = 1e-6

SUBLANES = 8
LANES = 128

COL_Q = 0
COL_K = ML_WIDTH
COL_V = 2 * ML_WIDTH
COL_O = 3 * ML_WIDTH
COL_ZML = 4 * ML_WIDTH
COL_IF = 5 * ML_WIDTH
COL_BG = COL_IF + LANES
COL_CG = COL_BG + CONV_CH
COL_XC = COL_CG + CONV_CH
COL_ZCV = COL_XC + CONV_CH
COL_QX = COL_ZCV + CONV_CH
COL_ZX = COL_QX + XA_WIDTH
PROJ_COLS = COL_ZX + XA_WIDTH
N_IF = 2 * ML_HEADS

GC_R = 0 * SUBLANES
GC_WI = 1 * SUBLANES
GC_EN = 2 * SUBLANES
GC_WS = 3 * SUBLANES
GC_WC = 4 * SUBLANES
GC_USED = 5 * SUBLANES

PROMPT_BLOCK = 256
SAMPLE_BLOCK_SEQS = 16
SAMPLE_STEP_SEQS = 4
INPROJ_TILE = 640
VMEM_LIMIT = 56 * 1024 * 1024

_F32 = jnp.float32
_BF16 = jnp.bfloat16


def _dot(a, b):
    return jnp.dot(a, b, preferred_element_type=_F32)


def _dot_nt(a, b):
    return lax.dot_general(a, b, (((1,), (1,)), ((), ())), preferred_element_type=_F32)


def _dot_tn(a, b):
    return lax.dot_general(a, b, (((0,), (0,)), ((), ())), preferred_element_type=_F32)


def _rms(x, g):
    r = lax.rsqrt(jnp.mean(x * x, axis=-1, keepdims=True) + EPS)
    return (x * r) * g


def _silu(x):
    return x * jax.nn.sigmoid(x)


def _log_sigmoid(x):
    return jnp.minimum(x, 0.0) - jnp.log1p(jnp.exp(-jnp.abs(x)))


def _gate_rows(slab, m_prev, seg):
    n = slab.shape[1]
    pos = lax.broadcasted_iota(jnp.int32, slab.shape, 1) & (seg - 1)

    def scan(x, op, fill, reverse=False):
        k = 1
        while k < seg:
            if reverse:
                shifted, ok = pltpu.roll(x, n - k, axis=1), pos < seg - k
            else:
                shifted, ok = pltpu.roll(x, k, axis=1), pos >= k
            x = op(x, jnp.where(ok, shifted, fill))
            k *= 2
        return x

    logf = _log_sigmoid(pltpu.roll(slab, ML_HEADS, axis=0))
    f_cum = scan(logf, jnp.add, 0.0)
    a = slab - f_cum
    r = jnp.maximum(scan(a, jnp.maximum, -jnp.inf), m_prev)
    r_last = scan(r, jnp.maximum, -jnp.inf, reverse=True)
    m_t = f_cum + r
    w_inter = jnp.exp(m_prev - r)
    e_negm = jnp.exp(-m_t)
    w_s = jnp.exp(a - r_last)
    w_c = jnp.exp(m_prev - r_last)
    return a, r, m_t, w_inter, e_negm, w_s, w_c


def _gate_cols(r, w_inter, e_negm, w_s, w_c):
    n = r.shape[1]
    pad = jnp.zeros((LANES - GC_USED, n), _F32)
    return jnp.concatenate([r, w_inter, e_negm, w_s, w_c, pad], axis=0).T


def _ml_head_out(hh, o_pre, z, g_head):
    hh = jax.nn.sigmoid(o_pre) * hh
    return _rms(hh, g_head) * _silu(z)


def _out_tail(ymix_bf16, w_out, x, g_final):
    y = _dot(ymix_bf16, w_out) + x
    return _rms(y, g_final)


def _memkv_kernel(mem_ref, g_ref, w_ref, mk_ref, mv_ref, mk_rows_ref, mv_rows_ref):
    xn = _rms(mem_ref[0], g_ref[...]).astype(_BF16)
    kv = _dot(xn, w_ref[...])
    mk_ref[0] = kv[:, :XA_WIDTH]
    mv_ref[0] = kv[:, XA_WIDTH:]
    for h in range(XA_HEADS):
        head_rows = pl.ds(h, N_MEM, stride=XA_HEADS)
        mk_rows_ref[0, head_rows, :] = kv[:, h * XA_HD:(h + 1) * XA_HD]
        mv_rows_ref[0, head_rows, :] = kv[:, XA_WIDTH + h * XA_HD:XA_WIDTH + (h + 1) * XA_HD]


def _memkv_call(mem, g_mem, w_kv):
    b = mem.shape[0]
    const = lambda i: (0, 0)
    per_batch = lambda i: (i, 0, 0)
    return pl.pallas_call(
        _memkv_kernel,
        grid=(b,),
        in_specs=[
            pl.BlockSpec((1, N_MEM, D_MODEL), per_batch),
            pl.BlockSpec((1, D_MODEL), const),
            pl.BlockSpec((D_MODEL, 2 * XA_WIDTH), const),
        ],
        out_specs=[
            pl.BlockSpec((1, N_MEM, XA_WIDTH), per_batch),
            pl.BlockSpec((1, N_MEM, XA_WIDTH), per_batch),
            pl.BlockSpec((1, XA_KV_ROWS, XA_HD), per_batch),
            pl.BlockSpec((1, XA_KV_ROWS, XA_HD), per_batch),
        ],
        out_shape=[jax.ShapeDtypeStruct((b, N_MEM, XA_WIDTH), _F32)] * 2
        + [jax.ShapeDtypeStruct((b, XA_KV_ROWS, XA_HD), _F32)] * 2,
        compiler_params=pltpu.CompilerParams(dimension_semantics=("arbitrary",)),
        name="memkv",
    )(mem, g_mem, w_kv)


def _prompt_kernel(x_ref, mk_ref, mv_ref, gnorm_ref, w_ref, bif_ref, ghead_ref, convw_ref, convb_ref,
                   wout_ref, gfinal_ref,
                   y_ref, c_ref, n_ref, m_ref, conv_ref,
                   ext_ref, ymix_ref, proj_ref):
    tb = PROMPT_BLOCK
    t_blk = pl.program_id(1)

    @pl.when(t_blk == 0)
    def _():
        c_ref[...] = jnp.zeros_like(c_ref)
        n_ref[...] = jnp.zeros_like(n_ref)
        m_ref[...] = jnp.zeros_like(m_ref)
        ext_ref[0:SUBLANES, :] = jnp.zeros((SUBLANES, CONV_CH), _F32)

    x = x_ref[0]
    xb = _rms(x, gnorm_ref[...]).astype(_BF16)

    proj_ref[...] = _dot(xb, w_ref[...])

    def proj(col, width):
        return proj_ref[:, col:col + width]

    slab = (proj(COL_IF, LANES) + bif_ref[...]).T[0:SUBLANES, :]
    m_prev8 = m_ref[0]
    m_prev = jnp.concatenate([m_prev8] * (tb // LANES), axis=1)
    a, r, m_t, w_inter, e_negm, w_s, w_c = _gate_rows(slab, m_prev, tb)
    cols = _gate_cols(r, w_inter, e_negm, w_s, w_c)
    m_ref[0] = jnp.broadcast_to(m_t[:, tb - 1:tb], (SUBLANES, LANES))

    row = lax.broadcasted_iota(jnp.int32, (tb, tb), 0)
    col = lax.broadcasted_iota(jnp.int32, (tb, tb), 1)
    causal = col <= row

    for h in range(ML_HEADS):
        hs = h * ML_HD
        q = proj(COL_Q + hs, ML_HD)
        k = proj(COL_K + hs, ML_HD) * (ML_HD ** -0.5)
        vb = proj(COL_V + hs, ML_HD).astype(_BF16)
        qb = q.astype(_BF16)
        r_col = cols[:, GC_R + h:GC_R + h + 1]
        wi_col = cols[:, GC_WI + h:GC_WI + h + 1]
        en_col = cols[:, GC_EN + h:GC_EN + h + 1]
        ws_col = cols[:, GC_WS + h:GC_WS + h + 1]
        wc = w_c[h:h + 1, 0:1]

        decay = jnp.exp(jnp.where(causal, a[h:h + 1, :] - r_col, -jnp.inf))
        s_mat = _dot_nt(qb, k.astype(_BF16)) * decay
        c_old = c_ref[0, h]
        n_old = n_ref[0, h:h + 1, :]
        num = wi_col * _dot(qb, c_old.astype(_BF16)) + _dot(s_mat.astype(_BF16), vb)
        den = wi_col * jnp.sum(q * n_old, axis=1, keepdims=True) + jnp.sum(s_mat, axis=1, keepdims=True)
        hh = num * (1.0 / jnp.maximum(jnp.abs(den), en_col))

        kw = k * ws_col
        c_ref[0, h] = wc * c_old + _dot_tn(kw.astype(_BF16), vb)
        n_ref[0, h:h + 1, :] = wc * n_old + jnp.sum(kw, axis=0, keepdims=True)

        y_ml = _ml_head_out(hh, proj(COL_O + hs, ML_HD), proj(COL_ZML + hs, ML_HD), ghead_ref[:, hs:hs + ML_HD])
        ymix_ref[:, hs:hs + ML_HD] = y_ml.astype(_BF16)

    u = proj(COL_CG, CONV_CH) * proj(COL_XC, CONV_CH)
    ext_ref[SUBLANES:SUBLANES + tb, :] = u
    u1 = ext_ref[SUBLANES - 1:SUBLANES - 1 + tb, :]
    u2 = ext_ref[SUBLANES - 2:SUBLANES - 2 + tb, :]
    yc = convb_ref[...] + convw_ref[0:1, :] * u2 + convw_ref[1:2, :] * u1 + convw_ref[2:3, :] * u
    y_cv = proj(COL_BG, CONV_CH) * yc * _silu(proj(COL_ZCV, CONV_CH))
    ymix_ref[:, ML_WIDTH:ML_WIDTH + CONV_CH] = y_cv.astype(_BF16)
    tail = u[tb - (CONV_K - 1):tb, :]
    ext_ref[SUBLANES - (CONV_K - 1):SUBLANES, :] = tail
    conv_ref[0] = tail

    qx = proj(COL_QX, XA_WIDTH)
    zx = proj(COL_ZX, XA_WIDTH)
    for h in range(XA_HEADS):
        hs = h * XA_HD
        kh = mk_ref[0, :, hs:hs + XA_HD].astype(_BF16)
        vh = mv_ref[0, :, hs:hs + XA_HD].astype(_BF16)
        s = _dot_nt(qx[:, hs:hs + XA_HD].astype(_BF16), kh) * (XA_HD ** -0.5)
        e = jnp.exp(s - jnp.max(s, axis=1, keepdims=True))
        p = e * (1.0 / jnp.sum(e, axis=1, keepdims=True))
        y_xa = _dot(p.astype(_BF16), vh) * _silu(zx[:, hs:hs + XA_HD])
        ymix_ref[:, ML_WIDTH + CONV_CH + hs:ML_WIDTH + CONV_CH + hs + XA_HD] = y_xa.astype(_BF16)

    y_ref[0] = _out_tail(ymix_ref[...], wout_ref[...], x, gfinal_ref[...])


def _prompt_call(x, mk, mv, g_norm, w_all, b_if, g_head, conv_w, conv_b, w_out, g_final):
    b, t, _ = x.shape
    tb = PROMPT_BLOCK
    const2 = lambda i, j: (0, 0)
    resident = functools.partial(pl.BlockSpec, index_map=const2, pipeline_mode=pl.Buffered(1))
    per_batch3 = lambda i, j: (i, 0, 0)
    return pl.pallas_call(
        _prompt_kernel,
        grid=(b, t // tb),
        in_specs=[
            pl.BlockSpec((1, tb, D_MODEL), lambda i, j: (i, j, 0)),
            pl.BlockSpec((1, N_MEM, XA_WIDTH), per_batch3),
            pl.BlockSpec((1, N_MEM, XA_WIDTH), per_batch3),
            pl.BlockSpec((1, D_MODEL), const2),
            resident((D_MODEL, PROJ_COLS)),
            pl.BlockSpec((1, LANES), const2),
            pl.BlockSpec((1, ML_WIDTH), const2),
            pl.BlockSpec((CONV_K, CONV_CH), const2),
            pl.BlockSpec((1, CONV_CH), const2),
            resident((MIX_WIDTH, D_MODEL)),
            pl.BlockSpec((1, D_MODEL), const2),
        ],
        out_specs=[
            pl.BlockSpec((1, tb, D_MODEL), lambda i, j: (i, j, 0)),
            pl.BlockSpec((1, ML_HEADS, ML_HD, ML_HD), lambda i, j: (i, 0, 0, 0)),
            pl.BlockSpec((1, ML_HEADS, ML_HD), per_batch3),
            pl.BlockSpec((1, SUBLANES, LANES), per_batch3),
            pl.BlockSpec((1, CONV_K - 1, CONV_CH), per_batch3),
        ],
        out_shape=[
            jax.ShapeDtypeStruct((b, t, D_MODEL), _F32),
            jax.ShapeDtypeStruct((b, ML_HEADS, ML_HD, ML_HD), _F32),
            jax.ShapeDtypeStruct((b, ML_HEADS, ML_HD), _F32),
            jax.ShapeDtypeStruct((b, SUBLANES, LANES), _F32),
            jax.ShapeDtypeStruct((b, CONV_K - 1, CONV_CH), _F32),
        ],
        scratch_shapes=[
            pltpu.VMEM((SUBLANES + tb, CONV_CH), _F32),
            pltpu.VMEM((tb, MIX_WIDTH), _BF16),
            pltpu.VMEM((tb, PROJ_COLS), _F32),
        ],
        compiler_params=pltpu.CompilerParams(
            dimension_semantics=("arbitrary", "arbitrary"), vmem_limit_bytes=VMEM_LIMIT),
        name="prompt_layer",
    )(x, mk, mv, g_norm, w_all, b_if, g_head, conv_w, conv_b, w_out, g_final)


def _inproj_kernel(x_ref, gnorm_ref, w_ref, bif_ref, m0_ref, proj_ref, cols_ref, rows_ref, xb_ref):
    j = pl.program_id(0)

    @pl.when(j == 0)
    def _():
        xb_ref[...] = _rms(x_ref[...], gnorm_ref[...]).astype(_BF16)

    p = _dot(xb_ref[...], w_ref[...])
    proj_ref[...] = p

    @pl.when(j == COL_IF // INPROJ_TILE)
    def _():
        off = COL_IF % INPROJ_TILE
        slab = (p[:, off:off + LANES] + bif_ref[...]).T[0:SUBLANES, :]
        a, r, m_t, w_inter, e_negm, w_s, w_c = _gate_rows(slab, m0_ref[...], DEC_SEQ_LEN)
        cols_ref[...] = _gate_cols(r, w_inter, e_negm, w_s, w_c)
        rows_ref[0:SUBLANES, :] = a
        rows_ref[SUBLANES:2 * SUBLANES, :] = m_t


def _inproj_call(x, g_norm, w_all, b_if, m0_rows):
    n_tok = x.shape[0]
    const = lambda j: (0, 0)
    return pl.pallas_call(
        _inproj_kernel,
        grid=(PROJ_COLS // INPROJ_TILE,),
        in_specs=[
            pl.BlockSpec((n_tok, D_MODEL), const),
            pl.BlockSpec((1, D_MODEL), const),
            pl.BlockSpec((D_MODEL, INPROJ_TILE), lambda j: (0, j)),
            pl.BlockSpec((1, LANES), const),
            pl.BlockSpec((SUBLANES, n_tok), const),
        ],
        out_specs=[
            pl.BlockSpec((n_tok, INPROJ_TILE), lambda j: (0, j)),
            pl.BlockSpec((n_tok, LANES), const),
            pl.BlockSpec((2 * SUBLANES, n_tok), const),
        ],
        out_shape=[
            jax.ShapeDtypeStruct((n_tok, PROJ_COLS), _F32),
            jax.ShapeDtypeStruct((n_tok, LANES), _F32),
            jax.ShapeDtypeStruct((2 * SUBLANES, n_tok), _F32),
        ],
        scratch_shapes=[pltpu.VMEM((n_tok, D_MODEL), _BF16)],
        compiler_params=pltpu.CompilerParams(
            dimension_semantics=("arbitrary",), vmem_limit_bytes=VMEM_LIMIT),
        name="sample_inproj",
    )(x, g_norm, w_all, b_if, m0_rows)


def _sample_kernel(proj_ref, cols_ref, rows_ref, c0_ref, n0_ref, conv0_ref, mk_ref, mv_ref,
                   ghead_ref, convw_ref, convb_ref,
                   ymix_ref, c_ref, n_ref, conv_ref,
                   sv_ref, rs_ref, ext_ref):
    nt = SAMPLE_BLOCK_SEQS * DEC_SEQ_LEN
    j = pl.program_id(1)

    @pl.when(j == 0)
    def _():
        row = lax.broadcasted_iota(jnp.int32, (nt, nt), 0)
        col = lax.broadcasted_iota(jnp.int32, (nt, nt), 1)
        mask = (col <= row) & ((row // DEC_SEQ_LEN) == (col // DEC_SEQ_LEN))
        a = rows_ref[0:SUBLANES, :]
        for h in range(ML_HEADS):
            hs = h * ML_HD
            qb = proj_ref[:, COL_Q + hs:COL_Q + hs + ML_HD].astype(_BF16)
            kb = (proj_ref[:, COL_K + hs:COL_K + hs + ML_HD] * (ML_HD ** -0.5)).astype(_BF16)
            vb = proj_ref[:, COL_V + hs:COL_V + hs + ML_HD].astype(_BF16)
            r_col = cols_ref[:, GC_R + h:GC_R + h + 1]
            decay = jnp.exp(jnp.where(mask, a[h:h + 1, :] - r_col, -jnp.inf))
            s_mat = _dot_nt(qb, kb) * decay
            sv_ref[:, hs:hs + ML_HD] = _dot(s_mat.astype(_BF16), vb)
            rs_ref[:, h:h + 1] = jnp.sum(s_mat, axis=1, keepdims=True)

    xa_rows = XA_HEADS * DEC_SEQ_LEN
    own_head = (lax.broadcasted_iota(jnp.int32, (xa_rows, XA_KV_ROWS), 0) // DEC_SEQ_LEN
                == lax.broadcasted_iota(jnp.int32, (xa_rows, XA_KV_ROWS), 1) % XA_HEADS)

    for sl in range(SAMPLE_STEP_SEQS):
        ro = pl.multiple_of((j * SAMPLE_STEP_SEQS + sl) * DEC_SEQ_LEN, DEC_SEQ_LEN)
        rows = pl.ds(ro, DEC_SEQ_LEN)
        gc = cols_ref[rows, :]
        for h in range(ML_HEADS):
            hs = h * ML_HD
            q = proj_ref[rows, COL_Q + hs:COL_Q + hs + ML_HD]
            k = proj_ref[rows, COL_K + hs:COL_K + hs + ML_HD] * (ML_HD ** -0.5)
            vb = proj_ref[rows, COL_V + hs:COL_V + hs + ML_HD].astype(_BF16)
            wi_col = gc[:, GC_WI + h:GC_WI + h + 1]
            en_col = gc[:, GC_EN + h:GC_EN + h + 1]
            ws_col = gc[:, GC_WS + h:GC_WS + h + 1]
            wc = gc[0:1, GC_WC + h:GC_WC + h + 1]
            c_old = c0_ref[sl, h]
            n_old = n0_ref[sl, h:h + 1, :]
            num = wi_col * _dot(q.astype(_BF16), c_old.astype(_BF16)) + sv_ref[rows, hs:hs + ML_HD]
            den = wi_col * jnp.sum(q * n_old, axis=1, keepdims=True) + rs_ref[rows, h:h + 1]
            hh = num * (1.0 / jnp.maximum(jnp.abs(den), en_col))
            kw = k * ws_col
            c_ref[sl, h] = wc * c_old + _dot_tn(kw.astype(_BF16), vb)
            n_ref[sl, h:h + 1, :] = wc * n_old + jnp.sum(kw, axis=0, keepdims=True)
            y_ml = _ml_head_out(hh, proj_ref[rows, COL_O + hs:COL_O + hs + ML_HD],
                                proj_ref[rows, COL_ZML + hs:COL_ZML + hs + ML_HD], ghead_ref[:, hs:hs + ML_HD])
            ymix_ref[rows, hs:hs + ML_HD] = y_ml

        u = proj_ref[rows, COL_CG:COL_CG + CONV_CH] * proj_ref[rows, COL_XC:COL_XC + CONV_CH]
        ext_ref[sl, SUBLANES - (CONV_K - 1):SUBLANES, :] = conv0_ref[sl]
        ext_ref[sl, SUBLANES:2 * SUBLANES, :] = u
        u1 = ext_ref[sl, SUBLANES - 1:2 * SUBLANES - 1, :]
        u2 = ext_ref[sl, SUBLANES - 2:2 * SUBLANES - 2, :]
        yc = convb_ref[...] + convw_ref[0:1, :] * u2 + convw_ref[1:2, :] * u1 + convw_ref[2:3, :] * u
        y_cv = (proj_ref[rows, COL_BG:COL_BG + CONV_CH] * yc
                * _silu(proj_ref[rows, COL_ZCV:COL_ZCV + CONV_CH]))
        ymix_ref[rows, ML_WIDTH:ML_WIDTH + CONV_CH] = y_cv
        conv_ref[sl] = u[DEC_SEQ_LEN - (CONV_K - 1):DEC_SEQ_LEN, :]

        q4 = jnp.concatenate(
            [proj_ref[rows, COL_QX + h * XA_HD:COL_QX + (h + 1) * XA_HD] for h in range(XA_HEADS)], axis=0)
        s = _dot_nt(q4.astype(_BF16), mk_ref[sl].astype(_BF16)) * (XA_HD ** -0.5)
        s = jnp.where(own_head, s, -jnp.inf)
        e = jnp.exp(s - jnp.max(s, axis=1, keepdims=True))
        p = e * (1.0 / jnp.sum(e, axis=1, keepdims=True))
        y4 = _dot(p.astype(_BF16), mv_ref[sl].astype(_BF16))
        for h in range(XA_HEADS):
            hs = h * XA_HD
            y_xa = y4[h * DEC_SEQ_LEN:(h + 1) * DEC_SEQ_LEN, :] * _silu(proj_ref[rows, COL_ZX + hs:COL_ZX + hs + XA_HD])
            c0 = ML_WIDTH + CONV_CH + hs
            ymix_ref[rows, c0:c0 + XA_HD] = y_xa


def _sample_call(proj, cols, rows, c0, n0, conv0, mk, mv, g_head, conv_w, conv_b):
    n_seq = c0.shape[0]
    nt = SAMPLE_BLOCK_SEQS * DEC_SEQ_LEN
    steps = SAMPLE_BLOCK_SEQS // SAMPLE_STEP_SEQS
    ss = SAMPLE_STEP_SEQS
    const2 = lambda i, j: (0, 0)
    blk2 = lambda i, j: (i, 0)
    seq3 = lambda i, j: (i * steps + j, 0, 0)
    seq4 = lambda i, j: (i * steps + j, 0, 0, 0)
    return pl.pallas_call(
        _sample_kernel,
        grid=(n_seq // SAMPLE_BLOCK_SEQS, steps),
        in_specs=[
            pl.BlockSpec((nt, PROJ_COLS), blk2),
            pl.BlockSpec((nt, LANES), blk2),
            pl.BlockSpec((2 * SUBLANES, nt), lambda i, j: (0, i)),
            pl.BlockSpec((ss, ML_HEADS, ML_HD, ML_HD), seq4),
            pl.BlockSpec((ss, ML_HEADS, ML_HD), seq3),
            pl.BlockSpec((ss, CONV_K - 1, CONV_CH), seq3),
            pl.BlockSpec((ss, XA_KV_ROWS, XA_HD), seq3),
            pl.BlockSpec((ss, XA_KV_ROWS, XA_HD), seq3),
            pl.BlockSpec((1, ML_WIDTH), const2),
            pl.BlockSpec((CONV_K, CONV_CH), const2),
            pl.BlockSpec((1, CONV_CH), const2),
        ],
        out_specs=[
            pl.BlockSpec((nt, MIX_WIDTH), blk2),
            pl.BlockSpec((ss, ML_HEADS, ML_HD, ML_HD), seq4),
            pl.BlockSpec((ss, ML_HEADS, ML_HD), seq3),
            pl.BlockSpec((ss, CONV_K - 1, CONV_CH), seq3),
        ],
        out_shape=[
            jax.ShapeDtypeStruct((n_seq * DEC_SEQ_LEN, MIX_WIDTH), _F32),
            jax.ShapeDtypeStruct(c0.shape, _F32),
            jax.ShapeDtypeStruct(n0.shape, _F32),
            jax.ShapeDtypeStruct(conv0.shape, _F32),
        ],
        scratch_shapes=[
            pltpu.VMEM((nt, ML_WIDTH), _F32),
            pltpu.VMEM((nt, LANES), _F32),
            pltpu.VMEM((ss, 2 * SUBLANES, CONV_CH), _F32),
        ],
        compiler_params=pltpu.CompilerParams(
            dimension_semantics=("arbitrary", "arbitrary"), vmem_limit_bytes=VMEM_LIMIT),
        name="sample_state",
    )(proj, cols, rows, c0, n0, conv0, mk, mv, g_head, conv_w, conv_b)


def _outproj_kernel(ymix_ref, x_ref, wout_ref, gfinal_ref, y_ref):
    y_ref[...] = _out_tail(ymix_ref[...].astype(_BF16), wout_ref[...], x_ref[...], gfinal_ref[...])


def _outproj_call(ymix, x, w_out, g_final):
    n_tok = x.shape[0]
    rb = 256
    const = lambda i: (0, 0)
    return pl.pallas_call(
        _outproj_kernel,
        grid=(n_tok // rb,),
        in_specs=[
            pl.BlockSpec((rb, MIX_WIDTH), lambda i: (i, 0)),
            pl.BlockSpec((rb, D_MODEL), lambda i: (i, 0)),
            pl.BlockSpec((MIX_WIDTH, D_MODEL), const),
            pl.BlockSpec((1, D_MODEL), const),
        ],
        out_specs=pl.BlockSpec((rb, D_MODEL), lambda i: (i, 0)),
        out_shape=jax.ShapeDtypeStruct((n_tok, D_MODEL), _F32),
        compiler_params=pltpu.CompilerParams(dimension_semantics=("arbitrary",)),
        name="sample_outproj",
    )(ymix, x, w_out, g_final)


def kernel(x_prompt, x_sample, state_mlstm_C, state_mlstm_n, state_mlstm_m, state_conv, cache_mem_k, cache_mem_v, mem_prompt, g_norm, w_in, b_if, g_head, conv_w, conv_b, g_mem, w_mem_kv, w_out, g_final):
    assert w_in.shape[0] == 1, "single-layer trunk"
    bp, tp, _ = x_prompt.shape
    bs, ts, _ = x_sample.shape
    assert ts == DEC_SEQ_LEN and tp % PROMPT_BLOCK == 0 and bs % SAMPLE_BLOCK_SEQS == 0

    w = w_in[0]
    w_if = jnp.pad(w[:, COL_IF:COL_IF + N_IF], ((0, 0), (0, LANES - N_IF)))
    w_all = jnp.concatenate([w[:, :COL_IF], w_if, w[:, COL_IF + N_IF:]], axis=1).astype(_BF16)
    w_out_b = w_out[0].astype(_BF16)
    w_kv_b = w_mem_kv[0].astype(_BF16)
    b_if_p = jnp.pad(b_if[0][None, :], ((0, 0), (0, LANES - N_IF)))
    row = lambda v: v.reshape(1, -1)

    mk_p, mv_p, mk_rows, mv_rows = _memkv_call(mem_prompt, row(g_mem[0]), w_kv_b)
    y_p, c_p, n_p, m_p, conv_p = _prompt_call(
        x_prompt, mk_p, mv_p, row(g_norm[0]), w_all, b_if_p, row(g_head[0]), conv_w[0], row(conv_b[0]),
        w_out_b, row(g_final))

    n_tok = bs * ts
    xs = x_sample.reshape(n_tok, D_MODEL)
    m0_rows = jnp.pad(jnp.repeat(state_mlstm_m[0].T, ts, axis=1), ((0, SUBLANES - ML_HEADS), (0, 0)))
    proj, cols, rows = _inproj_call(xs, row(g_norm[0]), w_all, b_if_p, m0_rows)
    ymix, c_s, n_s, conv_s = _sample_call(
        proj, cols, rows, state_mlstm_C[0], state_mlstm_n[0], state_conv[0],
        cache_mem_k[0].reshape(bs, XA_KV_ROWS, XA_HD), cache_mem_v[0].reshape(bs, XA_KV_ROWS, XA_HD),
        row(g_head[0]), conv_w[0], row(conv_b[0]))
    y_s = _outproj_call(ymix, xs, w_out_b, row(g_final))
    m_s = rows[SUBLANES:SUBLANES + ML_HEADS, ts - 1::ts].T

    kv_shape = (1, bp, N_MEM, XA_HEADS, XA_HD)
    return (y_p, y_s.reshape(bs, ts, D_MODEL),
            c_p[None], n_p[None], m_p[:, :ML_HEADS, 0][None], conv_p[None],
            mk_rows.reshape(kv_shape), mv_rows.reshape(kv_shape),
            c_s[None], n_s[None], m_s[None], conv_s[None])
```

```python
import functools

import jax
import jax.numpy as jnp
from jax import lax
from jax.experimental import pallas as pl
from jax.experimental.pallas import tpu as pltpu

D_MODEL = 1024
ML_HEADS = 4
ML_HD = 256
ML_WIDTH = ML_HEADS * ML_HD
CONV_CH = 512
CONV_K = 3
XA_HEADS = 4
XA_HD = 128
XA_WIDTH = XA_HEADS * XA_HD
N_MEM = 256
XA_KV_ROWS = N_MEM * XA_HEADS
DEC_SEQ_LEN = 8
MIX_WIDTH = ML_WIDTH + CONV_CH + XA_WIDTH
EPS = 1e-6

SUBLANES = 8
LANES = 128

ML_HEAD_COLS = 4 * ML_HD
OFF_Q = 0
OFF_K = ML_HD
OFF_V = 2 * ML_HD
OFF_O = 3 * ML_HD
COL_ZML = ML_HEADS * ML_HEAD_COLS
COL_IF = 5 * ML_WIDTH
COL_BG = COL_IF + LANES
COL_CG = COL_BG + CONV_CH
COL_XC = COL_CG + CONV_CH
COL_ZCV = COL_XC + CONV_CH
COL_QX = COL_ZCV + CONV_CH
COL_ZX = COL_QX + XA_WIDTH
PROJ_COLS = COL_ZX + XA_WIDTH
N_IF = 2 * ML_HEADS

GC_R = 0 * SUBLANES
GC_WI = 1 * SUBLANES
GC_EN = 2 * SUBLANES
GC_WS = 3 * SUBLANES
GC_WC = 4 * SUBLANES
GC_USED = 5 * SUBLANES

PROMPT_BLOCK = 256
SAMPLE_BLOCK_SEQS = 16
SAMPLE_STEP_SEQS = 4
INPROJ_TILE = 640
VMEM_LIMIT = 60 * 1024 * 1024

_F32 = jnp.float32
_BF16 = jnp.bfloat16


def _dot(a, b):
    return jnp.dot(a, b, preferred_element_type=_F32)


def _dot_nt(a, b):
    return lax.dot_general(a, b, (((1,), (1,)), ((), ())), preferred_element_type=_F32)


def _dot_tn(a, b):
    return lax.dot_general(a, b, (((0,), (0,)), ((), ())), preferred_element_type=_F32)


def _rms(x, g):
    r = lax.rsqrt(jnp.mean(x * x, axis=-1, keepdims=True) + EPS)
    return (x * r) * g


def _silu(x):
    return x * jax.nn.sigmoid(x)


def _log_sigmoid(x):
    return jnp.minimum(x, 0.0) - jnp.log1p(jnp.exp(-jnp.abs(x)))


def _gate_rows(slab, m_prev, seg):
    n = slab.shape[1]
    pos = lax.broadcasted_iota(jnp.int32, slab.shape, 1) & (seg - 1)

    def scan(x, op, fill, reverse=False):
        k = 1
        while k < seg:
            if reverse:
                shifted, ok = pltpu.roll(x, n - k, axis=1), pos < seg - k
            else:
                shifted, ok = pltpu.roll(x, k, axis=1), pos >= k
            x = op(x, jnp.where(ok, shifted, fill))
            k *= 2
        return x

    logf = _log_sigmoid(pltpu.roll(slab, ML_HEADS, axis=0))
    f_cum = scan(logf, jnp.add, 0.0)
    a = slab - f_cum
    r = jnp.maximum(scan(a, jnp.maximum, -jnp.inf), m_prev)
    r_last = scan(r, jnp.maximum, -jnp.inf, reverse=True)
    m_t = f_cum + r
    w_inter = jnp.exp(m_prev - r)
    e_negm = jnp.exp(-m_t)
    w_s = jnp.exp(a - r_last)
    w_c = jnp.exp(m_prev - r_last)
    return a, r, m_t, w_inter, e_negm, w_s, w_c


def _gate_cols(r, w_inter, e_negm, w_s, w_c):
    n = r.shape[1]
    pad = jnp.zeros((LANES - GC_USED, n), _F32)
    return jnp.concatenate([r, w_inter, e_negm, w_s, w_c, pad], axis=0).T


def _ml_head_out(hh, o_pre, z, g_head):
    hh = jax.nn.sigmoid(o_pre) * hh
    return _rms(hh, g_head) * _silu(z)


def _out_tail(ymix_bf16, w_out, x, g_final):
    y = _dot(ymix_bf16, w_out) + x
    return _rms(y, g_final)


def _memkv_kernel(mem_ref, g_ref, w_ref, mk_ref, mv_ref, mk_rows_ref, mv_rows_ref):
    xn = _rms(mem_ref[0], g_ref[...]).astype(_BF16)
    kv = _dot(xn, w_ref[...])
    mk_ref[0] = kv[:, :XA_WIDTH]
    mv_ref[0] = kv[:, XA_WIDTH:]
    for h in range(XA_HEADS):
        head_rows = pl.ds(h, N_MEM, stride=XA_HEADS)
        mk_rows_ref[0, head_rows, :] = kv[:, h * XA_HD:(h + 1) * XA_HD]
        mv_rows_ref[0, head_rows, :] = kv[:, XA_WIDTH + h * XA_HD:XA_WIDTH + (h + 1) * XA_HD]


def _memkv_call(mem, g_mem, w_kv):
    b = mem.shape[0]
    const = lambda i: (0, 0)
    per_batch = lambda i: (i, 0, 0)
    return pl.pallas_call(
        _memkv_kernel,
        grid=(b,),
        in_specs=[
            pl.BlockSpec((1, N_MEM, D_MODEL), per_batch),
            pl.BlockSpec((1, D_MODEL), const),
            pl.BlockSpec((D_MODEL, 2 * XA_WIDTH), const),
        ],
        out_specs=[
            pl.BlockSpec((1, N_MEM, XA_WIDTH), per_batch),
            pl.BlockSpec((1, N_MEM, XA_WIDTH), per_batch),
            pl.BlockSpec((1, XA_KV_ROWS, XA_HD), per_batch),
            pl.BlockSpec((1, XA_KV_ROWS, XA_HD), per_batch),
        ],
        out_shape=[jax.ShapeDtypeStruct((b, N_MEM, XA_WIDTH), _F32)] * 2
        + [jax.ShapeDtypeStruct((b, XA_KV_ROWS, XA_HD), _F32)] * 2,
        compiler_params=pltpu.CompilerParams(dimension_semantics=("arbitrary",)),
        name="memkv",
    )(mem, g_mem, w_kv)


N_FILLERS = 8


def _finish_block(proj, filler, x, mk_ref, mv_ref, bif_ref, ghead_ref, convw_ref, convb_ref, wout_ref, gfinal_ref,
                  y_ref, c_ref, n_ref, m_ref, conv_ref, ext_ref, acc_ref):
    tb = PROMPT_BLOCK

    def out_part(y_bf16, row0):
        return _dot(y_bf16, wout_ref[row0:row0 + y_bf16.shape[1], :])

    gates = proj(COL_IF, LANES)
    heads_qkvo = [None] * ML_HEADS
    heads_qkvo[0] = proj(0, ML_HEAD_COLS)
    slab = (gates + bif_ref[...]).T[0:SUBLANES, :]
    m_prev8 = m_ref[0]
    m_prev = jnp.concatenate([m_prev8] * (tb // LANES), axis=1)
    a, r, m_t, w_inter, e_negm, w_s, w_c = _gate_rows(slab, m_prev, tb)
    cols = _gate_cols(r, w_inter, e_negm, w_s, w_c)
    m_ref[0] = jnp.broadcast_to(m_t[:, tb - 1:tb], (SUBLANES, LANES))

    row = lax.broadcasted_iota(jnp.int32, (tb, tb), 0)
    col = lax.broadcasted_iota(jnp.int32, (tb, tb), 1)
    causal = col <= row

    def head_first(h):
        p = heads_qkvo[h]
        q = p[:, OFF_Q:OFF_Q + ML_HD]
        k = p[:, OFF_K:OFF_K + ML_HD] * (ML_HD ** -0.5)
        qb = q.astype(_BF16)
        c_old = c_ref[0, h]
        qk = _dot_nt(qb, k.astype(_BF16))
        qc = _dot(qb, c_old.astype(_BF16))
        return q, k, c_old, qk, qc

    def head_second(h, first):
        q, k, c_old, qk, qc = first
        p = heads_qkvo[h]
        vb = p[:, OFF_V:OFF_V + ML_HD].astype(_BF16)
        r_col = cols[:, GC_R + h:GC_R + h + 1]
        wi_col = cols[:, GC_WI + h:GC_WI + h + 1]
        en_col = cols[:, GC_EN + h:GC_EN + h + 1]
        ws_col = cols[:, GC_WS + h:GC_WS + h + 1]
        wc = w_c[h:h + 1, 0:1]
        s_mat = qk * jnp.exp(jnp.where(causal, a[h:h + 1, :] - r_col, -jnp.inf))
        kw = k * ws_col
        sv = _dot(s_mat.astype(_BF16), vb)
        c_ref[0, h] = wc * c_old + _dot_tn(kw.astype(_BF16), vb)
        n_old = n_ref[0, h:h + 1, :]
        n_ref[0, h:h + 1, :] = wc * n_old + jnp.sum(kw, axis=0, keepdims=True)
        num = wi_col * qc + sv
        den = wi_col * jnp.sum(q * n_old, axis=1, keepdims=True) + jnp.sum(s_mat, axis=1, keepdims=True)
        hh = num * (1.0 / jnp.maximum(jnp.abs(den), en_col))
        return jax.nn.sigmoid(p[:, OFF_O:OFF_O + ML_HD]) * hh

    def head_out(h, gated, z_all):
        hs = h * ML_HD
        y_ml = _rms(gated, ghead_ref[:, hs:hs + ML_HD]) * _silu(z_all[:, hs:hs + ML_HD])
        return out_part(y_ml.astype(_BF16), hs)

    def xa_scores(h):
        hs = h * XA_HD
        kh = mk_ref[0, :, hs:hs + XA_HD].astype(_BF16)
        return _dot_nt(xa_in[:, hs:hs + XA_HD].astype(_BF16), kh) * (XA_HD ** -0.5)

    def xa_out(h, s):
        hs = h * XA_HD
        vh = mv_ref[0, :, hs:hs + XA_HD].astype(_BF16)
        e = jnp.exp(s - jnp.max(s, axis=1, keepdims=True))
        p = e * (1.0 / jnp.sum(e, axis=1, keepdims=True))
        y_xa = _dot(p.astype(_BF16), vh) * _silu(xa_in[:, XA_WIDTH + hs:XA_WIDTH + hs + XA_HD])
        return y_xa.astype(_BF16)

    xa_in = proj(COL_QX, 2 * XA_WIDTH)
    conv_in = proj(COL_BG, 4 * CONV_CH)
    for h in range(1, ML_HEADS):
        heads_qkvo[h] = proj(h * ML_HEAD_COLS, ML_HEAD_COLS)
    z_all = proj(COL_ZML, ML_WIDTH)

    s0, s1, s2, s3 = (xa_scores(h) for h in range(XA_HEADS))
    filler()
    first0 = head_first(0)
    first1 = head_first(1)
    y_xa0 = xa_out(0, s0)
    y_xa1 = xa_out(1, s1)
    filler()
    first2 = head_first(2)
    first3 = head_first(3)
    y_xa2 = xa_out(2, s2)
    y_xa3 = xa_out(3, s3)
    filler()

    b_g = conv_in[:, 0:CONV_CH]
    u = conv_in[:, CONV_CH:2 * CONV_CH] * conv_in[:, 2 * CONV_CH:3 * CONV_CH]
    ext_ref[SUBLANES:SUBLANES + tb, :] = u
    u1 = ext_ref[SUBLANES - 1:SUBLANES - 1 + tb, :]
    u2 = ext_ref[SUBLANES - 2:SUBLANES - 2 + tb, :]
    yc = convb_ref[...] + convw_ref[0:1, :] * u2 + convw_ref[1:2, :] * u1 + convw_ref[2:3, :] * u
    y_cv = b_g * yc * _silu(conv_in[:, 3 * CONV_CH:4 * CONV_CH])
    tail = u[tb - (CONV_K - 1):tb, :]
    ext_ref[SUBLANES - (CONV_K - 1):SUBLANES, :] = tail
    conv_ref[0] = tail
    gated0 = head_second(0, first0)
    filler()
    gated1 = head_second(1, first1)
    acc_ref[...] = out_part(jnp.concatenate([y_cv.astype(_BF16), y_xa0, y_xa1, y_xa2, y_xa3], axis=1), ML_WIDTH)
    filler()
    gated2 = head_second(2, first2)
    acc = head_out(0, gated0, z_all)
    filler()
    gated3 = head_second(3, first3)
    acc = acc + head_out(1, gated1, z_all)
    filler()
    acc = acc + head_out(2, gated2, z_all)
    acc = acc + head_out(3, gated3, z_all)
    filler()

    y_ref[0] = _rms(acc_ref[...] + acc + x, gfinal_ref[...])


def _prompt_kernel(xnext_ref, x_ref, mk_ref, mv_ref, gnorm_ref, w_ref, bif_ref, ghead_ref, convw_ref, convb_ref,
                   wout_ref, gfinal_ref,
                   y_ref, c_ref, n_ref, m_ref, conv_ref,
                   proj_even_ref, proj_odd_ref, ext_ref, acc_ref, *, blocks_per_seq):
    s = pl.program_id(0)

    @pl.when(s == 0)
    def _():
        proj_odd_ref[...] = jnp.zeros_like(proj_odd_ref)

    @pl.when((s == 0) | (lax.rem(s + blocks_per_seq - 1, blocks_per_seq) == 0))
    def _():
        c_ref[...] = jnp.zeros_like(c_ref)
        n_ref[...] = jnp.zeros_like(n_ref)
        m_ref[...] = jnp.zeros_like(m_ref)
        ext_ref[0:SUBLANES, :] = jnp.zeros((SUBLANES, CONV_CH), _F32)

    def step(proj_w_ref, proj_r_ref):
        xb_next = _rms(xnext_ref[0], gnorm_ref[...]).astype(_BF16)
        chunk = PROJ_COLS // LANES // N_FILLERS * LANES
        starts = iter(range(0, chunk * N_FILLERS, chunk))

        def project_chunk():
            c0 = next(starts)
            c1 = PROJ_COLS if c0 == chunk * (N_FILLERS - 1) else c0 + chunk
            proj_w_ref[:, c0:c1] = _dot(xb_next, w_ref[:, c0:c1])

        _finish_block(lambda col, width: proj_r_ref[:, col:col + width], project_chunk, x_ref[0],
                      mk_ref, mv_ref, bif_ref, ghead_ref, convw_ref, convb_ref, wout_ref, gfinal_ref,
                      y_ref, c_ref, n_ref, m_ref, conv_ref, ext_ref, acc_ref)

    @pl.when(lax.rem(s, 2) == 0)
    def _():
        step(proj_even_ref, proj_odd_ref)

    @pl.when(lax.rem(s, 2) == 1)
    def _():
        step(proj_odd_ref, proj_even_ref)


def _prompt_call(x, mk, mv, g_norm, w_all, b_if, g_head, conv_w, conv_b, w_out, g_final):
    b, t, _ = x.shape
    tb = PROMPT_BLOCK
    nt = t // tb
    n_blocks = b * nt
    const2 = lambda s: (0, 0)
    resident = functools.partial(pl.BlockSpec, index_map=const2, pipeline_mode=pl.Buffered(1))
    nxt = lambda s: jnp.minimum(s, n_blocks - 1)
    cur = lambda s: jnp.maximum(s - 1, 0)
    cur_batch3 = lambda s: (cur(s) // nt, 0, 0)
    return pl.pallas_call(
        functools.partial(_prompt_kernel, blocks_per_seq=nt),
        grid=(n_blocks + 1,),
        in_specs=[
            pl.BlockSpec((1, tb, D_MODEL), lambda s: (nxt(s) // nt, nxt(s) % nt, 0)),
            pl.BlockSpec((1, tb, D_MODEL), lambda s: (cur(s) // nt, cur(s) % nt, 0)),
            pl.BlockSpec((1, N_MEM, XA_WIDTH), cur_batch3),
            pl.BlockSpec((1, N_MEM, XA_WIDTH), cur_batch3),
            pl.BlockSpec((1, D_MODEL), const2),
            resident((D_MODEL, PROJ_COLS)),
            pl.BlockSpec((1, LANES), const2),
            pl.BlockSpec((1, ML_WIDTH), const2),
            pl.BlockSpec((CONV_K, CONV_CH), const2),
            pl.BlockSpec((1, CONV_CH), const2),
            resident((MIX_WIDTH, D_MODEL)),
            pl.BlockSpec((1, D_MODEL), const2),
        ],
        out_specs=[
            pl.BlockSpec((1, tb, D_MODEL), lambda s: (cur(s) // nt, cur(s) % nt, 0)),
            pl.BlockSpec((1, ML_HEADS, ML_HD, ML_HD), lambda s: (cur(s) // nt, 0, 0, 0)),
            pl.BlockSpec((1, ML_HEADS, ML_HD), cur_batch3),
            pl.BlockSpec((1, SUBLANES, LANES), cur_batch3),
            pl.BlockSpec((1, CONV_K - 1, CONV_CH), cur_batch3),
        ],
        out_shape=[
            jax.ShapeDtypeStruct((b, t, D_MODEL), _F32),
            jax.ShapeDtypeStruct((b, ML_HEADS, ML_HD, ML_HD), _F32),
            jax.ShapeDtypeStruct((b, ML_HEADS, ML_HD), _F32),
            jax.ShapeDtypeStruct((b, SUBLANES, LANES), _F32),
            jax.ShapeDtypeStruct((b, CONV_K - 1, CONV_CH), _F32),
        ],
        scratch_shapes=[
            pltpu.VMEM((tb, PROJ_COLS), _F32),
            pltpu.VMEM((tb, PROJ_COLS), _F32),
            pltpu.VMEM((SUBLANES + tb, CONV_CH), _F32),
            pltpu.VMEM((tb, D_MODEL), _F32),
        ],
        compiler_params=pltpu.CompilerParams(
            dimension_semantics=("arbitrary",), vmem_limit_bytes=VMEM_LIMIT),
        name="prompt_layer",
    )(x, x, mk, mv, g_norm, w_all, b_if, g_head, conv_w, conv_b, w_out, g_final)


def _inproj_kernel(x_ref, gnorm_ref, w_ref, bif_ref, m0_ref, proj_ref, cols_ref, rows_ref, xb_ref):
    j = pl.program_id(0)

    @pl.when(j == 0)
    def _():
        xb_ref[...] = _rms(x_ref[...], gnorm_ref[...]).astype(_BF16)

    p = _dot(xb_ref[...], w_ref[...])
    proj_ref[...] = p

    @pl.when(j == COL_IF // INPROJ_TILE)
    def _():
        off = COL_IF % INPROJ_TILE
        slab = (p[:, off:off + LANES] + bif_ref[...]).T[0:SUBLANES, :]
        a, r, m_t, w_inter, e_negm, w_s, w_c = _gate_rows(slab, m0_ref[...], DEC_SEQ_LEN)
        cols_ref[...] = _gate_cols(r, w_inter, e_negm, w_s, w_c)
        rows_ref[0:SUBLANES, :] = a
        rows_ref[SUBLANES:2 * SUBLANES, :] = m_t


def _inproj_call(x, g_norm, w_all, b_if, m0_rows):
    n_tok = x.shape[0]
    const = lambda j: (0, 0)
    return pl.pallas_call(
        _inproj_kernel,
        grid=(PROJ_COLS // INPROJ_TILE,),
        in_specs=[
            pl.BlockSpec((n_tok, D_MODEL), const),
            pl.BlockSpec((1, D_MODEL), const),
            pl.BlockSpec((D_MODEL, INPROJ_TILE), lambda j: (0, j)),
            pl.BlockSpec((1, LANES), const),
            pl.BlockSpec((SUBLANES, n_tok), const),
        ],
        out_specs=[
            pl.BlockSpec((n_tok, INPROJ_TILE), lambda j: (0, j)),
            pl.BlockSpec((n_tok, LANES), const),
            pl.BlockSpec((2 * SUBLANES, n_tok), const),
        ],
        out_shape=[
            jax.ShapeDtypeStruct((n_tok, PROJ_COLS), _F32),
            jax.ShapeDtypeStruct((n_tok, LANES), _F32),
            jax.ShapeDtypeStruct((2 * SUBLANES, n_tok), _F32),
        ],
        scratch_shapes=[pltpu.VMEM((n_tok, D_MODEL), _BF16)],
        compiler_params=pltpu.CompilerParams(
            dimension_semantics=("arbitrary",), vmem_limit_bytes=VMEM_LIMIT),
        name="sample_inproj",
    )(x, g_norm, w_all, b_if, m0_rows)


def _sample_kernel(proj_ref, cols_ref, rows_ref, c0_ref, n0_ref, conv0_ref, mk_ref, mv_ref,
                   ghead_ref, convw_ref, convb_ref,
                   ymix_ref, c_ref, n_ref, conv_ref,
                   sv_ref, rs_ref, ext_ref):
    nt = SAMPLE_BLOCK_SEQS * DEC_SEQ_LEN
    j = pl.program_id(1)

    @pl.when(j == 0)
    def _():
        row = lax.broadcasted_iota(jnp.int32, (nt, nt), 0)
        col = lax.broadcasted_iota(jnp.int32, (nt, nt), 1)
        mask = (col <= row) & ((row // DEC_SEQ_LEN) == (col // DEC_SEQ_LEN))
        a = rows_ref[0:SUBLANES, :]
        for h in range(ML_HEADS):
            hs = h * ML_HD
            hc = h * ML_HEAD_COLS
            qb = proj_ref[:, hc + OFF_Q:hc + OFF_Q + ML_HD].astype(_BF16)
            kb = (proj_ref[:, hc + OFF_K:hc + OFF_K + ML_HD] * (ML_HD ** -0.5)).astype(_BF16)
            vb = proj_ref[:, hc + OFF_V:hc + OFF_V + ML_HD].astype(_BF16)
            r_col = cols_ref[:, GC_R + h:GC_R + h + 1]
            decay = jnp.exp(jnp.where(mask, a[h:h + 1, :] - r_col, -jnp.inf))
            s_mat = _dot_nt(qb, kb) * decay
            sv_ref[:, hs:hs + ML_HD] = _dot(s_mat.astype(_BF16), vb)
            rs_ref[:, h:h + 1] = jnp.sum(s_mat, axis=1, keepdims=True)

    xa_rows = XA_HEADS * DEC_SEQ_LEN
    own_head = (lax.broadcasted_iota(jnp.int32, (xa_rows, XA_KV_ROWS), 0) // DEC_SEQ_LEN
                == lax.broadcasted_iota(jnp.int32, (xa_rows, XA_KV_ROWS), 1) % XA_HEADS)

    for sl in range(SAMPLE_STEP_SEQS):
        ro = pl.multiple_of((j * SAMPLE_STEP_SEQS + sl) * DEC_SEQ_LEN, DEC_SEQ_LEN)
        rows = pl.ds(ro, DEC_SEQ_LEN)
        gc = cols_ref[rows, :]
        for h in range(ML_HEADS):
            hs = h * ML_HD
            hc = h * ML_HEAD_COLS
            q = proj_ref[rows, hc + OFF_Q:hc + OFF_Q + ML_HD]
            k = proj_ref[rows, hc + OFF_K:hc + OFF_K + ML_HD] * (ML_HD ** -0.5)
            vb = proj_ref[rows, hc + OFF_V:hc + OFF_V + ML_HD].astype(_BF16)
            wi_col = gc[:, GC_WI + h:GC_WI + h + 1]
            en_col = gc[:, GC_EN + h:GC_EN + h + 1]
            ws_col = gc[:, GC_WS + h:GC_WS + h + 1]
            wc = gc[0:1, GC_WC + h:GC_WC + h + 1]
            c_old = c0_ref[sl, h]
            n_old = n0_ref[sl, h:h + 1, :]
            num = wi_col * _dot(q.astype(_BF16), c_old.astype(_BF16)) + sv_ref[rows, hs:hs + ML_HD]
            den = wi_col * jnp.sum(q * n_old, axis=1, keepdims=True) + rs_ref[rows, h:h + 1]
            hh = num * (1.0 / jnp.maximum(jnp.abs(den), en_col))
            kw = k * ws_col
            c_ref[sl, h] = wc * c_old + _dot_tn(kw.astype(_BF16), vb)
            n_ref[sl, h:h + 1, :] = wc * n_old + jnp.sum(kw, axis=0, keepdims=True)
            y_ml = _ml_head_out(hh, proj_ref[rows, hc + OFF_O:hc + OFF_O + ML_HD],
                                proj_ref[rows, COL_ZML + hs:COL_ZML + hs + ML_HD], ghead_ref[:, hs:hs + ML_HD])
            ymix_ref[rows, hs:hs + ML_HD] = y_ml

        u = proj_ref[rows, COL_CG:COL_CG + CONV_CH] * proj_ref[rows, COL_XC:COL_XC + CONV_CH]
        ext_ref[sl, SUBLANES - (CONV_K - 1):SUBLANES, :] = conv0_ref[sl]
        ext_ref[sl, SUBLANES:2 * SUBLANES, :] = u
        u1 = ext_ref[sl, SUBLANES - 1:2 * SUBLANES - 1, :]
        u2 = ext_ref[sl, SUBLANES - 2:2 * SUBLANES - 2, :]
        yc = convb_ref[...] + convw_ref[0:1, :] * u2 + convw_ref[1:2, :] * u1 + convw_ref[2:3, :] * u
        y_cv = (proj_ref[rows, COL_BG:COL_BG + CONV_CH] * yc
                * _silu(proj_ref[rows, COL_ZCV:COL_ZCV + CONV_CH]))
        ymix_ref[rows, ML_WIDTH:ML_WIDTH + CONV_CH] = y_cv
        conv_ref[sl] = u[DEC_SEQ_LEN - (CONV_K - 1):DEC_SEQ_LEN, :]

        q4 = jnp.concatenate(
            [proj_ref[rows, COL_QX + h * XA_HD:COL_QX + (h + 1) * XA_HD] for h in range(XA_HEADS)], axis=0)
        s = _dot_nt(q4.astype(_BF16), mk_ref[sl].astype(_BF16)) * (XA_HD ** -0.5)
        s = jnp.where(own_head, s, -jnp.inf)
        e = jnp.exp(s - jnp.max(s, axis=1, keepdims=True))
        p = e * (1.0 / jnp.sum(e, axis=1, keepdims=True))
        y4 = _dot(p.astype(_BF16), mv_ref[sl].astype(_BF16))
        for h in range(XA_HEADS):
            hs = h * XA_HD
            y_xa = y4[h * DEC_SEQ_LEN:(h + 1) * DEC_SEQ_LEN, :] * _silu(proj_ref[rows, COL_ZX + hs:COL_ZX + hs + XA_HD])
            c0 = ML_WIDTH + CONV_CH + hs
            ymix_ref[rows, c0:c0 + XA_HD] = y_xa


def _sample_call(proj, cols, rows, c0, n0, conv0, mk, mv, g_head, conv_w, conv_b):
    n_seq = c0.shape[0]
    nt = SAMPLE_BLOCK_SEQS * DEC_SEQ_LEN
    steps = SAMPLE_BLOCK_SEQS // SAMPLE_STEP_SEQS
    ss = SAMPLE_STEP_SEQS
    const2 = lambda i, j: (0, 0)
    blk2 = lambda i, j: (i, 0)
    seq3 = lambda i, j: (i * steps + j, 0, 0)
    seq4 = lambda i, j: (i * steps + j, 0, 0, 0)
    return pl.pallas_call(
        _sample_kernel,
        grid=(n_seq // SAMPLE_BLOCK_SEQS, steps),
        in_specs=[
            pl.BlockSpec((nt, PROJ_COLS), blk2),
            pl.BlockSpec((nt, LANES), blk2),
            pl.BlockSpec((2 * SUBLANES, nt), lambda i, j: (0, i)),
            pl.BlockSpec((ss, ML_HEADS, ML_HD, ML_HD), seq4),
            pl.BlockSpec((ss, ML_HEADS, ML_HD), seq3),
            pl.BlockSpec((ss, CONV_K - 1, CONV_CH), seq3),
            pl.BlockSpec((ss, XA_KV_ROWS, XA_HD), seq3),
            pl.BlockSpec((ss, XA_KV_ROWS, XA_HD), seq3),
            pl.BlockSpec((1, ML_WIDTH), const2),
            pl.BlockSpec((CONV_K, CONV_CH), const2),
            pl.BlockSpec((1, CONV_CH), const2),
        ],
        out_specs=[
            pl.BlockSpec((nt, MIX_WIDTH), blk2),
            pl.BlockSpec((ss, ML_HEADS, ML_HD, ML_HD), seq4),
            pl.BlockSpec((ss, ML_HEADS, ML_HD), seq3),
            pl.BlockSpec((ss, CONV_K - 1, CONV_CH), seq3),
        ],
        out_shape=[
            jax.ShapeDtypeStruct((n_seq * DEC_SEQ_LEN, MIX_WIDTH), _F32),
            jax.ShapeDtypeStruct(c0.shape, _F32),
            jax.ShapeDtypeStruct(n0.shape, _F32),
            jax.ShapeDtypeStruct(conv0.shape, _F32),
        ],
        scratch_shapes=[
            pltpu.VMEM((nt, ML_WIDTH), _F32),
            pltpu.VMEM((nt, LANES), _F32),
            pltpu.VMEM((ss, 2 * SUBLANES, CONV_CH), _F32),
        ],
        compiler_params=pltpu.CompilerParams(
            dimension_semantics=("arbitrary", "arbitrary"), vmem_limit_bytes=VMEM_LIMIT),
        name="sample_state",
    )(proj, cols, rows, c0, n0, conv0, mk, mv, g_head, conv_w, conv_b)


def _outproj_kernel(ymix_ref, x_ref, wout_ref, gfinal_ref, y_ref):
    y_ref[...] = _out_tail(ymix_ref[...].astype(_BF16), wout_ref[...], x_ref[...], gfinal_ref[...])


def _outproj_call(ymix, x, w_out, g_final):
    n_tok = x.shape[0]
    rb = 256
    const = lambda i: (0, 0)
    return pl.pallas_call(
        _outproj_kernel,
        grid=(n_tok // rb,),
        in_specs=[
            pl.BlockSpec((rb, MIX_WIDTH), lambda i: (i, 0)),
            pl.BlockSpec((rb, D_MODEL), lambda i: (i, 0)),
            pl.BlockSpec((MIX_WIDTH, D_MODEL), const),
            pl.BlockSpec((1, D_MODEL), const),
        ],
        out_specs=pl.BlockSpec((rb, D_MODEL), lambda i: (i, 0)),
        out_shape=jax.ShapeDtypeStruct((n_tok, D_MODEL), _F32),
        compiler_params=pltpu.CompilerParams(dimension_semantics=("arbitrary",)),
        name="sample_outproj",
    )(ymix, x, w_out, g_final)


def kernel(x_prompt, x_sample, state_mlstm_C, state_mlstm_n, state_mlstm_m, state_conv, cache_mem_k, cache_mem_v, mem_prompt, g_norm, w_in, b_if, g_head, conv_w, conv_b, g_mem, w_mem_kv, w_out, g_final):
    assert w_in.shape[0] == 1, "single-layer trunk"
    bp, tp, _ = x_prompt.shape
    bs, ts, _ = x_sample.shape
    assert ts == DEC_SEQ_LEN and tp % PROMPT_BLOCK == 0 and bs % SAMPLE_BLOCK_SEQS == 0

    w = w_in[0]
    w_if = jnp.pad(w[:, COL_IF:COL_IF + N_IF], ((0, 0), (0, LANES - N_IF)))
    w_qkvo = (w[:, :COL_ZML].reshape(D_MODEL, 4, ML_HEADS, ML_HD).transpose(0, 2, 1, 3)
              .reshape(D_MODEL, COL_ZML))
    w_all = jnp.concatenate([w_qkvo, w[:, COL_ZML:COL_IF], w_if, w[:, COL_IF + N_IF:]], axis=1).astype(_BF16)
    w_out_b = w_out[0].astype(_BF16)
    w_kv_b = w_mem_kv[0].astype(_BF16)
    b_if_p = jnp.pad(b_if[0][None, :], ((0, 0), (0, LANES - N_IF)))
    row = lambda v: v.reshape(1, -1)

    mk_p, mv_p, mk_rows, mv_rows = _memkv_call(mem_prompt, row(g_mem[0]), w_kv_b)
    y_p, c_p, n_p, m_p, conv_p = _prompt_call(
        x_prompt, mk_p, mv_p, row(g_norm[0]), w_all, b_if_p, row(g_head[0]), conv_w[0], row(conv_b[0]),
        w_out_b, row(g_final))

    n_tok = bs * ts
    xs = x_sample.reshape(n_tok, D_MODEL)
    m0_rows = jnp.pad(jnp.repeat(state_mlstm_m[0].T, ts, axis=1), ((0, SUBLANES - ML_HEADS), (0, 0)))
    proj, cols, rows = _inproj_call(xs, row(g_norm[0]), w_all, b_if_p, m0_rows)
    ymix, c_s, n_s, conv_s = _sample_call(
        proj, cols, rows, state_mlstm_C[0], state_mlstm_n[0], state_conv[0],
        cache_mem_k[0].reshape(bs, XA_KV_ROWS, XA_HD), cache_mem_v[0].reshape(bs, XA_KV_ROWS, XA_HD),
        row(g_head[0]), conv_w[0], row(conv_b[0]))
    y_s = _outproj_call(ymix, xs, w_out_b, row(g_final))
    m_s = rows[SUBLANES:SUBLANES + ML_HEADS, ts - 1::ts].T

    kv_shape = (1, bp, N_MEM, XA_HEADS, XA_HD)
    return (y_p, y_s.reshape(bs, ts, D_MODEL),
            c_p[None], n_p[None], m_p[:, :ML_HEADS, 0][None], conv_p[None],
            mk_rows.reshape(kv_shape), mv_rows.reshape(kv_shape),
            c_s[None], n_s[None], m_s[None], conv_s[None])
```

```python
import functools

import jax
import jax.numpy as jnp
from jax import lax
from jax.experimental import pallas as pl
from jax.experimental.pallas import tpu as pltpu

D_MODEL = 1024
ML_HEADS = 4
ML_HD = 256
ML_WIDTH = ML_HEADS * ML_HD
CONV_CH = 512
CONV_K = 3
XA_HEADS = 4
XA_HD = 128
XA_WIDTH = XA_HEADS * XA_HD
N_MEM = 256
XA_KV_ROWS = N_MEM * XA_HEADS
DEC_SEQ_LEN = 8
MIX_WIDTH = ML_WIDTH + CONV_CH + XA_WIDTH
EPS = 1e-6

SUBLANES = 8
LANES = 128

ML_HEAD_COLS = ML_HD
OFF_Q = 0
OFF_K = ML_WIDTH
OFF_V = 2 * ML_WIDTH
OFF_O = 3 * ML_WIDTH
COL_ZML = 4 * ML_WIDTH
COL_IF = 5 * ML_WIDTH
COL_BG = COL_IF + LANES
COL_CG = COL_BG + CONV_CH
COL_XC = COL_CG + CONV_CH
COL_ZCV = COL_XC + CONV_CH
COL_QX = COL_ZCV + CONV_CH
COL_ZX = COL_QX + XA_WIDTH
PROJ_COLS = COL_ZX + XA_WIDTH
N_IF = 2 * ML_HEADS

GC_R = 0 * SUBLANES
GC_WI = 1 * SUBLANES
GC_EN = 2 * SUBLANES
GC_WS = 3 * SUBLANES
GC_WC = 4 * SUBLANES
GC_USED = 5 * SUBLANES

PROMPT_BLOCK = 256
SAMPLE_BLOCK_SEQS = 16
SAMPLE_STEP_SEQS = 4
INPROJ_TILE = 640
VMEM_LIMIT = 60 * 1024 * 1024

_F32 = jnp.float32
_BF16 = jnp.bfloat16


def _dot(a, b):
    return jnp.dot(a, b, preferred_element_type=_F32)


def _dot_nt(a, b):
    return lax.dot_general(a, b, (((1,), (1,)), ((), ())), preferred_element_type=_F32)


def _dot_tn(a, b):
    return lax.dot_general(a, b, (((0,), (0,)), ((), ())), preferred_element_type=_F32)


def _rms(x, g):
    r = lax.rsqrt(jnp.mean(x * x, axis=-1, keepdims=True) + EPS)
    return (x * r) * g


def _silu(x):
    return x * jax.nn.sigmoid(x)


def _log_sigmoid(x):
    return jnp.minimum(x, 0.0) - jnp.log1p(jnp.exp(-jnp.abs(x)))


def _gate_rows(slab, m_prev, seg):
    n = slab.shape[1]
    pos = lax.broadcasted_iota(jnp.int32, slab.shape, 1) & (seg - 1)

    def scan(x, op, fill, reverse=False):
        k = 1
        while k < seg:
            if reverse:
                shifted, ok = pltpu.roll(x, n - k, axis=1), pos < seg - k
            else:
                shifted, ok = pltpu.roll(x, k, axis=1), pos >= k
            x = op(x, jnp.where(ok, shifted, fill))
            k *= 2
        return x

    logf = _log_sigmoid(pltpu.roll(slab, ML_HEADS, axis=0))
    f_cum = scan(logf, jnp.add, 0.0)
    a = slab - f_cum
    r = jnp.maximum(scan(a, jnp.maximum, -jnp.inf), m_prev)
    r_last = scan(r, jnp.maximum, -jnp.inf, reverse=True)
    m_t = f_cum + r
    w_inter = jnp.exp(m_prev - r)
    e_negm = jnp.exp(-m_t)
    w_s = jnp.exp(a - r_last)
    w_c = jnp.exp(m_prev - r_last)
    return a, r, m_t, w_inter, e_negm, w_s, w_c


def _gate_cols(r, w_inter, e_negm, w_s, w_c):
    n = r.shape[1]
    pad = jnp.zeros((LANES - GC_USED, n), _F32)
    return jnp.concatenate([r, w_inter, e_negm, w_s, w_c, pad], axis=0).T


def _ml_head_out(hh, o_pre, z, g_head):
    hh = jax.nn.sigmoid(o_pre) * hh
    return _rms(hh, g_head) * _silu(z)


def _out_tail(ymix_bf16, w_out, x, g_final):
    y = _dot(ymix_bf16, w_out) + x
    return _rms(y, g_final)


def _wprep_kernel(cur_ref, prev_ref, out_ref):
    j = pl.program_id(0)
    gate_tile = COL_IF // INPROJ_TILE
    shift = LANES - N_IF
    lane = lax.broadcasted_iota(jnp.int32, cur_ref.shape, 1)

    @pl.when(j < gate_tile)
    def _():
        out_ref[...] = cur_ref[...].astype(_BF16)

    @pl.when(j == gate_tile)
    def _():
        cur = cur_ref[...]
        moved = pltpu.roll(cur, shift, axis=1)
        out_ref[...] = jnp.where(lane < N_IF, cur, jnp.where(lane < LANES, 0.0, moved)).astype(_BF16)

    @pl.when(j > gate_tile)
    def _():
        moved = jnp.where(lane < shift, pltpu.roll(prev_ref[...], shift, axis=1),
                          pltpu.roll(cur_ref[...], shift, axis=1))
        out_ref[...] = moved.astype(_BF16)


def _wprep_call(w):
    assert COL_IF % INPROJ_TILE == 0 and PROJ_COLS % INPROJ_TILE == 0
    gate_tile = COL_IF // INPROJ_TILE
    return pl.pallas_call(
        _wprep_kernel,
        grid=(PROJ_COLS // INPROJ_TILE,),
        in_specs=[
            pl.BlockSpec((D_MODEL, INPROJ_TILE), lambda j: (0, j)),
            pl.BlockSpec((D_MODEL, INPROJ_TILE), lambda j: (0, jnp.where(j > gate_tile, j - 1, 0))),
        ],
        out_specs=pl.BlockSpec((D_MODEL, INPROJ_TILE), lambda j: (0, j)),
        out_shape=jax.ShapeDtypeStruct((D_MODEL, PROJ_COLS), _BF16),
        compiler_params=pltpu.CompilerParams(dimension_semantics=("arbitrary",)),
        name="weight_repack",
    )(w, w)


def _memkv_kernel(mem_ref, g_ref, w_ref, mk_ref, mv_ref, mk_rows_ref, mv_rows_ref):
    xn = _rms(mem_ref[0], g_ref[...]).astype(_BF16)
    kv = _dot(xn, w_ref[...])
    mk_ref[0] = kv[:, :XA_WIDTH]
    mv_ref[0] = kv[:, XA_WIDTH:]
    for h in range(XA_HEADS):
        head_rows = pl.ds(h, N_MEM, stride=XA_HEADS)
        mk_rows_ref[0, head_rows, :] = kv[:, h * XA_HD:(h + 1) * XA_HD]
        mv_rows_ref[0, head_rows, :] = kv[:, XA_WIDTH + h * XA_HD:XA_WIDTH + (h + 1) * XA_HD]


def _memkv_call(mem, g_mem, w_kv):
    b = mem.shape[0]
    const = lambda i: (0, 0)
    per_batch = lambda i: (i, 0, 0)
    return pl.pallas_call(
        _memkv_kernel,
        grid=(b,),
        in_specs=[
            pl.BlockSpec((1, N_MEM, D_MODEL), per_batch),
            pl.BlockSpec((1, D_MODEL), const),
            pl.BlockSpec((D_MODEL, 2 * XA_WIDTH), const),
        ],
        out_specs=[
            pl.BlockSpec((1, N_MEM, XA_WIDTH), per_batch),
            pl.BlockSpec((1, N_MEM, XA_WIDTH), per_batch),
            pl.BlockSpec((1, XA_KV_ROWS, XA_HD), per_batch),
            pl.BlockSpec((1, XA_KV_ROWS, XA_HD), per_batch),
        ],
        out_shape=[jax.ShapeDtypeStruct((b, N_MEM, XA_WIDTH), _F32)] * 2
        + [jax.ShapeDtypeStruct((b, XA_KV_ROWS, XA_HD), _F32)] * 2,
        compiler_params=pltpu.CompilerParams(dimension_semantics=("arbitrary",)),
        name="memkv",
    )(mem, g_mem, w_kv)


N_FILLERS = 8


def _finish_block(proj, filler, x, mk_ref, mv_ref, bif_ref, ghead_ref, convw_ref, convb_ref, wout_ref, gfinal_ref,
                  y_ref, c_ref, n_ref, m_ref, conv_ref, ext_ref, acc_ref):
    tb = PROMPT_BLOCK

    def out_part(y_bf16, row0):
        return _dot(y_bf16, wout_ref[row0:row0 + y_bf16.shape[1], :])

    gates = proj(COL_IF, LANES)
    slab = (gates + bif_ref[...]).T[0:SUBLANES, :]
    m_prev8 = m_ref[0]
    m_prev = jnp.concatenate([m_prev8] * (tb // LANES), axis=1)
    a, r, m_t, w_inter, e_negm, w_s, w_c = _gate_rows(slab, m_prev, tb)
    cols = _gate_cols(r, w_inter, e_negm, w_s, w_c)
    m_ref[0] = jnp.broadcast_to(m_t[:, tb - 1:tb], (SUBLANES, LANES))

    row = lax.broadcasted_iota(jnp.int32, (tb, tb), 0)
    col = lax.broadcasted_iota(jnp.int32, (tb, tb), 1)
    causal = col <= row

    def head_first(h):
        q = proj(h * ML_HEAD_COLS + OFF_Q, ML_HD)
        k = proj(h * ML_HEAD_COLS + OFF_K, ML_HD) * (ML_HD ** -0.5)
        qb = q.astype(_BF16)
        c_old = c_ref[0, h]
        qk = _dot_nt(qb, k.astype(_BF16))
        qc = _dot(qb, c_old.astype(_BF16))
        return q, k, c_old, qk, qc

    def head_second(h, first):
        q, k, c_old, qk, qc = first
        vb = proj(h * ML_HEAD_COLS + OFF_V, ML_HD).astype(_BF16)
        r_col = cols[:, GC_R + h:GC_R + h + 1]
        wi_col = cols[:, GC_WI + h:GC_WI + h + 1]
        en_col = cols[:, GC_EN + h:GC_EN + h + 1]
        ws_col = cols[:, GC_WS + h:GC_WS + h + 1]
        wc = w_c[h:h + 1, 0:1]
        s_mat = qk * jnp.exp(jnp.where(causal, a[h:h + 1, :] - r_col, -jnp.inf))
        kw = k * ws_col
        sv = _dot(s_mat.astype(_BF16), vb)
        c_ref[0, h] = wc * c_old + _dot_tn(kw.astype(_BF16), vb)
        n_old = n_ref[0, h:h + 1, :]
        n_ref[0, h:h + 1, :] = wc * n_old + jnp.sum(kw, axis=0, keepdims=True)
        num = wi_col * qc + sv
        den = wi_col * jnp.sum(q * n_old, axis=1, keepdims=True) + jnp.sum(s_mat, axis=1, keepdims=True)
        hh = num * (1.0 / jnp.maximum(jnp.abs(den), en_col))
        return jax.nn.sigmoid(proj(h * ML_HEAD_COLS + OFF_O, ML_HD)) * hh

    def head_out(h, gated, z_all):
        hs = h * ML_HD
        y_ml = _rms(gated, ghead_ref[:, hs:hs + ML_HD]) * _silu(z_all[:, hs:hs + ML_HD])
        return out_part(y_ml.astype(_BF16), hs)

    def xa_scores(h):
        hs = h * XA_HD
        kh = mk_ref[0, :, hs:hs + XA_HD].astype(_BF16)
        return _dot_nt(xa_in[:, hs:hs + XA_HD].astype(_BF16), kh) * (XA_HD ** -0.5)

    def xa_out(h, s):
        hs = h * XA_HD
        vh = mv_ref[0, :, hs:hs + XA_HD].astype(_BF16)
        e = jnp.exp(s - jnp.max(s, axis=1, keepdims=True))
        p = e * (1.0 / jnp.sum(e, axis=1, keepdims=True))
        y_xa = _dot(p.astype(_BF16), vh) * _silu(xa_in[:, XA_WIDTH + hs:XA_WIDTH + hs + XA_HD])
        return y_xa.astype(_BF16)

    xa_in = proj(COL_QX, 2 * XA_WIDTH)
    conv_in = proj(COL_BG, 4 * CONV_CH)
    z_all = proj(COL_ZML, ML_WIDTH)

    s0, s1, s2, s3 = (xa_scores(h) for h in range(XA_HEADS))
    filler()
    first0 = head_first(0)
    first1 = head_first(1)
    y_xa0 = xa_out(0, s0)
    y_xa1 = xa_out(1, s1)
    filler()
    first2 = head_first(2)
    first3 = head_first(3)
    y_xa2 = xa_out(2, s2)
    y_xa3 = xa_out(3, s3)
    filler()

    b_g = conv_in[:, 0:CONV_CH]
    u = conv_in[:, CONV_CH:2 * CONV_CH] * conv_in[:, 2 * CONV_CH:3 * CONV_CH]
    ext_ref[SUBLANES:SUBLANES + tb, :] = u
    u1 = ext_ref[SUBLANES - 1:SUBLANES - 1 + tb, :]
    u2 = ext_ref[SUBLANES - 2:SUBLANES - 2 + tb, :]
    yc = convb_ref[...] + convw_ref[0:1, :] * u2 + convw_ref[1:2, :] * u1 + convw_ref[2:3, :] * u
    y_cv = b_g * yc * _silu(conv_in[:, 3 * CONV_CH:4 * CONV_CH])
    tail = u[tb - (CONV_K - 1):tb, :]
    ext_ref[SUBLANES - (CONV_K - 1):SUBLANES, :] = tail
    conv_ref[0] = tail
    gated0 = head_second(0, first0)
    filler()
    gated1 = head_second(1, first1)
    acc_ref[...] = out_part(jnp.concatenate([y_cv.astype(_BF16), y_xa0, y_xa1, y_xa2, y_xa3], axis=1), ML_WIDTH)
    filler()
    gated2 = head_second(2, first2)
    acc = head_out(0, gated0, z_all)
    filler()
    gated3 = head_second(3, first3)
    acc = acc + head_out(1, gated1, z_all)
    filler()
    acc = acc + head_out(2, gated2, z_all)
    acc = acc + head_out(3, gated3, z_all)
    filler()

    y_ref[0] = _rms(acc_ref[...] + acc + x, gfinal_ref[...])


def _prompt_kernel(xnext_ref, x_ref, mk_ref, mv_ref, gnorm_ref, w_ref, bif_ref, ghead_ref, convw_ref, convb_ref,
                   wout_ref, gfinal_ref,
                   y_ref, c_ref, n_ref, m_ref, conv_ref,
                   proj_even_ref, proj_odd_ref, ext_ref, acc_ref, *, blocks_per_seq):
    s = pl.program_id(0)

    @pl.when(s == 0)
    def _():
        proj_odd_ref[...] = jnp.zeros_like(proj_odd_ref)

    @pl.when((s == 0) | (lax.rem(s + blocks_per_seq - 1, blocks_per_seq) == 0))
    def _():
        c_ref[...] = jnp.zeros_like(c_ref)
        n_ref[...] = jnp.zeros_like(n_ref)
        m_ref[...] = jnp.zeros_like(m_ref)
        ext_ref[0:SUBLANES, :] = jnp.zeros((SUBLANES, CONV_CH), _F32)

    def step(proj_w_ref, proj_r_ref):
        xb_next = _rms(xnext_ref[0], gnorm_ref[...]).astype(_BF16)
        chunk = PROJ_COLS // LANES // N_FILLERS * LANES
        starts = iter(range(0, chunk * N_FILLERS, chunk))

        def project_chunk():
            c0 = next(starts)
            c1 = PROJ_COLS if c0 == chunk * (N_FILLERS - 1) else c0 + chunk
            proj_w_ref[:, c0:c1] = _dot(xb_next, w_ref[:, c0:c1])

        _finish_block(lambda col, width: proj_r_ref[:, col:col + width], project_chunk, x_ref[0],
                      mk_ref, mv_ref, bif_ref, ghead_ref, convw_ref, convb_ref, wout_ref, gfinal_ref,
                      y_ref, c_ref, n_ref, m_ref, conv_ref, ext_ref, acc_ref)

    @pl.when(lax.rem(s, 2) == 0)
    def _():
        step(proj_even_ref, proj_odd_ref)

    @pl.when(lax.rem(s, 2) == 1)
    def _():
        step(proj_odd_ref, proj_even_ref)


def _prompt_call(x, mk, mv, g_norm, w_all, b_if, g_head, conv_w, conv_b, w_out, g_final):
    b, t, _ = x.shape
    tb = PROMPT_BLOCK
    nt = t // tb
    n_blocks = b * nt
    const2 = lambda s: (0, 0)
    resident = functools.partial(pl.BlockSpec, index_map=const2, pipeline_mode=pl.Buffered(1))
    nxt = lambda s: jnp.minimum(s, n_blocks - 1)
    cur = lambda s: jnp.maximum(s - 1, 0)
    cur_batch3 = lambda s: (cur(s) // nt, 0, 0)
    return pl.pallas_call(
        functools.partial(_prompt_kernel, blocks_per_seq=nt),
        grid=(n_blocks + 1,),
        in_specs=[
            pl.BlockSpec((1, tb, D_MODEL), lambda s: (nxt(s) // nt, nxt(s) % nt, 0)),
            pl.BlockSpec((1, tb, D_MODEL), lambda s: (cur(s) // nt, cur(s) % nt, 0)),
            pl.BlockSpec((1, N_MEM, XA_WIDTH), cur_batch3),
            pl.BlockSpec((1, N_MEM, XA_WIDTH), cur_batch3),
            pl.BlockSpec((1, D_MODEL), const2),
            resident((D_MODEL, PROJ_COLS)),
            pl.BlockSpec((1, LANES), const2),
            pl.BlockSpec((1, ML_WIDTH), const2),
            pl.BlockSpec((CONV_K, CONV_CH), const2),
            pl.BlockSpec((1, CONV_CH), const2),
            resident((MIX_WIDTH, D_MODEL)),
            pl.BlockSpec((1, D_MODEL), const2),
        ],
        out_specs=[
            pl.BlockSpec((1, tb, D_MODEL), lambda s: (cur(s) // nt, cur(s) % nt, 0)),
            pl.BlockSpec((1, ML_HEADS, ML_HD, ML_HD), lambda s: (cur(s) // nt, 0, 0, 0)),
            pl.BlockSpec((1, ML_HEADS, ML_HD), cur_batch3),
            pl.BlockSpec((1, SUBLANES, LANES), cur_batch3),
            pl.BlockSpec((1, CONV_K - 1, CONV_CH), cur_batch3),
        ],
        out_shape=[
            jax.ShapeDtypeStruct((b, t, D_MODEL), _F32),
            jax.ShapeDtypeStruct((b, ML_HEADS, ML_HD, ML_HD), _F32),
            jax.ShapeDtypeStruct((b, ML_HEADS, ML_HD), _F32),
            jax.ShapeDtypeStruct((b, SUBLANES, LANES), _F32),
            jax.ShapeDtypeStruct((b, CONV_K - 1, CONV_CH), _F32),
        ],
        scratch_shapes=[
            pltpu.VMEM((tb, PROJ_COLS), _F32),
            pltpu.VMEM((tb, PROJ_COLS), _F32),
            pltpu.VMEM((SUBLANES + tb, CONV_CH), _F32),
            pltpu.VMEM((tb, D_MODEL), _F32),
        ],
        compiler_params=pltpu.CompilerParams(
            dimension_semantics=("arbitrary",), vmem_limit_bytes=VMEM_LIMIT),
        name="prompt_layer",
    )(x, x, mk, mv, g_norm, w_all, b_if, g_head, conv_w, conv_b, w_out, g_final)


def _inproj_kernel(x_ref, gnorm_ref, w_ref, bif_ref, m0_ref, proj_ref, cols_ref, rows_ref, xb_ref):
    j = pl.program_id(0)

    @pl.when(j == 0)
    def _():
        xb_ref[...] = _rms(x_ref[...], gnorm_ref[...]).astype(_BF16)

    p = _dot(xb_ref[...], w_ref[...])
    proj_ref[...] = p

    @pl.when(j == COL_IF // INPROJ_TILE)
    def _():
        off = COL_IF % INPROJ_TILE
        slab = (p[:, off:off + LANES] + bif_ref[...]).T[0:SUBLANES, :]
        a, r, m_t, w_inter, e_negm, w_s, w_c = _gate_rows(slab, m0_ref[...], DEC_SEQ_LEN)
        cols_ref[...] = _gate_cols(r, w_inter, e_negm, w_s, w_c)
        rows_ref[0:SUBLANES, :] = a
        rows_ref[SUBLANES:2 * SUBLANES, :] = m_t


def _inproj_call(x, g_norm, w_all, b_if, m0_rows):
    n_tok = x.shape[0]
    const = lambda j: (0, 0)
    return pl.pallas_call(
        _inproj_kernel,
        grid=(PROJ_COLS // INPROJ_TILE,),
        in_specs=[
            pl.BlockSpec((n_tok, D_MODEL), const),
            pl.BlockSpec((1, D_MODEL), const),
            pl.BlockSpec((D_MODEL, INPROJ_TILE), lambda j: (0, j)),
            pl.BlockSpec((1, LANES), const),
            pl.BlockSpec((SUBLANES, n_tok), const),
        ],
        out_specs=[
            pl.BlockSpec((n_tok, INPROJ_TILE), lambda j: (0, j)),
            pl.BlockSpec((n_tok, LANES), const),
            pl.BlockSpec((2 * SUBLANES, n_tok), const),
        ],
        out_shape=[
            jax.ShapeDtypeStruct((n_tok, PROJ_COLS), _F32),
            jax.ShapeDtypeStruct((n_tok, LANES), _F32),
            jax.ShapeDtypeStruct((2 * SUBLANES, n_tok), _F32),
        ],
        scratch_shapes=[pltpu.VMEM((n_tok, D_MODEL), _BF16)],
        compiler_params=pltpu.CompilerParams(
            dimension_semantics=("arbitrary",), vmem_limit_bytes=VMEM_LIMIT),
        name="sample_inproj",
    )(x, g_norm, w_all, b_if, m0_rows)


def _sample_kernel(proj_ref, cols_ref, rows_ref, c0_ref, n0_ref, conv0_ref, mk_ref, mv_ref,
                   ghead_ref, convw_ref, convb_ref,
                   ymix_ref, c_ref, n_ref, conv_ref,
                   sv_ref, rs_ref, ext_ref):
    nt = SAMPLE_BLOCK_SEQS * DEC_SEQ_LEN
    j = pl.program_id(1)

    @pl.when(j == 0)
    def _():
        row = lax.broadcasted_iota(jnp.int32, (nt, nt), 0)
        col = lax.broadcasted_iota(jnp.int32, (nt, nt), 1)
        mask = (col <= row) & ((row // DEC_SEQ_LEN) == (col // DEC_SEQ_LEN))
        a = rows_ref[0:SUBLANES, :]
        for h in range(ML_HEADS):
            hs = h * ML_HD
            hc = h * ML_HEAD_COLS
            qb = proj_ref[:, hc + OFF_Q:hc + OFF_Q + ML_HD].astype(_BF16)
            kb = (proj_ref[:, hc + OFF_K:hc + OFF_K + ML_HD] * (ML_HD ** -0.5)).astype(_BF16)
            vb = proj_ref[:, hc + OFF_V:hc + OFF_V + ML_HD].astype(_BF16)
            r_col = cols_ref[:, GC_R + h:GC_R + h + 1]
            decay = jnp.exp(jnp.where(mask, a[h:h + 1, :] - r_col, -jnp.inf))
            s_mat = _dot_nt(qb, kb) * decay
            sv_ref[:, hs:hs + ML_HD] = _dot(s_mat.astype(_BF16), vb)
            rs_ref[:, h:h + 1] = jnp.sum(s_mat, axis=1, keepdims=True)

    xa_rows = XA_HEADS * DEC_SEQ_LEN
    own_head = (lax.broadcasted_iota(jnp.int32, (xa_rows, XA_KV_ROWS), 0) // DEC_SEQ_LEN
                == lax.broadcasted_iota(jnp.int32, (xa_rows, XA_KV_ROWS), 1) % XA_HEADS)

    for sl in range(SAMPLE_STEP_SEQS):
        ro = pl.multiple_of((j * SAMPLE_STEP_SEQS + sl) * DEC_SEQ_LEN, DEC_SEQ_LEN)
        rows = pl.ds(ro, DEC_SEQ_LEN)
        gc = cols_ref[rows, :]
        for h in range(ML_HEADS):
            hs = h * ML_HD
            hc = h * ML_HEAD_COLS
            q = proj_ref[rows, hc + OFF_Q:hc + OFF_Q + ML_HD]
            k = proj_ref[rows, hc + OFF_K:hc + OFF_K + ML_HD] * (ML_HD ** -0.5)
            vb = proj_ref[rows, hc + OFF_V:hc + OFF_V + ML_HD].astype(_BF16)
            wi_col = gc[:, GC_WI + h:GC_WI + h + 1]
            en_col = gc[:, GC_EN + h:GC_EN + h + 1]
            ws_col = gc[:, GC_WS + h:GC_WS + h + 1]
            wc = gc[0:1, GC_WC + h:GC_WC + h + 1]
            c_old = c0_ref[sl, h]
            n_old = n0_ref[sl, h:h + 1, :]
            num = wi_col * _dot(q.astype(_BF16), c_old.astype(_BF16)) + sv_ref[rows, hs:hs + ML_HD]
            den = wi_col * jnp.sum(q * n_old, axis=1, keepdims=True) + rs_ref[rows, h:h + 1]
            hh = num * (1.0 / jnp.maximum(jnp.abs(den), en_col))
            kw = k * ws_col
            c_ref[sl, h] = wc * c_old + _dot_tn(kw.astype(_BF16), vb)
            n_ref[sl, h:h + 1, :] = wc * n_old + jnp.sum(kw, axis=0, keepdims=True)
            y_ml = _ml_head_out(hh, proj_ref[rows, hc + OFF_O:hc + OFF_O + ML_HD],
                                proj_ref[rows, COL_ZML + hs:COL_ZML + hs + ML_HD], ghead_ref[:, hs:hs + ML_HD])
            ymix_ref[rows, hs:hs + ML_HD] = y_ml

        u = proj_ref[rows, COL_CG:COL_CG + CONV_CH] * proj_ref[rows, COL_XC:COL_XC + CONV_CH]
        ext_ref[sl, SUBLANES - (CONV_K - 1):SUBLANES, :] = conv0_ref[sl]
        ext_ref[sl, SUBLANES:2 * SUBLANES, :] = u
        u1 = ext_ref[sl, SUBLANES - 1:2 * SUBLANES - 1, :]
        u2 = ext_ref[sl, SUBLANES - 2:2 * SUBLANES - 2, :]
        yc = convb_ref[...] + convw_ref[0:1, :] * u2 + convw_ref[1:2, :] * u1 + convw_ref[2:3, :] * u
        y_cv = (proj_ref[rows, COL_BG:COL_BG + CONV_CH] * yc
                * _silu(proj_ref[rows, COL_ZCV:COL_ZCV + CONV_CH]))
        ymix_ref[rows, ML_WIDTH:ML_WIDTH + CONV_CH] = y_cv
        conv_ref[sl] = u[DEC_SEQ_LEN - (CONV_K - 1):DEC_SEQ_LEN, :]

        q4 = jnp.concatenate(
            [proj_ref[rows, COL_QX + h * XA_HD:COL_QX + (h + 1) * XA_HD] for h in range(XA_HEADS)], axis=0)
        s = _dot_nt(q4.astype(_BF16), mk_ref[sl].astype(_BF16)) * (XA_HD ** -0.5)
        s = jnp.where(own_head, s, -jnp.inf)
        e = jnp.exp(s - jnp.max(s, axis=1, keepdims=True))
        p = e * (1.0 / jnp.sum(e, axis=1, keepdims=True))
        y4 = _dot(p.astype(_BF16), mv_ref[sl].astype(_BF16))
        for h in range(XA_HEADS):
            hs = h * XA_HD
            y_xa = y4[h * DEC_SEQ_LEN:(h + 1) * DEC_SEQ_LEN, :] * _silu(proj_ref[rows, COL_ZX + hs:COL_ZX + hs + XA_HD])
            c0 = ML_WIDTH + CONV_CH + hs
            ymix_ref[rows, c0:c0 + XA_HD] = y_xa


def _sample_call(proj, cols, rows, c0, n0, conv0, mk, mv, g_head, conv_w, conv_b):
    n_seq = c0.shape[0]
    nt = SAMPLE_BLOCK_SEQS * DEC_SEQ_LEN
    steps = SAMPLE_BLOCK_SEQS // SAMPLE_STEP_SEQS
    ss = SAMPLE_STEP_SEQS
    const2 = lambda i, j: (0, 0)
    blk2 = lambda i, j: (i, 0)
    seq3 = lambda i, j: (i * steps + j, 0, 0)
    seq4 = lambda i, j: (i * steps + j, 0, 0, 0)
    return pl.pallas_call(
        _sample_kernel,
        grid=(n_seq // SAMPLE_BLOCK_SEQS, steps),
        in_specs=[
            pl.BlockSpec((nt, PROJ_COLS), blk2),
            pl.BlockSpec((nt, LANES), blk2),
            pl.BlockSpec((2 * SUBLANES, nt), lambda i, j: (0, i)),
            pl.BlockSpec((ss, ML_HEADS, ML_HD, ML_HD), seq4),
            pl.BlockSpec((ss, ML_HEADS, ML_HD), seq3),
            pl.BlockSpec((ss, CONV_K - 1, CONV_CH), seq3),
            pl.BlockSpec((ss, XA_KV_ROWS, XA_HD), seq3),
            pl.BlockSpec((ss, XA_KV_ROWS, XA_HD), seq3),
            pl.BlockSpec((1, ML_WIDTH), const2),
            pl.BlockSpec((CONV_K, CONV_CH), const2),
            pl.BlockSpec((1, CONV_CH), const2),
        ],
        out_specs=[
            pl.BlockSpec((nt, MIX_WIDTH), blk2),
            pl.BlockSpec((ss, ML_HEADS, ML_HD, ML_HD), seq4),
            pl.BlockSpec((ss, ML_HEADS, ML_HD), seq3),
            pl.BlockSpec((ss, CONV_K - 1, CONV_CH), seq3),
        ],
        out_shape=[
            jax.ShapeDtypeStruct((n_seq * DEC_SEQ_LEN, MIX_WIDTH), _F32),
            jax.ShapeDtypeStruct(c0.shape, _F32),
            jax.ShapeDtypeStruct(n0.shape, _F32),
            jax.ShapeDtypeStruct(conv0.shape, _F32),
        ],
        scratch_shapes=[
            pltpu.VMEM((nt, ML_WIDTH), _F32),
            pltpu.VMEM((nt, LANES), _F32),
            pltpu.VMEM((ss, 2 * SUBLANES, CONV_CH), _F32),
        ],
        compiler_params=pltpu.CompilerParams(
            dimension_semantics=("arbitrary", "arbitrary"), vmem_limit_bytes=VMEM_LIMIT),
        name="sample_state",
    )(proj, cols, rows, c0, n0, conv0, mk, mv, g_head, conv_w, conv_b)


def _outproj_kernel(ymix_ref, x_ref, wout_ref, gfinal_ref, y_ref):
    y_ref[...] = _out_tail(ymix_ref[...].astype(_BF16), wout_ref[...], x_ref[...], gfinal_ref[...])


def _outproj_call(ymix, x, w_out, g_final):
    n_tok = x.shape[0]
    rb = 256
    const = lambda i: (0, 0)
    return pl.pallas_call(
        _outproj_kernel,
        grid=(n_tok // rb,),
        in_specs=[
            pl.BlockSpec((rb, MIX_WIDTH), lambda i: (i, 0)),
            pl.BlockSpec((rb, D_MODEL), lambda i: (i, 0)),
            pl.BlockSpec((MIX_WIDTH, D_MODEL), const),
            pl.BlockSpec((1, D_MODEL), const),
        ],
        out_specs=pl.BlockSpec((rb, D_MODEL), lambda i: (i, 0)),
        out_shape=jax.ShapeDtypeStruct((n_tok, D_MODEL), _F32),
        compiler_params=pltpu.CompilerParams(dimension_semantics=("arbitrary",)),
        name="sample_outproj",
    )(ymix, x, w_out, g_final)


def kernel(x_prompt, x_sample, state_mlstm_C, state_mlstm_n, state_mlstm_m, state_conv, cache_mem_k, cache_mem_v, mem_prompt, g_norm, w_in, b_if, g_head, conv_w, conv_b, g_mem, w_mem_kv, w_out, g_final):
    assert w_in.shape[0] == 1, "single-layer trunk"
    bp, tp, _ = x_prompt.shape
    bs, ts, _ = x_sample.shape
    assert ts == DEC_SEQ_LEN and tp % PROMPT_BLOCK == 0 and bs % SAMPLE_BLOCK_SEQS == 0

    w_all = _wprep_call(w_in[0])
    w_out_b = w_out[0].astype(_BF16)
    w_kv_b = w_mem_kv[0].astype(_BF16)
    b_if_p = jnp.pad(b_if[0][None, :], ((0, 0), (0, LANES - N_IF)))
    row = lambda v: v.reshape(1, -1)

    mk_p, mv_p, mk_rows, mv_rows = _memkv_call(mem_prompt, row(g_mem[0]), w_kv_b)
    y_p, c_p, n_p, m_p, conv_p = _prompt_call(
        x_prompt, mk_p, mv_p, row(g_norm[0]), w_all, b_if_p, row(g_head[0]), conv_w[0], row(conv_b[0]),
        w_out_b, row(g_final))

    n_tok = bs * ts
    xs = x_sample.reshape(n_tok, D_MODEL)
    m0_rows = jnp.pad(jnp.repeat(state_mlstm_m[0].T, ts, axis=1), ((0, SUBLANES - ML_HEADS), (0, 0)))
    proj, cols, rows = _inproj_call(xs, row(g_norm[0]), w_all, b_if_p, m0_rows)
    ymix, c_s, n_s, conv_s = _sample_call(
        proj, cols, rows, state_mlstm_C[0], state_mlstm_n[0], state_conv[0],
        cache_mem_k[0].reshape(bs, XA_KV_ROWS, XA_HD), cache_mem_v[0].reshape(bs, XA_KV_ROWS, XA_HD),
        row(g_head[0]), conv_w[0], row(conv_b[0]))
    y_s = _outproj_call(ymix, xs, w_out_b, row(g_final))
    m_s = rows[SUBLANES:SUBLANES + ML_HEADS, ts - 1::ts].T

    kv_shape = (1, bp, N_MEM, XA_HEADS, XA_HD)
    return (y_p, y_s.reshape(bs, ts, D_MODEL),
            c_p[None], n_p[None], m_p[:, :ML_HEADS, 0][None], conv_p[None],
            mk_rows.reshape(kv_shape), mv_rows.reshape(kv_shape),
            c_s[None], n_s[None], m_s[None], conv_s[None])
```

```python
import functools

import jax
import jax.numpy as jnp
from jax import lax
from jax.experimental import pallas as pl
from jax.experimental.pallas import tpu as pltpu

D_MODEL = 1024
ML_HEADS = 4
ML_HD = 256
ML_WIDTH = ML_HEADS * ML_HD
CONV_CH = 512
CONV_K = 3
XA_HEADS = 4
XA_HD = 128
XA_WIDTH = XA_HEADS * XA_HD
N_MEM = 256
XA_KV_ROWS = N_MEM * XA_HEADS
DEC_SEQ_LEN = 8
MIX_WIDTH = ML_WIDTH + CONV_CH + XA_WIDTH
EPS = 1e-6

SUBLANES = 8
LANES = 128

ML_HEAD_COLS = ML_HD
OFF_Q = 0
OFF_K = ML_WIDTH
OFF_V = 2 * ML_WIDTH
OFF_O = 3 * ML_WIDTH
COL_ZML = 4 * ML_WIDTH
COL_IF = 5 * ML_WIDTH
COL_BG = COL_IF + LANES
COL_CG = COL_BG + CONV_CH
COL_XC = COL_CG + CONV_CH
COL_ZCV = COL_XC + CONV_CH
COL_QX = COL_ZCV + CONV_CH
COL_ZX = COL_QX + XA_WIDTH
PROJ_COLS = COL_ZX + XA_WIDTH
N_IF = 2 * ML_HEADS

GC_R = 0 * SUBLANES
GC_WI = 1 * SUBLANES
GC_EN = 2 * SUBLANES
GC_WS = 3 * SUBLANES
GC_WC = 4 * SUBLANES
GC_USED = 5 * SUBLANES

PROMPT_BLOCK = 256
SAMPLE_BLOCK_SEQS = 16
SAMPLE_STEP_SEQS = 4
INPROJ_TILE = 640
VMEM_LIMIT = 60 * 1024 * 1024

_F32 = jnp.float32
_BF16 = jnp.bfloat16


def _dot(a, b):
    return jnp.dot(a, b, preferred_element_type=_F32)


def _dot_nt(a, b):
    return lax.dot_general(a, b, (((1,), (1,)), ((), ())), preferred_element_type=_F32)


def _dot_tn(a, b):
    return lax.dot_general(a, b, (((0,), (0,)), ((), ())), preferred_element_type=_F32)


def _rms(x, g):
    r = lax.rsqrt(jnp.mean(x * x, axis=-1, keepdims=True) + EPS)
    return (x * r) * g


def _silu(x):
    return x * jax.nn.sigmoid(x)


def _log_sigmoid(x):
    return jnp.minimum(x, 0.0) - jnp.log1p(jnp.exp(-jnp.abs(x)))


def _gate_rows(slab, m_prev, seg):
    n = slab.shape[1]
    pos = lax.broadcasted_iota(jnp.int32, slab.shape, 1) & (seg - 1)

    def scan(x, op, fill, reverse=False):
        k = 1
        while k < seg:
            if reverse:
                shifted, ok = pltpu.roll(x, n - k, axis=1), pos < seg - k
            else:
                shifted, ok = pltpu.roll(x, k, axis=1), pos >= k
            x = op(x, jnp.where(ok, shifted, fill))
            k *= 2
        return x

    logf = _log_sigmoid(pltpu.roll(slab, ML_HEADS, axis=0))
    f_cum = scan(logf, jnp.add, 0.0)
    a = slab - f_cum
    r = jnp.maximum(scan(a, jnp.maximum, -jnp.inf), m_prev)
    r_last = scan(r, jnp.maximum, -jnp.inf, reverse=True)
    m_t = f_cum + r
    w_inter = jnp.exp(m_prev - r)
    e_negm = jnp.exp(-m_t)
    w_s = jnp.exp(a - r_last)
    w_c = jnp.exp(m_prev - r_last)
    return a, r, m_t, w_inter, e_negm, w_s, w_c


def _gate_cols(r, w_inter, e_negm, w_s, w_c):
    n = r.shape[1]
    pad = jnp.zeros((LANES - GC_USED, n), _F32)
    return jnp.concatenate([r, w_inter, e_negm, w_s, w_c, pad], axis=0).T


def _ml_head_out(hh, o_pre, z, g_head):
    hh = jax.nn.sigmoid(o_pre) * hh
    return _rms(hh, g_head) * _silu(z)


def _out_tail(ymix_bf16, w_out, x, g_final):
    y = _dot(ymix_bf16, w_out) + x
    return _rms(y, g_final)


WPREP_PAD = LANES - N_IF


def _wprep_kernel(wt_ref, out_ref):
    j = pl.program_id(0)
    gate_tile = COL_IF // INPROJ_TILE

    @pl.when(j != gate_tile)
    def _():
        out_ref[...] = wt_ref[...].T.astype(_BF16)

    @pl.when(j == gate_tile)
    def _():
        rows = wt_ref[...]
        padded = jnp.concatenate(
            [rows[0:N_IF], jnp.zeros((WPREP_PAD, D_MODEL), _F32), rows[N_IF:INPROJ_TILE - WPREP_PAD]], axis=0)
        out_ref[...] = padded.T.astype(_BF16)


def _wprep_call(w_t):
    assert COL_IF % INPROJ_TILE == 0 and PROJ_COLS % INPROJ_TILE == 0 and N_IF % SUBLANES == 0
    gate_tile = COL_IF // INPROJ_TILE
    first_row = lambda j: (j * (INPROJ_TILE // SUBLANES)
                           - (j > gate_tile).astype(jnp.int32) * (WPREP_PAD // SUBLANES)) * SUBLANES
    return pl.pallas_call(
        _wprep_kernel,
        grid=(PROJ_COLS // INPROJ_TILE,),
        in_specs=[pl.BlockSpec((pl.Element(INPROJ_TILE), pl.Element(D_MODEL)), lambda j: (first_row(j), 0))],
        out_specs=pl.BlockSpec((D_MODEL, INPROJ_TILE), lambda j: (0, j)),
        out_shape=jax.ShapeDtypeStruct((D_MODEL, PROJ_COLS), _BF16),
        compiler_params=pltpu.CompilerParams(dimension_semantics=("arbitrary",)),
        name="weight_repack",
    )(w_t)


def _memkv_kernel(mem_ref, g_ref, w_ref, mk_ref, mv_ref, mk_rows_ref, mv_rows_ref):
    xn = _rms(mem_ref[0], g_ref[...]).astype(_BF16)
    kv = _dot(xn, w_ref[...])
    mk_ref[0] = kv[:, :XA_WIDTH]
    mv_ref[0] = kv[:, XA_WIDTH:]
    for h in range(XA_HEADS):
        head_rows = pl.ds(h, N_MEM, stride=XA_HEADS)
        mk_rows_ref[0, head_rows, :] = kv[:, h * XA_HD:(h + 1) * XA_HD]
        mv_rows_ref[0, head_rows, :] = kv[:, XA_WIDTH + h * XA_HD:XA_WIDTH + (h + 1) * XA_HD]


def _memkv_call(mem, g_mem, w_kv):
    b = mem.shape[0]
    const = lambda i: (0, 0)
    per_batch = lambda i: (i, 0, 0)
    return pl.pallas_call(
        _memkv_kernel,
        grid=(b,),
        in_specs=[
            pl.BlockSpec((1, N_MEM, D_MODEL), per_batch),
            pl.BlockSpec((1, D_MODEL), const),
            pl.BlockSpec((D_MODEL, 2 * XA_WIDTH), const),
        ],
        out_specs=[
            pl.BlockSpec((1, N_MEM, XA_WIDTH), per_batch),
            pl.BlockSpec((1, N_MEM, XA_WIDTH), per_batch),
            pl.BlockSpec((1, XA_KV_ROWS, XA_HD), per_batch),
            pl.BlockSpec((1, XA_KV_ROWS, XA_HD), per_batch),
        ],
        out_shape=[jax.ShapeDtypeStruct((b, N_MEM, XA_WIDTH), _F32)] * 2
        + [jax.ShapeDtypeStruct((b, XA_KV_ROWS, XA_HD), _F32)] * 2,
        compiler_params=pltpu.CompilerParams(dimension_semantics=("arbitrary",)),
        name="memkv",
    )(mem, g_mem, w_kv)


N_FILLERS = 8


def _finish_block(proj, filler, x, mk_ref, mv_ref, bif_ref, ghead_ref, convw_ref, convb_ref, wout_ref, gfinal_ref,
                  y_ref, c_ref, n_ref, m_ref, conv_ref, ext_ref, acc_ref):
    tb = PROMPT_BLOCK

    def out_part(y_bf16, row0):
        return _dot(y_bf16, wout_ref[row0:row0 + y_bf16.shape[1], :])

    gates = proj(COL_IF, LANES)
    slab = (gates + bif_ref[...]).T[0:SUBLANES, :]
    m_prev8 = m_ref[0]
    m_prev = jnp.concatenate([m_prev8] * (tb // LANES), axis=1)
    a, r, m_t, w_inter, e_negm, w_s, w_c = _gate_rows(slab, m_prev, tb)
    cols = _gate_cols(r, w_inter, e_negm, w_s, w_c)
    m_ref[0] = jnp.broadcast_to(m_t[:, tb - 1:tb], (SUBLANES, LANES))

    row = lax.broadcasted_iota(jnp.int32, (tb, tb), 0)
    col = lax.broadcasted_iota(jnp.int32, (tb, tb), 1)
    causal = col <= row

    def head_first(h):
        q = proj(h * ML_HEAD_COLS + OFF_Q, ML_HD)
        k = proj(h * ML_HEAD_COLS + OFF_K, ML_HD) * (ML_HD ** -0.5)
        qb = q.astype(_BF16)
        c_old = c_ref[0, h]
        qk = _dot_nt(qb, k.astype(_BF16))
        qc = _dot(qb, c_old.astype(_BF16))
        return q, k, c_old, qk, qc

    def head_second(h, first):
        q, k, c_old, qk, qc = first
        vb = proj(h * ML_HEAD_COLS + OFF_V, ML_HD).astype(_BF16)
        r_col = cols[:, GC_R + h:GC_R + h + 1]
        wi_col = cols[:, GC_WI + h:GC_WI + h + 1]
        en_col = cols[:, GC_EN + h:GC_EN + h + 1]
        ws_col = cols[:, GC_WS + h:GC_WS + h + 1]
        wc = w_c[h:h + 1, 0:1]
        s_mat = qk * jnp.exp(jnp.where(causal, a[h:h + 1, :] - r_col, -jnp.inf))
        kw = k * ws_col
        sv = _dot(s_mat.astype(_BF16), vb)
        c_ref[0, h] = wc * c_old + _dot_tn(kw.astype(_BF16), vb)
        n_old = n_ref[0, h:h + 1, :]
        n_ref[0, h:h + 1, :] = wc * n_old + jnp.sum(kw, axis=0, keepdims=True)
        num = wi_col * qc + sv
        den = wi_col * jnp.sum(q * n_old, axis=1, keepdims=True) + jnp.sum(s_mat, axis=1, keepdims=True)
        hh = num * (1.0 / jnp.maximum(jnp.abs(den), en_col))
        return jax.nn.sigmoid(proj(h * ML_HEAD_COLS + OFF_O, ML_HD)) * hh

    def head_out(h, gated, z_all):
        hs = h * ML_HD
        y_ml = _rms(gated, ghead_ref[:, hs:hs + ML_HD]) * _silu(z_all[:, hs:hs + ML_HD])
        return out_part(y_ml.astype(_BF16), hs)

    def xa_scores(h):
        hs = h * XA_HD
        kh = mk_ref[0, :, hs:hs + XA_HD].astype(_BF16)
        return _dot_nt(xa_in[:, hs:hs + XA_HD].astype(_BF16), kh) * (XA_HD ** -0.5)

    def xa_out(h, s):
        hs = h * XA_HD
        vh = mv_ref[0, :, hs:hs + XA_HD].astype(_BF16)
        e = jnp.exp(s - jnp.max(s, axis=1, keepdims=True))
        p = e * (1.0 / jnp.sum(e, axis=1, keepdims=True))
        y_xa = _dot(p.astype(_BF16), vh) * _silu(xa_in[:, XA_WIDTH + hs:XA_WIDTH + hs + XA_HD])
        return y_xa.astype(_BF16)

    xa_in = proj(COL_QX, 2 * XA_WIDTH)
    conv_in = proj(COL_BG, 4 * CONV_CH)
    z_all = proj(COL_ZML, ML_WIDTH)

    s0, s1, s2, s3 = (xa_scores(h) for h in range(XA_HEADS))
    filler()
    first0 = head_first(0)
    first1 = head_first(1)
    y_xa0 = xa_out(0, s0)
    y_xa1 = xa_out(1, s1)
    filler()
    first2 = head_first(2)
    first3 = head_first(3)
    y_xa2 = xa_out(2, s2)
    y_xa3 = xa_out(3, s3)
    filler()

    b_g = conv_in[:, 0:CONV_CH]
    u = conv_in[:, CONV_CH:2 * CONV_CH] * conv_in[:, 2 * CONV_CH:3 * CONV_CH]
    ext_ref[SUBLANES:SUBLANES + tb, :] = u
    u1 = ext_ref[SUBLANES - 1:SUBLANES - 1 + tb, :]
    u2 = ext_ref[SUBLANES - 2:SUBLANES - 2 + tb, :]
    yc = convb_ref[...] + convw_ref[0:1, :] * u2 + convw_ref[1:2, :] * u1 + convw_ref[2:3, :] * u
    y_cv = b_g * yc * _silu(conv_in[:, 3 * CONV_CH:4 * CONV_CH])
    tail = u[tb - (CONV_K - 1):tb, :]
    ext_ref[SUBLANES - (CONV_K - 1):SUBLANES, :] = tail
    conv_ref[0] = tail
    gated0 = head_second(0, first0)
    filler()
    gated1 = head_second(1, first1)
    acc_ref[...] = out_part(jnp.concatenate([y_cv.astype(_BF16), y_xa0, y_xa1, y_xa2, y_xa3], axis=1), ML_WIDTH)
    filler()
    gated2 = head_second(2, first2)
    acc = head_out(0, gated0, z_all)
    filler()
    gated3 = head_second(3, first3)
    acc = acc + head_out(1, gated1, z_all)
    filler()
    acc = acc + head_out(2, gated2, z_all)
    acc = acc + head_out(3, gated3, z_all)
    filler()

    y_ref[0] = _rms(acc_ref[...] + acc + x, gfinal_ref[...])


def _prompt_kernel(xnext_ref, x_ref, mk_ref, mv_ref, gnorm_ref, w_ref, bif_ref, ghead_ref, convw_ref, convb_ref,
                   wout_ref, gfinal_ref,
                   y_ref, c_ref, n_ref, m_ref, conv_ref,
                   proj_even_ref, proj_odd_ref, ext_ref, acc_ref, *, blocks_per_seq):
    s = pl.program_id(0)

    @pl.when(s == 0)
    def _():
        proj_odd_ref[...] = jnp.zeros_like(proj_odd_ref)

    @pl.when((s == 0) | (lax.rem(s + blocks_per_seq - 1, blocks_per_seq) == 0))
    def _():
        c_ref[...] = jnp.zeros_like(c_ref)
        n_ref[...] = jnp.zeros_like(n_ref)
        m_ref[...] = jnp.zeros_like(m_ref)
        ext_ref[0:SUBLANES, :] = jnp.zeros((SUBLANES, CONV_CH), _F32)

    def step(proj_w_ref, proj_r_ref):
        xb_next = _rms(xnext_ref[0], gnorm_ref[...]).astype(_BF16)
        chunk = PROJ_COLS // LANES // N_FILLERS * LANES
        starts = iter(range(0, chunk * N_FILLERS, chunk))

        def project_chunk():
            c0 = next(starts)
            c1 = PROJ_COLS if c0 == chunk * (N_FILLERS - 1) else c0 + chunk
            proj_w_ref[:, c0:c1] = _dot(xb_next, w_ref[:, c0:c1])

        _finish_block(lambda col, width: proj_r_ref[:, col:col + width], project_chunk, x_ref[0],
                      mk_ref, mv_ref, bif_ref, ghead_ref, convw_ref, convb_ref, wout_ref, gfinal_ref,
                      y_ref, c_ref, n_ref, m_ref, conv_ref, ext_ref, acc_ref)

    @pl.when(lax.rem(s, 2) == 0)
    def _():
        step(proj_even_ref, proj_odd_ref)

    @pl.when(lax.rem(s, 2) == 1)
    def _():
        step(proj_odd_ref, proj_even_ref)


def _prompt_call(x, mk, mv, g_norm, w_all, b_if, g_head, conv_w, conv_b, w_out, g_final):
    b, t, _ = x.shape
    tb = PROMPT_BLOCK
    nt = t // tb
    n_blocks = b * nt
    const2 = lambda s: (0, 0)
    resident = functools.partial(pl.BlockSpec, index_map=const2, pipeline_mode=pl.Buffered(1))
    nxt = lambda s: jnp.minimum(s, n_blocks - 1)
    cur = lambda s: jnp.maximum(s - 1, 0)
    cur_batch3 = lambda s: (cur(s) // nt, 0, 0)
    return pl.pallas_call(
        functools.partial(_prompt_kernel, blocks_per_seq=nt),
        grid=(n_blocks + 1,),
        in_specs=[
            pl.BlockSpec((1, tb, D_MODEL), lambda s: (nxt(s) // nt, nxt(s) % nt, 0)),
            pl.BlockSpec((1, tb, D_MODEL), lambda s: (cur(s) // nt, cur(s) % nt, 0)),
            pl.BlockSpec((1, N_MEM, XA_WIDTH), cur_batch3),
            pl.BlockSpec((1, N_MEM, XA_WIDTH), cur_batch3),
            pl.BlockSpec((1, D_MODEL), const2),
            resident((D_MODEL, PROJ_COLS)),
            pl.BlockSpec((1, LANES), const2),
            pl.BlockSpec((1, ML_WIDTH), const2),
            pl.BlockSpec((CONV_K, CONV_CH), const2),
            pl.BlockSpec((1, CONV_CH), const2),
            resident((MIX_WIDTH, D_MODEL)),
            pl.BlockSpec((1, D_MODEL), const2),
        ],
        out_specs=[
            pl.BlockSpec((1, tb, D_MODEL), lambda s: (cur(s) // nt, cur(s) % nt, 0)),
            pl.BlockSpec((1, ML_HEADS, ML_HD, ML_HD), lambda s: (cur(s) // nt, 0, 0, 0)),
            pl.BlockSpec((1, ML_HEADS, ML_HD), cur_batch3),
            pl.BlockSpec((1, SUBLANES, LANES), cur_batch3),
            pl.BlockSpec((1, CONV_K - 1, CONV_CH), cur_batch3),
        ],
        out_shape=[
            jax.ShapeDtypeStruct((b, t, D_MODEL), _F32),
            jax.ShapeDtypeStruct((b, ML_HEADS, ML_HD, ML_HD), _F32),
            jax.ShapeDtypeStruct((b, ML_HEADS, ML_HD), _F32),
            jax.ShapeDtypeStruct((b, SUBLANES, LANES), _F32),
            jax.ShapeDtypeStruct((b, CONV_K - 1, CONV_CH), _F32),
        ],
        scratch_shapes=[
            pltpu.VMEM((tb, PROJ_COLS), _F32),
            pltpu.VMEM((tb, PROJ_COLS), _F32),
            pltpu.VMEM((SUBLANES + tb, CONV_CH), _F32),
            pltpu.VMEM((tb, D_MODEL), _F32),
        ],
        compiler_params=pltpu.CompilerParams(
            dimension_semantics=("arbitrary",), vmem_limit_bytes=VMEM_LIMIT),
        name="prompt_layer",
    )(x, x, mk, mv, g_norm, w_all, b_if, g_head, conv_w, conv_b, w_out, g_final)


def _inproj_kernel(x_ref, gnorm_ref, w_ref, bif_ref, m0_ref, proj_ref, cols_ref, rows_ref, xb_ref):
    j = pl.program_id(0)

    @pl.when(j == 0)
    def _():
        xb_ref[...] = _rms(x_ref[...], gnorm_ref[...]).astype(_BF16)

    p = _dot(xb_ref[...], w_ref[...])
    proj_ref[...] = p

    @pl.when(j == COL_IF // INPROJ_TILE)
    def _():
        off = COL_IF % INPROJ_TILE
        slab = (p[:, off:off + LANES] + bif_ref[...]).T[0:SUBLANES, :]
        a, r, m_t, w_inter, e_negm, w_s, w_c = _gate_rows(slab, m0_ref[...], DEC_SEQ_LEN)
        cols_ref[...] = _gate_cols(r, w_inter, e_negm, w_s, w_c)
        rows_ref[0:SUBLANES, :] = a
        rows_ref[SUBLANES:2 * SUBLANES, :] = m_t


def _inproj_call(x, g_norm, w_all, b_if, m0_rows):
    n_tok = x.shape[0]
    const = lambda j: (0, 0)
    return pl.pallas_call(
        _inproj_kernel,
        grid=(PROJ_COLS // INPROJ_TILE,),
        in_specs=[
            pl.BlockSpec((n_tok, D_MODEL), const),
            pl.BlockSpec((1, D_MODEL), const),
            pl.BlockSpec((D_MODEL, INPROJ_TILE), lambda j: (0, j)),
            pl.BlockSpec((1, LANES), const),
            pl.BlockSpec((SUBLANES, n_tok), const),
        ],
        out_specs=[
            pl.BlockSpec((n_tok, INPROJ_TILE), lambda j: (0, j)),
            pl.BlockSpec((n_tok, LANES), const),
            pl.BlockSpec((2 * SUBLANES, n_tok), const),
        ],
        out_shape=[
            jax.ShapeDtypeStruct((n_tok, PROJ_COLS), _F32),
            jax.ShapeDtypeStruct((n_tok, LANES), _F32),
            jax.ShapeDtypeStruct((2 * SUBLANES, n_tok), _F32),
        ],
        scratch_shapes=[pltpu.VMEM((n_tok, D_MODEL), _BF16)],
        compiler_params=pltpu.CompilerParams(
            dimension_semantics=("arbitrary",), vmem_limit_bytes=VMEM_LIMIT),
        name="sample_inproj",
    )(x, g_norm, w_all, b_if, m0_rows)


def _sample_kernel(proj_ref, cols_ref, rows_ref, c0_ref, n0_ref, conv0_ref, mk_ref, mv_ref,
                   ghead_ref, convw_ref, convb_ref,
                   ymix_ref, c_ref, n_ref, conv_ref,
                   sv_ref, rs_ref, ext_ref):
    nt = SAMPLE_BLOCK_SEQS * DEC_SEQ_LEN
    j = pl.program_id(1)

    @pl.when(j == 0)
    def _():
        row = lax.broadcasted_iota(jnp.int32, (nt, nt), 0)
        col = lax.broadcasted_iota(jnp.int32, (nt, nt), 1)
        mask = (col <= row) & ((row // DEC_SEQ_LEN) == (col // DEC_SEQ_LEN))
        a = rows_ref[0:SUBLANES, :]
        for h in range(ML_HEADS):
            hs = h * ML_HD
            hc = h * ML_HEAD_COLS
            qb = proj_ref[:, hc + OFF_Q:hc + OFF_Q + ML_HD].astype(_BF16)
            kb = (proj_ref[:, hc + OFF_K:hc + OFF_K + ML_HD] * (ML_HD ** -0.5)).astype(_BF16)
            vb = proj_ref[:, hc + OFF_V:hc + OFF_V + ML_HD].astype(_BF16)
            r_col = cols_ref[:, GC_R + h:GC_R + h + 1]
            decay = jnp.exp(jnp.where(mask, a[h:h + 1, :] - r_col, -jnp.inf))
            s_mat = _dot_nt(qb, kb) * decay
            sv_ref[:, hs:hs + ML_HD] = _dot(s_mat.astype(_BF16), vb)
            rs_ref[:, h:h + 1] = jnp.sum(s_mat, axis=1, keepdims=True)

    xa_rows = XA_HEADS * DEC_SEQ_LEN
    own_head = (lax.broadcasted_iota(jnp.int32, (xa_rows, XA_KV_ROWS), 0) // DEC_SEQ_LEN
                == lax.broadcasted_iota(jnp.int32, (xa_rows, XA_KV_ROWS), 1) % XA_HEADS)

    for sl in range(SAMPLE_STEP_SEQS):
        ro = pl.multiple_of((j * SAMPLE_STEP_SEQS + sl) * DEC_SEQ_LEN, DEC_SEQ_LEN)
        rows = pl.ds(ro, DEC_SEQ_LEN)
        gc = cols_ref[rows, :]
        for h in range(ML_HEADS):
            hs = h * ML_HD
            hc = h * ML_HEAD_COLS
            q = proj_ref[rows, hc + OFF_Q:hc + OFF_Q + ML_HD]
            k = proj_ref[rows, hc + OFF_K:hc + OFF_K + ML_HD] * (ML_HD ** -0.5)
            vb = proj_ref[rows, hc + OFF_V:hc + OFF_V + ML_HD].astype(_BF16)
            wi_col = gc[:, GC_WI + h:GC_WI + h + 1]
            en_col = gc[:, GC_EN + h:GC_EN + h + 1]
            ws_col = gc[:, GC_WS + h:GC_WS + h + 1]
            wc = gc[0:1, GC_WC + h:GC_WC + h + 1]
            c_old = c0_ref[sl, h]
            n_old = n0_ref[sl, h:h + 1, :]
            num = wi_col * _dot(q.astype(_BF16), c_old.astype(_BF16)) + sv_ref[rows, hs:hs + ML_HD]
            den = wi_col * jnp.sum(q * n_old, axis=1, keepdims=True) + rs_ref[rows, h:h + 1]
            hh = num * (1.0 / jnp.maximum(jnp.abs(den), en_col))
            kw = k * ws_col
            c_ref[sl, h] = wc * c_old + _dot_tn(kw.astype(_BF16), vb)
            n_ref[sl, h:h + 1, :] = wc * n_old + jnp.sum(kw, axis=0, keepdims=True)
            y_ml = _ml_head_out(hh, proj_ref[rows, hc + OFF_O:hc + OFF_O + ML_HD],
                                proj_ref[rows, COL_ZML + hs:COL_ZML + hs + ML_HD], ghead_ref[:, hs:hs + ML_HD])
            ymix_ref[rows, hs:hs + ML_HD] = y_ml

        u = proj_ref[rows, COL_CG:COL_CG + CONV_CH] * proj_ref[rows, COL_XC:COL_XC + CONV_CH]
        ext_ref[sl, SUBLANES - (CONV_K - 1):SUBLANES, :] = conv0_ref[sl]
        ext_ref[sl, SUBLANES:2 * SUBLANES, :] = u
        u1 = ext_ref[sl, SUBLANES - 1:2 * SUBLANES - 1, :]
        u2 = ext_ref[sl, SUBLANES - 2:2 * SUBLANES - 2, :]
        yc = convb_ref[...] + convw_ref[0:1, :] * u2 + convw_ref[1:2, :] * u1 + convw_ref[2:3, :] * u
        y_cv = (proj_ref[rows, COL_BG:COL_BG + CONV_CH] * yc
                * _silu(proj_ref[rows, COL_ZCV:COL_ZCV + CONV_CH]))
        ymix_ref[rows, ML_WIDTH:ML_WIDTH + CONV_CH] = y_cv
        conv_ref[sl] = u[DEC_SEQ_LEN - (CONV_K - 1):DEC_SEQ_LEN, :]

        q4 = jnp.concatenate(
            [proj_ref[rows, COL_QX + h * XA_HD:COL_QX + (h + 1) * XA_HD] for h in range(XA_HEADS)], axis=0)
        s = _dot_nt(q4.astype(_BF16), mk_ref[sl].astype(_BF16)) * (XA_HD ** -0.5)
        s = jnp.where(own_head, s, -jnp.inf)
        e = jnp.exp(s - jnp.max(s, axis=1, keepdims=True))
        p = e * (1.0 / jnp.sum(e, axis=1, keepdims=True))
        y4 = _dot(p.astype(_BF16), mv_ref[sl].astype(_BF16))
        for h in range(XA_HEADS):
            hs = h * XA_HD
            y_xa = y4[h * DEC_SEQ_LEN:(h + 1) * DEC_SEQ_LEN, :] * _silu(proj_ref[rows, COL_ZX + hs:COL_ZX + hs + XA_HD])
            c0 = ML_WIDTH + CONV_CH + hs
            ymix_ref[rows, c0:c0 + XA_HD] = y_xa


def _sample_call(proj, cols, rows, c0, n0, conv0, mk, mv, g_head, conv_w, conv_b):
    n_seq = c0.shape[0]
    nt = SAMPLE_BLOCK_SEQS * DEC_SEQ_LEN
    steps = SAMPLE_BLOCK_SEQS // SAMPLE_STEP_SEQS
    ss = SAMPLE_STEP_SEQS
    const2 = lambda i, j: (0, 0)
    blk2 = lambda i, j: (i, 0)
    seq3 = lambda i, j: (i * steps + j, 0, 0)
    seq4 = lambda i, j: (i * steps + j, 0, 0, 0)
    return pl.pallas_call(
        _sample_kernel,
        grid=(n_seq // SAMPLE_BLOCK_SEQS, steps),
        in_specs=[
            pl.BlockSpec((nt, PROJ_COLS), blk2),
            pl.BlockSpec((nt, LANES), blk2),
            pl.BlockSpec((2 * SUBLANES, nt), lambda i, j: (0, i)),
            pl.BlockSpec((ss, ML_HEADS, ML_HD, ML_HD), seq4),
            pl.BlockSpec((ss, ML_HEADS, ML_HD), seq3),
            pl.BlockSpec((ss, CONV_K - 1, CONV_CH), seq3),
            pl.BlockSpec((ss, XA_KV_ROWS, XA_HD), seq3),
            pl.BlockSpec((ss, XA_KV_ROWS, XA_HD), seq3),
            pl.BlockSpec((1, ML_WIDTH), const2),
            pl.BlockSpec((CONV_K, CONV_CH), const2),
            pl.BlockSpec((1, CONV_CH), const2),
        ],
        out_specs=[
            pl.BlockSpec((nt, MIX_WIDTH), blk2),
            pl.BlockSpec((ss, ML_HEADS, ML_HD, ML_HD), seq4),
            pl.BlockSpec((ss, ML_HEADS, ML_HD), seq3),
            pl.BlockSpec((ss, CONV_K - 1, CONV_CH), seq3),
        ],
        out_shape=[
            jax.ShapeDtypeStruct((n_seq * DEC_SEQ_LEN, MIX_WIDTH), _F32),
            jax.ShapeDtypeStruct(c0.shape, _F32),
            jax.ShapeDtypeStruct(n0.shape, _F32),
            jax.ShapeDtypeStruct(conv0.shape, _F32),
        ],
        scratch_shapes=[
            pltpu.VMEM((nt, ML_WIDTH), _F32),
            pltpu.VMEM((nt, LANES), _F32),
            pltpu.VMEM((ss, 2 * SUBLANES, CONV_CH), _F32),
        ],
        compiler_params=pltpu.CompilerParams(
            dimension_semantics=("arbitrary", "arbitrary"), vmem_limit_bytes=VMEM_LIMIT),
        name="sample_state",
    )(proj, cols, rows, c0, n0, conv0, mk, mv, g_head, conv_w, conv_b)


def _outproj_kernel(ymix_ref, x_ref, wout_ref, gfinal_ref, y_ref):
    y_ref[...] = _out_tail(ymix_ref[...].astype(_BF16), wout_ref[...], x_ref[...], gfinal_ref[...])


def _outproj_call(ymix, x, w_out, g_final):
    n_tok = x.shape[0]
    rb = 256
    const = lambda i: (0, 0)
    return pl.pallas_call(
        _outproj_kernel,
        grid=(n_tok // rb,),
        in_specs=[
            pl.BlockSpec((rb, MIX_WIDTH), lambda i: (i, 0)),
            pl.BlockSpec((rb, D_MODEL), lambda i: (i, 0)),
            pl.BlockSpec((MIX_WIDTH, D_MODEL), const),
            pl.BlockSpec((1, D_MODEL), const),
        ],
        out_specs=pl.BlockSpec((rb, D_MODEL), lambda i: (i, 0)),
        out_shape=jax.ShapeDtypeStruct((n_tok, D_MODEL), _F32),
        compiler_params=pltpu.CompilerParams(dimension_semantics=("arbitrary",)),
        name="sample_outproj",
    )(ymix, x, w_out, g_final)


def kernel(x_prompt, x_sample, state_mlstm_C, state_mlstm_n, state_mlstm_m, state_conv, cache_mem_k, cache_mem_v, mem_prompt, g_norm, w_in, b_if, g_head, conv_w, conv_b, g_mem, w_mem_kv, w_out, g_final):
    assert w_in.shape[0] == 1, "single-layer trunk"
    bp, tp, _ = x_prompt.shape
    bs, ts, _ = x_sample.shape
    assert ts == DEC_SEQ_LEN and tp % PROMPT_BLOCK == 0 and bs % SAMPLE_BLOCK_SEQS == 0

    w_all = _wprep_call(jnp.swapaxes(w_in[0], 0, 1))
    w_out_b = w_out[0].astype(_BF16)
    w_kv_b = w_mem_kv[0].astype(_BF16)
    b_if_p = jnp.pad(b_if[0][None, :], ((0, 0), (0, LANES - N_IF)))
    row = lambda v: v.reshape(1, -1)

    mk_p, mv_p, mk_rows, mv_rows = _memkv_call(mem_prompt, row(g_mem[0]), w_kv_b)
    y_p, c_p, n_p, m_p, conv_p = _prompt_call(
        x_prompt, mk_p, mv_p, row(g_norm[0]), w_all, b_if_p, row(g_head[0]), conv_w[0], row(conv_b[0]),
        w_out_b, row(g_final))

    n_tok = bs * ts
    xs = x_sample.reshape(n_tok, D_MODEL)
    m0_rows = jnp.pad(jnp.repeat(state_mlstm_m[0].T, ts, axis=1), ((0, SUBLANES - ML_HEADS), (0, 0)))
    proj, cols, rows = _inproj_call(xs, row(g_norm[0]), w_all, b_if_p, m0_rows)
    ymix, c_s, n_s, conv_s = _sample_call(
        proj, cols, rows, state_mlstm_C[0], state_mlstm_n[0], state_conv[0],
        cache_mem_k[0].reshape(bs, XA_KV_ROWS, XA_HD), cache_mem_v[0].reshape(bs, XA_KV_ROWS, XA_HD),
        row(g_head[0]), conv_w[0], row(conv_b[0]))
    y_s = _outproj_call(ymix, xs, w_out_b, row(g_final))
    m_s = rows[SUBLANES:SUBLANES + ML_HEADS, ts - 1::ts].T

    kv_shape = (1, bp, N_MEM, XA_HEADS, XA_HD)
    return (y_p, y_s.reshape(bs, ts, D_MODEL),
            c_p[None], n_p[None], m_p[:, :ML_HEADS, 0][None], conv_p[None],
            mk_rows.reshape(kv_shape), mv_rows.reshape(kv_shape),
            c_s[None], n_s[None], m_s[None], conv_s[None])
```

```python
import functools

import jax
import jax.numpy as jnp
from jax import lax
from jax.experimental import pallas as pl
from jax.experimental.pallas import tpu as pltpu

D_MODEL = 1024
ML_HEADS = 4
ML_HD = 256
ML_WIDTH = ML_HEADS * ML_HD
CONV_CH = 512
CONV_K = 3
XA_HEADS = 4
XA_HD = 128
XA_WIDTH = XA_HEADS * XA_HD
N_MEM = 256
XA_KV_ROWS = N_MEM * XA_HEADS
DEC_SEQ_LEN = 8
MIX_WIDTH = ML_WIDTH + CONV_CH + XA_WIDTH
EPS = 1e-6

SUBLANES = 8
LANES = 128

ML_HEAD_COLS = ML_HD
OFF_Q = 0
OFF_K = ML_WIDTH
OFF_V = 2 * ML_WIDTH
OFF_O = 3 * ML_WIDTH
COL_ZML = 4 * ML_WIDTH
COL_IF = 5 * ML_WIDTH
COL_BG = COL_IF + LANES
COL_CG = COL_BG + CONV_CH
COL_XC = COL_CG + CONV_CH
COL_ZCV = COL_XC + CONV_CH
COL_QX = COL_ZCV + CONV_CH
COL_ZX = COL_QX + XA_WIDTH
PROJ_COLS = COL_ZX + XA_WIDTH
N_IF = 2 * ML_HEADS

GC_R = 0 * SUBLANES
GC_WI = 1 * SUBLANES
GC_EN = 2 * SUBLANES
GC_WS = 3 * SUBLANES
GC_WC = 4 * SUBLANES
GC_USED = 5 * SUBLANES

PROMPT_BLOCK = 256
SAMPLE_BLOCK_SEQS = 16
SAMPLE_STEP_SEQS = 4
INPROJ_TILE = 640
VMEM_LIMIT = 60 * 1024 * 1024

_F32 = jnp.float32
_BF16 = jnp.bfloat16


def _dot(a, b):
    return jnp.dot(a, b, preferred_element_type=_F32)


def _dot_nt(a, b):
    return lax.dot_general(a, b, (((1,), (1,)), ((), ())), preferred_element_type=_F32)


def _dot_tn(a, b):
    return lax.dot_general(a, b, (((0,), (0,)), ((), ())), preferred_element_type=_F32)


def _rms(x, g):
    r = lax.rsqrt(jnp.mean(x * x, axis=-1, keepdims=True) + EPS)
    return (x * r) * g


def _silu(x):
    return x * jax.nn.sigmoid(x)


def _log_sigmoid(x):
    return jnp.minimum(x, 0.0) - jnp.log1p(jnp.exp(-jnp.abs(x)))


def _gate_rows(slab, m_prev, seg):
    n = slab.shape[1]
    pos = lax.broadcasted_iota(jnp.int32, slab.shape, 1) & (seg - 1)

    def scan(x, op, fill, reverse=False):
        k = 1
        while k < seg:
            if reverse:
                shifted, ok = pltpu.roll(x, n - k, axis=1), pos < seg - k
            else:
                shifted, ok = pltpu.roll(x, k, axis=1), pos >= k
            x = op(x, jnp.where(ok, shifted, fill))
            k *= 2
        return x

    logf = _log_sigmoid(pltpu.roll(slab, ML_HEADS, axis=0))
    f_cum = scan(logf, jnp.add, 0.0)
    a = slab - f_cum
    r = jnp.maximum(scan(a, jnp.maximum, -jnp.inf), m_prev)
    r_last = scan(r, jnp.maximum, -jnp.inf, reverse=True)
    m_t = f_cum + r
    w_inter = jnp.exp(m_prev - r)
    e_negm = jnp.exp(-m_t)
    w_s = jnp.exp(a - r_last)
    w_c = jnp.exp(m_prev - r_last)
    return a, r, m_t, w_inter, e_negm, w_s, w_c


def _gate_cols(r, w_inter, e_negm, w_s, w_c):
    n = r.shape[1]
    pad = jnp.zeros((LANES - GC_USED, n), _F32)
    return jnp.concatenate([r, w_inter, e_negm, w_s, w_c, pad], axis=0).T


def _ml_head_out(hh, o_pre, z, g_head):
    hh = jax.nn.sigmoid(o_pre) * hh
    return _rms(hh, g_head) * _silu(z)


def _out_tail(ymix_bf16, w_out, x, g_final):
    y = _dot(ymix_bf16, w_out) + x
    return _rms(y, g_final)


WPREP_PAD = LANES - N_IF


def _wprep_kernel(wt_ref, out_ref):
    j = pl.program_id(0)
    gate_tile = COL_IF // INPROJ_TILE

    @pl.when(j != gate_tile)
    def _():
        out_ref[...] = wt_ref[...].T.astype(_BF16)

    @pl.when(j == gate_tile)
    def _():
        rows = wt_ref[...]
        padded = jnp.concatenate(
            [rows[0:N_IF], jnp.zeros((WPREP_PAD, D_MODEL), _F32), rows[N_IF:INPROJ_TILE - WPREP_PAD]], axis=0)
        out_ref[...] = padded.T.astype(_BF16)


def _wprep_call(w_t):
    assert COL_IF % INPROJ_TILE == 0 and PROJ_COLS % INPROJ_TILE == 0 and N_IF % SUBLANES == 0
    gate_tile = COL_IF // INPROJ_TILE
    first_row = lambda j: (j * (INPROJ_TILE // SUBLANES)
                           - (j > gate_tile).astype(jnp.int32) * (WPREP_PAD // SUBLANES)) * SUBLANES
    return pl.pallas_call(
        _wprep_kernel,
        grid=(PROJ_COLS // INPROJ_TILE,),
        in_specs=[pl.BlockSpec((pl.Element(INPROJ_TILE), pl.Element(D_MODEL)), lambda j: (first_row(j), 0))],
        out_specs=pl.BlockSpec((D_MODEL, INPROJ_TILE), lambda j: (0, j)),
        out_shape=jax.ShapeDtypeStruct((D_MODEL, PROJ_COLS), _BF16),
        compiler_params=pltpu.CompilerParams(dimension_semantics=("arbitrary",)),
        name="weight_repack",
    )(w_t)


def _memkv_kernel(mem_ref, g_ref, w_ref, mk_ref, mv_ref, mk_rows_ref, mv_rows_ref):
    xn = _rms(mem_ref[0], g_ref[...]).astype(_BF16)
    kv = _dot(xn, w_ref[...])
    mk_ref[0] = kv[:, :XA_WIDTH]
    mv_ref[0] = kv[:, XA_WIDTH:]
    for h in range(XA_HEADS):
        head_rows = pl.ds(h, N_MEM, stride=XA_HEADS)
        mk_rows_ref[0, head_rows, :] = kv[:, h * XA_HD:(h + 1) * XA_HD]
        mv_rows_ref[0, head_rows, :] = kv[:, XA_WIDTH + h * XA_HD:XA_WIDTH + (h + 1) * XA_HD]


def _memkv_call(mem, g_mem, w_kv):
    b = mem.shape[0]
    const = lambda i: (0, 0)
    per_batch = lambda i: (i, 0, 0)
    return pl.pallas_call(
        _memkv_kernel,
        grid=(b,),
        in_specs=[
            pl.BlockSpec((1, N_MEM, D_MODEL), per_batch),
            pl.BlockSpec((1, D_MODEL), const),
            pl.BlockSpec((D_MODEL, 2 * XA_WIDTH), const),
        ],
        out_specs=[
            pl.BlockSpec((1, N_MEM, XA_WIDTH), per_batch),
            pl.BlockSpec((1, N_MEM, XA_WIDTH), per_batch),
            pl.BlockSpec((1, XA_KV_ROWS, XA_HD), per_batch),
            pl.BlockSpec((1, XA_KV_ROWS, XA_HD), per_batch),
        ],
        out_shape=[jax.ShapeDtypeStruct((b, N_MEM, XA_WIDTH), _F32)] * 2
        + [jax.ShapeDtypeStruct((b, XA_KV_ROWS, XA_HD), _F32)] * 2,
        compiler_params=pltpu.CompilerParams(dimension_semantics=("arbitrary",)),
        name="memkv",
    )(mem, g_mem, w_kv)


N_FILLERS = 8


def _finish_block(proj, filler, x, mk_ref, mv_ref, bif_ref, ghead_ref, convw_ref, convb_ref, wout_ref, gfinal_ref,
                  y_ref, c_ref, n_ref, m_ref, conv_ref, ext_ref, acc_ref):
    tb = PROMPT_BLOCK

    def out_part(y_bf16, row0):
        return _dot(y_bf16, wout_ref[row0:row0 + y_bf16.shape[1], :])

    gates = proj(COL_IF, LANES)
    slab = (gates + bif_ref[...]).T[0:SUBLANES, :]
    m_prev8 = m_ref[0]
    m_prev = jnp.concatenate([m_prev8] * (tb // LANES), axis=1)
    a, r, m_t, w_inter, e_negm, w_s, w_c = _gate_rows(slab, m_prev, tb)
    cols = _gate_cols(r, w_inter, e_negm, w_s, w_c)
    m_ref[0] = jnp.broadcast_to(m_t[:, tb - 1:tb], (SUBLANES, LANES))

    row = lax.broadcasted_iota(jnp.int32, (tb, tb), 0)
    col = lax.broadcasted_iota(jnp.int32, (tb, tb), 1)
    causal = col <= row

    def head_first(h):
        q = proj(h * ML_HEAD_COLS + OFF_Q, ML_HD)
        k = proj(h * ML_HEAD_COLS + OFF_K, ML_HD) * (ML_HD ** -0.5)
        qb = q.astype(_BF16)
        c_old = c_ref[0, h]
        qk = _dot_nt(qb, k.astype(_BF16))
        qc = _dot(qb, c_old.astype(_BF16))
        return q, k, c_old, qk, qc

    def head_second(h, first):
        q, k, c_old, qk, qc = first
        vb = proj(h * ML_HEAD_COLS + OFF_V, ML_HD).astype(_BF16)
        r_col = cols[:, GC_R + h:GC_R + h + 1]
        wi_col = cols[:, GC_WI + h:GC_WI + h + 1]
        en_col = cols[:, GC_EN + h:GC_EN + h + 1]
        ws_col = cols[:, GC_WS + h:GC_WS + h + 1]
        wc = w_c[h:h + 1, 0:1]
        s_mat = qk * jnp.exp(jnp.where(causal, a[h:h + 1, :] - r_col, -jnp.inf))
        kw = k * ws_col
        sv = _dot(s_mat.astype(_BF16), vb)
        c_ref[0, h] = wc * c_old + _dot_tn(kw.astype(_BF16), vb)
        n_old = n_ref[0, h:h + 1, :]
        n_ref[0, h:h + 1, :] = wc * n_old + jnp.sum(kw, axis=0, keepdims=True)
        num = wi_col * qc + sv
        den = wi_col * jnp.sum(q * n_old, axis=1, keepdims=True) + jnp.sum(s_mat, axis=1, keepdims=True)
        hh = num * (1.0 / jnp.maximum(jnp.abs(den), en_col))
        return jax.nn.sigmoid(proj(h * ML_HEAD_COLS + OFF_O, ML_HD)) * hh

    def head_out(h, gated, z_all):
        hs = h * ML_HD
        y_ml = _rms(gated, ghead_ref[:, hs:hs + ML_HD]) * _silu(z_all[:, hs:hs + ML_HD])
        return out_part(y_ml.astype(_BF16), hs)

    def xa_scores(h):
        hs = h * XA_HD
        kh = mk_ref[0, :, hs:hs + XA_HD].astype(_BF16)
        return _dot_nt(xa_in[:, hs:hs + XA_HD].astype(_BF16), kh) * (XA_HD ** -0.5)

    def xa_out(h, s):
        hs = h * XA_HD
        vh = mv_ref[0, :, hs:hs + XA_HD].astype(_BF16)
        e = jnp.exp(s - jnp.max(s, axis=1, keepdims=True))
        p = e * (1.0 / jnp.sum(e, axis=1, keepdims=True))
        y_xa = _dot(p.astype(_BF16), vh) * _silu(xa_in[:, XA_WIDTH + hs:XA_WIDTH + hs + XA_HD])
        return y_xa.astype(_BF16)

    xa_in = proj(COL_QX, 2 * XA_WIDTH)
    conv_in = proj(COL_BG, 4 * CONV_CH)
    z_all = proj(COL_ZML, ML_WIDTH)

    s0, s1, s2, s3 = (xa_scores(h) for h in range(XA_HEADS))
    filler()
    first0 = head_first(0)
    first1 = head_first(1)
    y_xa0 = xa_out(0, s0)
    y_xa1 = xa_out(1, s1)
    filler()
    first2 = head_first(2)
    first3 = head_first(3)
    y_xa2 = xa_out(2, s2)
    y_xa3 = xa_out(3, s3)
    filler()

    b_g = conv_in[:, 0:CONV_CH]
    u = conv_in[:, CONV_CH:2 * CONV_CH] * conv_in[:, 2 * CONV_CH:3 * CONV_CH]
    ext_ref[SUBLANES:SUBLANES + tb, :] = u
    u1 = ext_ref[SUBLANES - 1:SUBLANES - 1 + tb, :]
    u2 = ext_ref[SUBLANES - 2:SUBLANES - 2 + tb, :]
    yc = convb_ref[...] + convw_ref[0:1, :] * u2 + convw_ref[1:2, :] * u1 + convw_ref[2:3, :] * u
    y_cv = b_g * yc * _silu(conv_in[:, 3 * CONV_CH:4 * CONV_CH])
    tail = u[tb - (CONV_K - 1):tb, :]
    ext_ref[SUBLANES - (CONV_K - 1):SUBLANES, :] = tail
    conv_ref[0] = tail
    gated0 = head_second(0, first0)
    filler()
    gated1 = head_second(1, first1)
    acc_ref[...] = out_part(jnp.concatenate([y_cv.astype(_BF16), y_xa0, y_xa1, y_xa2, y_xa3], axis=1), ML_WIDTH)
    filler()
    gated2 = head_second(2, first2)
    acc = head_out(0, gated0, z_all)
    filler()
    gated3 = head_second(3, first3)
    acc = acc + head_out(1, gated1, z_all)
    filler()
    acc = acc + head_out(2, gated2, z_all)
    acc = acc + head_out(3, gated3, z_all)
    filler()

    y_ref[0] = _rms(acc_ref[...] + acc + x, gfinal_ref[...])


def _prompt_kernel(xnext_ref, x_ref, mk_ref, mv_ref, gnorm_ref, w_ref, bif_ref, ghead_ref, convw_ref, convb_ref,
                   wout_ref, gfinal_ref,
                   y_ref, c_ref, n_ref, m_ref, conv_ref,
                   proj_even_ref, proj_odd_ref, ext_ref, acc_ref, *, blocks_per_seq):
    s = pl.program_id(0)

    @pl.when(s == 0)
    def _():
        proj_odd_ref[...] = jnp.zeros_like(proj_odd_ref)

    @pl.when((s == 0) | (lax.rem(s + blocks_per_seq - 1, blocks_per_seq) == 0))
    def _():
        c_ref[...] = jnp.zeros_like(c_ref)
        n_ref[...] = jnp.zeros_like(n_ref)
        m_ref[...] = jnp.zeros_like(m_ref)
        ext_ref[0:SUBLANES, :] = jnp.zeros((SUBLANES, CONV_CH), _F32)

    def step(proj_w_ref, proj_r_ref):
        xb_next = _rms(xnext_ref[0], gnorm_ref[...]).astype(_BF16)
        chunk = PROJ_COLS // LANES // N_FILLERS * LANES
        starts = iter(range(0, chunk * N_FILLERS, chunk))

        def project_chunk():
            c0 = next(starts)
            c1 = PROJ_COLS if c0 == chunk * (N_FILLERS - 1) else c0 + chunk
            proj_w_ref[:, c0:c1] = _dot(xb_next, w_ref[:, c0:c1])

        _finish_block(lambda col, width: proj_r_ref[:, col:col + width], project_chunk, x_ref[0],
                      mk_ref, mv_ref, bif_ref, ghead_ref, convw_ref, convb_ref, wout_ref, gfinal_ref,
                      y_ref, c_ref, n_ref, m_ref, conv_ref, ext_ref, acc_ref)

    @pl.when(lax.rem(s, 2) == 0)
    def _():
        step(proj_even_ref, proj_odd_ref)

    @pl.when(lax.rem(s, 2) == 1)
    def _():
        step(proj_odd_ref, proj_even_ref)


def _prompt_call(x, mk, mv, g_norm, w_all, b_if, g_head, conv_w, conv_b, w_out, g_final):
    b, t, _ = x.shape
    tb = PROMPT_BLOCK
    nt = t // tb
    n_blocks = b * nt
    const2 = lambda s: (0, 0)
    resident = functools.partial(pl.BlockSpec, index_map=const2, pipeline_mode=pl.Buffered(1))
    nxt = lambda s: jnp.minimum(s, n_blocks - 1)
    cur = lambda s: jnp.maximum(s - 1, 0)
    cur_batch3 = lambda s: (cur(s) // nt, 0, 0)
    return pl.pallas_call(
        functools.partial(_prompt_kernel, blocks_per_seq=nt),
        grid=(n_blocks + 1,),
        in_specs=[
            pl.BlockSpec((1, tb, D_MODEL), lambda s: (nxt(s) // nt, nxt(s) % nt, 0)),
            pl.BlockSpec((1, tb, D_MODEL), lambda s: (cur(s) // nt, cur(s) % nt, 0)),
            pl.BlockSpec((1, N_MEM, XA_WIDTH), cur_batch3),
            pl.BlockSpec((1, N_MEM, XA_WIDTH), cur_batch3),
            pl.BlockSpec((1, D_MODEL), const2),
            resident((D_MODEL, PROJ_COLS)),
            pl.BlockSpec((1, LANES), const2),
            pl.BlockSpec((1, ML_WIDTH), const2),
            pl.BlockSpec((CONV_K, CONV_CH), const2),
            pl.BlockSpec((1, CONV_CH), const2),
            resident((MIX_WIDTH, D_MODEL)),
            pl.BlockSpec((1, D_MODEL), const2),
        ],
        out_specs=[
            pl.BlockSpec((1, tb, D_MODEL), lambda s: (cur(s) // nt, cur(s) % nt, 0)),
            pl.BlockSpec((1, ML_HEADS, ML_HD, ML_HD), lambda s: (cur(s) // nt, 0, 0, 0)),
            pl.BlockSpec((1, ML_HEADS, ML_HD), cur_batch3),
            pl.BlockSpec((1, SUBLANES, LANES), cur_batch3),
            pl.BlockSpec((1, CONV_K - 1, CONV_CH), cur_batch3),
        ],
        out_shape=[
            jax.ShapeDtypeStruct((b, t, D_MODEL), _F32),
            jax.ShapeDtypeStruct((b, ML_HEADS, ML_HD, ML_HD), _F32),
            jax.ShapeDtypeStruct((b, ML_HEADS, ML_HD), _F32),
            jax.ShapeDtypeStruct((b, SUBLANES, LANES), _F32),
            jax.ShapeDtypeStruct((b, CONV_K - 1, CONV_CH), _F32),
        ],
        scratch_shapes=[
            pltpu.VMEM((tb, PROJ_COLS), _F32),
            pltpu.VMEM((tb, PROJ_COLS), _F32),
            pltpu.VMEM((SUBLANES + tb, CONV_CH), _F32),
            pltpu.VMEM((tb, D_MODEL), _F32),
        ],
        compiler_params=pltpu.CompilerParams(
            dimension_semantics=("arbitrary",), vmem_limit_bytes=VMEM_LIMIT),
        name="prompt_layer",
    )(x, x, mk, mv, g_norm, w_all, b_if, g_head, conv_w, conv_b, w_out, g_final)


def _inproj_kernel(x_ref, gnorm_ref, w_ref, bif_ref, m0_ref, proj_ref, cols_ref, rows_ref, xb_ref):
    j = pl.program_id(0)

    @pl.when(j == 0)
    def _():
        xb_ref[...] = _rms(x_ref[...], gnorm_ref[...]).astype(_BF16)

    p = _dot(xb_ref[...], w_ref[...])
    proj_ref[...] = p

    @pl.when(j == COL_IF // INPROJ_TILE)
    def _():
        off = COL_IF % INPROJ_TILE
        slab = (p[:, off:off + LANES] + bif_ref[...]).T[0:SUBLANES, :]
        a, r, m_t, w_inter, e_negm, w_s, w_c = _gate_rows(slab, m0_ref[...], DEC_SEQ_LEN)
        cols_ref[...] = _gate_cols(r, w_inter, e_negm, w_s, w_c)
        rows_ref[0:SUBLANES, :] = a
        rows_ref[SUBLANES:2 * SUBLANES, :] = m_t


def _inproj_call(x, g_norm, w_all, b_if, m0_rows):
    n_tok = x.shape[0]
    const = lambda j: (0, 0)
    return pl.pallas_call(
        _inproj_kernel,
        grid=(PROJ_COLS // INPROJ_TILE,),
        in_specs=[
            pl.BlockSpec((n_tok, D_MODEL), const),
            pl.BlockSpec((1, D_MODEL), const),
            pl.BlockSpec((D_MODEL, INPROJ_TILE), lambda j: (0, j)),
            pl.BlockSpec((1, LANES), const),
            pl.BlockSpec((SUBLANES, n_tok), const),
        ],
        out_specs=[
            pl.BlockSpec((n_tok, INPROJ_TILE), lambda j: (0, j)),
            pl.BlockSpec((n_tok, LANES), const),
            pl.BlockSpec((2 * SUBLANES, n_tok), const),
        ],
        out_shape=[
            jax.ShapeDtypeStruct((n_tok, PROJ_COLS), _F32),
            jax.ShapeDtypeStruct((n_tok, LANES), _F32),
            jax.ShapeDtypeStruct((2 * SUBLANES, n_tok), _F32),
        ],
        scratch_shapes=[pltpu.VMEM((n_tok, D_MODEL), _BF16)],
        compiler_params=pltpu.CompilerParams(
            dimension_semantics=("arbitrary",), vmem_limit_bytes=VMEM_LIMIT),
        name="sample_inproj",
    )(x, g_norm, w_all, b_if, m0_rows)


def _sample_kernel(proj_ref, cols_ref, rows_ref, c0_ref, n0_ref, conv0_ref, mk_ref, mv_ref,
                   ghead_ref, convw_ref, convb_ref,
                   ymix_ref, c_ref, n_ref, conv_ref,
                   sv_ref, rs_ref, ext_ref):
    nt = SAMPLE_BLOCK_SEQS * DEC_SEQ_LEN
    j = pl.program_id(1)

    @pl.when(j == 0)
    def _():
        row = lax.broadcasted_iota(jnp.int32, (nt, nt), 0)
        col = lax.broadcasted_iota(jnp.int32, (nt, nt), 1)
        mask = (col <= row) & ((row // DEC_SEQ_LEN) == (col // DEC_SEQ_LEN))
        a = rows_ref[0:SUBLANES, :]
        for h in range(ML_HEADS):
            hs = h * ML_HD
            hc = h * ML_HEAD_COLS
            qb = proj_ref[:, hc + OFF_Q:hc + OFF_Q + ML_HD].astype(_BF16)
            kb = (proj_ref[:, hc + OFF_K:hc + OFF_K + ML_HD] * (ML_HD ** -0.5)).astype(_BF16)
            vb = proj_ref[:, hc + OFF_V:hc + OFF_V + ML_HD].astype(_BF16)
            r_col = cols_ref[:, GC_R + h:GC_R + h + 1]
            decay = jnp.exp(jnp.where(mask, a[h:h + 1, :] - r_col, -jnp.inf))
            s_mat = _dot_nt(qb, kb) * decay
            sv_ref[:, hs:hs + ML_HD] = _dot(s_mat.astype(_BF16), vb)
            rs_ref[:, h:h + 1] = jnp.sum(s_mat, axis=1, keepdims=True)

    xa_rows = XA_HEADS * DEC_SEQ_LEN
    own_head = (lax.broadcasted_iota(jnp.int32, (xa_rows, XA_KV_ROWS), 0) // DEC_SEQ_LEN
                == lax.broadcasted_iota(jnp.int32, (xa_rows, XA_KV_ROWS), 1) % XA_HEADS)

    seqs = range(SAMPLE_STEP_SEQS)
    pairs = [(sl, h) for sl in seqs for h in range(ML_HEADS)]
    rows_of = [pl.ds(pl.multiple_of((j * SAMPLE_STEP_SEQS + sl) * DEC_SEQ_LEN, DEC_SEQ_LEN), DEC_SEQ_LEN)
               for sl in seqs]

    def ml(sl, h, off):
        return proj_ref[rows_of[sl], h * ML_HEAD_COLS + off:h * ML_HEAD_COLS + off + ML_HD]

    scores = []
    for sl in seqs:
        q4 = jnp.concatenate(
            [proj_ref[rows_of[sl], COL_QX + h * XA_HD:COL_QX + (h + 1) * XA_HD] for h in range(XA_HEADS)], axis=0)
        s = _dot_nt(q4.astype(_BF16), mk_ref[sl].astype(_BF16)) * (XA_HD ** -0.5)
        scores.append(jnp.where(own_head, s, -jnp.inf))

    qc = {(sl, h): _dot(ml(sl, h, OFF_Q).astype(_BF16), c0_ref[sl, h].astype(_BF16)) for sl, h in pairs}

    y4 = []
    for sl in seqs:
        e = jnp.exp(scores[sl] - jnp.max(scores[sl], axis=1, keepdims=True))
        p = e * (1.0 / jnp.sum(e, axis=1, keepdims=True))
        y4.append(_dot(p.astype(_BF16), mv_ref[sl].astype(_BF16)))

    for sl, h in pairs:
        gc = cols_ref[rows_of[sl], :]
        wc = gc[0:1, GC_WC + h:GC_WC + h + 1]
        kw = ml(sl, h, OFF_K) * (ML_HD ** -0.5) * gc[:, GC_WS + h:GC_WS + h + 1]
        c_ref[sl, h] = wc * c0_ref[sl, h] + _dot_tn(kw.astype(_BF16), ml(sl, h, OFF_V).astype(_BF16))
        n_ref[sl, h:h + 1, :] = wc * n0_ref[sl, h:h + 1, :] + jnp.sum(kw, axis=0, keepdims=True)

    step_tokens = SAMPLE_STEP_SEQS * DEC_SEQ_LEN
    step_rows = pl.ds(pl.multiple_of(j * step_tokens, step_tokens), step_tokens)
    gc_step = cols_ref[step_rows, :]
    for h in range(ML_HEADS):
        hs = h * ML_HD
        hc = h * ML_HEAD_COLS
        wi_col = gc_step[:, GC_WI + h:GC_WI + h + 1]
        en_col = gc_step[:, GC_EN + h:GC_EN + h + 1]
        qc_step = jnp.concatenate([qc[sl, h] for sl in seqs], axis=0)
        n_step = jnp.concatenate(
            [jnp.broadcast_to(n0_ref[sl, h:h + 1, :], (DEC_SEQ_LEN, ML_HD)) for sl in seqs], axis=0)
        q_step = proj_ref[step_rows, hc + OFF_Q:hc + OFF_Q + ML_HD]
        num = wi_col * qc_step + sv_ref[step_rows, hs:hs + ML_HD]
        den = wi_col * jnp.sum(q_step * n_step, axis=1, keepdims=True) + rs_ref[step_rows, h:h + 1]
        hh = num * (1.0 / jnp.maximum(jnp.abs(den), en_col))
        ymix_ref[step_rows, hs:hs + ML_HD] = _ml_head_out(
            hh, proj_ref[step_rows, hc + OFF_O:hc + OFF_O + ML_HD],
            proj_ref[step_rows, COL_ZML + hs:COL_ZML + hs + ML_HD], ghead_ref[:, hs:hs + ML_HD])

    for sl in seqs:
        rows = rows_of[sl]
        u = proj_ref[rows, COL_CG:COL_CG + CONV_CH] * proj_ref[rows, COL_XC:COL_XC + CONV_CH]
        ext_ref[sl, SUBLANES - (CONV_K - 1):SUBLANES, :] = conv0_ref[sl]
        ext_ref[sl, SUBLANES:2 * SUBLANES, :] = u
        u1 = ext_ref[sl, SUBLANES - 1:2 * SUBLANES - 1, :]
        u2 = ext_ref[sl, SUBLANES - 2:2 * SUBLANES - 2, :]
        yc = convb_ref[...] + convw_ref[0:1, :] * u2 + convw_ref[1:2, :] * u1 + convw_ref[2:3, :] * u
        y_cv = (proj_ref[rows, COL_BG:COL_BG + CONV_CH] * yc
                * _silu(proj_ref[rows, COL_ZCV:COL_ZCV + CONV_CH]))
        ymix_ref[rows, ML_WIDTH:ML_WIDTH + CONV_CH] = y_cv
        conv_ref[sl] = u[DEC_SEQ_LEN - (CONV_K - 1):DEC_SEQ_LEN, :]

        for h in range(XA_HEADS):
            hs = h * XA_HD
            y_xa = (y4[sl][h * DEC_SEQ_LEN:(h + 1) * DEC_SEQ_LEN, :]
                    * _silu(proj_ref[rows, COL_ZX + hs:COL_ZX + hs + XA_HD]))
            c0 = ML_WIDTH + CONV_CH + hs
            ymix_ref[rows, c0:c0 + XA_HD] = y_xa


def _sample_call(proj, cols, rows, c0, n0, conv0, mk, mv, g_head, conv_w, conv_b):
    n_seq = c0.shape[0]
    nt = SAMPLE_BLOCK_SEQS * DEC_SEQ_LEN
    steps = SAMPLE_BLOCK_SEQS // SAMPLE_STEP_SEQS
    ss = SAMPLE_STEP_SEQS
    const2 = lambda i, j: (0, 0)
    blk2 = lambda i, j: (i, 0)
    seq3 = lambda i, j: (i * steps + j, 0, 0)
    seq4 = lambda i, j: (i * steps + j, 0, 0, 0)
    return pl.pallas_call(
        _sample_kernel,
        grid=(n_seq // SAMPLE_BLOCK_SEQS, steps),
        in_specs=[
            pl.BlockSpec((nt, PROJ_COLS), blk2),
            pl.BlockSpec((nt, LANES), blk2),
            pl.BlockSpec((2 * SUBLANES, nt), lambda i, j: (0, i)),
            pl.BlockSpec((ss, ML_HEADS, ML_HD, ML_HD), seq4),
            pl.BlockSpec((ss, ML_HEADS, ML_HD), seq3),
            pl.BlockSpec((ss, CONV_K - 1, CONV_CH), seq3),
            pl.BlockSpec((ss, XA_KV_ROWS, XA_HD), seq3),
            pl.BlockSpec((ss, XA_KV_ROWS, XA_HD), seq3),
            pl.BlockSpec((1, ML_WIDTH), const2),
            pl.BlockSpec((CONV_K, CONV_CH), const2),
            pl.BlockSpec((1, CONV_CH), const2),
        ],
        out_specs=[
            pl.BlockSpec((nt, MIX_WIDTH), blk2),
            pl.BlockSpec((ss, ML_HEADS, ML_HD, ML_HD), seq4),
            pl.BlockSpec((ss, ML_HEADS, ML_HD), seq3),
            pl.BlockSpec((ss, CONV_K - 1, CONV_CH), seq3),
        ],
        out_shape=[
            jax.ShapeDtypeStruct((n_seq * DEC_SEQ_LEN, MIX_WIDTH), _F32),
            jax.ShapeDtypeStruct(c0.shape, _F32),
            jax.ShapeDtypeStruct(n0.shape, _F32),
            jax.ShapeDtypeStruct(conv0.shape, _F32),
        ],
        scratch_shapes=[
            pltpu.VMEM((nt, ML_WIDTH), _F32),
            pltpu.VMEM((nt, LANES), _F32),
            pltpu.VMEM((ss, 2 * SUBLANES, CONV_CH), _F32),
        ],
        compiler_params=pltpu.CompilerParams(
            dimension_semantics=("arbitrary", "arbitrary"), vmem_limit_bytes=VMEM_LIMIT),
        name="sample_state",
    )(proj, cols, rows, c0, n0, conv0, mk, mv, g_head, conv_w, conv_b)


def _outproj_kernel(ymix_ref, x_ref, wout_ref, gfinal_ref, y_ref):
    y_ref[...] = _out_tail(ymix_ref[...].astype(_BF16), wout_ref[...], x_ref[...], gfinal_ref[...])


def _outproj_call(ymix, x, w_out, g_final):
    n_tok = x.shape[0]
    rb = 256
    const = lambda i: (0, 0)
    return pl.pallas_call(
        _outproj_kernel,
        grid=(n_tok // rb,),
        in_specs=[
            pl.BlockSpec((rb, MIX_WIDTH), lambda i: (i, 0)),
            pl.BlockSpec((rb, D_MODEL), lambda i: (i, 0)),
            pl.BlockSpec((MIX_WIDTH, D_MODEL), const),
            pl.BlockSpec((1, D_MODEL), const),
        ],
        out_specs=pl.BlockSpec((rb, D_MODEL), lambda i: (i, 0)),
        out_shape=jax.ShapeDtypeStruct((n_tok, D_MODEL), _F32),
        compiler_params=pltpu.CompilerParams(dimension_semantics=("arbitrary",)),
        name="sample_outproj",
    )(ymix, x, w_out, g_final)


def kernel(x_prompt, x_sample, state_mlstm_C, state_mlstm_n, state_mlstm_m, state_conv, cache_mem_k, cache_mem_v, mem_prompt, g_norm, w_in, b_if, g_head, conv_w, conv_b, g_mem, w_mem_kv, w_out, g_final):
    assert w_in.shape[0] == 1, "single-layer trunk"
    bp, tp, _ = x_prompt.shape
    bs, ts, _ = x_sample.shape
    assert ts == DEC_SEQ_LEN and tp % PROMPT_BLOCK == 0 and bs % SAMPLE_BLOCK_SEQS == 0

    w_all = _wprep_call(jnp.swapaxes(w_in[0], 0, 1))
    w_out_b = w_out[0].astype(_BF16)
    w_kv_b = w_mem_kv[0].astype(_BF16)
    b_if_p = jnp.pad(b_if[0][None, :], ((0, 0), (0, LANES - N_IF)))
    row = lambda v: v.reshape(1, -1)

    mk_p, mv_p, mk_rows, mv_rows = _memkv_call(mem_prompt, row(g_mem[0]), w_kv_b)
    y_p, c_p, n_p, m_p, conv_p = _prompt_call(
        x_prompt, mk_p, mv_p, row(g_norm[0]), w_all, b_if_p, row(g_head[0]), conv_w[0], row(conv_b[0]),
        w_out_b, row(g_final))

    n_tok = bs * ts
    xs = x_sample.reshape(n_tok, D_MODEL)
    m0_rows = jnp.pad(jnp.repeat(state_mlstm_m[0].T, ts, axis=1), ((0, SUBLANES - ML_HEADS), (0, 0)))
    proj, cols, rows = _inproj_call(xs, row(g_norm[0]), w_all, b_if_p, m0_rows)
    ymix, c_s, n_s, conv_s = _sample_call(
        proj, cols, rows, state_mlstm_C[0], state_mlstm_n[0], state_conv[0],
        cache_mem_k[0].reshape(bs, XA_KV_ROWS, XA_HD), cache_mem_v[0].reshape(bs, XA_KV_ROWS, XA_HD),
        row(g_head[0]), conv_w[0], row(conv_b[0]))
    y_s = _outproj_call(ymix, xs, w_out_b, row(g_final))
    m_s = rows[SUBLANES:SUBLANES + ML_HEADS, ts - 1::ts].T

    kv_shape = (1, bp, N_MEM, XA_HEADS, XA_HD)
    return (y_p, y_s.reshape(bs, ts, D_MODEL),
            c_p[None], n_p[None], m_p[:, :ML_HEADS, 0][None], conv_p[None],
            mk_rows.reshape(kv_shape), mv_rows.reshape(kv_shape),
            c_s[None], n_s[None], m_s[None], conv_s[None])
```

```python
import functools

import jax
import jax.numpy as jnp
from jax import lax
from jax.experimental import pallas as pl
from jax.experimental.pallas import tpu as pltpu

D_MODEL = 1024
ML_HEADS = 4
ML_HD = 256
ML_WIDTH = ML_HEADS * ML_HD
CONV_CH = 512
CONV_K = 3
XA_HEADS = 4
XA_HD = 128
XA_WIDTH = XA_HEADS * XA_HD
N_MEM = 256
XA_KV_ROWS = N_MEM * XA_HEADS
DEC_SEQ_LEN = 8
MIX_WIDTH = ML_WIDTH + CONV_CH + XA_WIDTH
EPS = 1e-6

SUBLANES = 8
LANES = 128

ML_HEAD_COLS = ML_HD
OFF_Q = 0
OFF_K = ML_WIDTH
OFF_V = 2 * ML_WIDTH
OFF_O = 3 * ML_WIDTH
COL_ZML = 4 * ML_WIDTH
COL_IF = 5 * ML_WIDTH
COL_BG = COL_IF + LANES
COL_CG = COL_BG + CONV_CH
COL_XC = COL_CG + CONV_CH
COL_ZCV = COL_XC + CONV_CH
COL_QX = COL_ZCV + CONV_CH
COL_ZX = COL_QX + XA_WIDTH
PROJ_COLS = COL_ZX + XA_WIDTH
N_IF = 2 * ML_HEADS

GC_R = 0 * SUBLANES
GC_WI = 1 * SUBLANES
GC_EN = 2 * SUBLANES
GC_WS = 3 * SUBLANES
GC_WC = 4 * SUBLANES
GC_USED = 5 * SUBLANES

PROMPT_BLOCK = 256
SAMPLE_BLOCK_SEQS = 16
SAMPLE_STEP_SEQS = 4
INPROJ_TILE = 640
VMEM_LIMIT = 60 * 1024 * 1024

_F32 = jnp.float32
_BF16 = jnp.bfloat16


def _dot(a, b):
    return jnp.dot(a, b, preferred_element_type=_F32)


def _dot_nt(a, b):
    return lax.dot_general(a, b, (((1,), (1,)), ((), ())), preferred_element_type=_F32)


def _dot_tn(a, b):
    return lax.dot_general(a, b, (((0,), (0,)), ((), ())), preferred_element_type=_F32)


def _rms(x, g):
    r = lax.rsqrt(jnp.mean(x * x, axis=-1, keepdims=True) + EPS)
    return (x * r) * g


def _silu(x):
    return x * jax.nn.sigmoid(x)


def _log_sigmoid(x):
    return jnp.minimum(x, 0.0) - jnp.log1p(jnp.exp(-jnp.abs(x)))


def _gate_rows(slab, m_prev, seg):
    n = slab.shape[1]
    pos = lax.broadcasted_iota(jnp.int32, slab.shape, 1) & (seg - 1)

    def scan(x, op, fill, reverse=False):
        k = 1
        while k < seg:
            if reverse:
                shifted, ok = pltpu.roll(x, n - k, axis=1), pos < seg - k
            else:
                shifted, ok = pltpu.roll(x, k, axis=1), pos >= k
            x = op(x, jnp.where(ok, shifted, fill))
            k *= 2
        return x

    logf = _log_sigmoid(pltpu.roll(slab, ML_HEADS, axis=0))
    f_cum = scan(logf, jnp.add, 0.0)
    a = slab - f_cum
    r = jnp.maximum(scan(a, jnp.maximum, -jnp.inf), m_prev)
    r_last = scan(r, jnp.maximum, -jnp.inf, reverse=True)
    m_t = f_cum + r
    w_inter = jnp.exp(m_prev - r)
    e_negm = jnp.exp(-m_t)
    w_s = jnp.exp(a - r_last)
    w_c = jnp.exp(m_prev - r_last)
    return a, r, m_t, w_inter, e_negm, w_s, w_c


def _gate_cols(r, w_inter, e_negm, w_s, w_c):
    n = r.shape[1]
    pad = jnp.zeros((LANES - GC_USED, n), _F32)
    return jnp.concatenate([r, w_inter, e_negm, w_s, w_c, pad], axis=0).T


def _ml_head_out(hh, o_pre, z, g_head):
    hh = jax.nn.sigmoid(o_pre) * hh
    return _rms(hh, g_head) * _silu(z)


def _out_tail(ymix_bf16, w_out, x, g_final):
    y = _dot(ymix_bf16, w_out) + x
    return _rms(y, g_final)


WPREP_PAD = LANES - N_IF


def _wprep_kernel(wt_ref, out_ref):
    j = pl.program_id(0)
    gate_tile = COL_IF // INPROJ_TILE

    @pl.when(j != gate_tile)
    def _():
        out_ref[...] = wt_ref[...].T.astype(_BF16)

    @pl.when(j == gate_tile)
    def _():
        rows = wt_ref[...]
        padded = jnp.concatenate(
            [rows[0:N_IF], jnp.zeros((WPREP_PAD, D_MODEL), _F32), rows[N_IF:INPROJ_TILE - WPREP_PAD]], axis=0)
        out_ref[...] = padded.T.astype(_BF16)


def _wprep_call(w_t):
    assert COL_IF % INPROJ_TILE == 0 and PROJ_COLS % INPROJ_TILE == 0 and N_IF % SUBLANES == 0
    gate_tile = COL_IF // INPROJ_TILE
    first_row = lambda j: (j * (INPROJ_TILE // SUBLANES)
                           - (j > gate_tile).astype(jnp.int32) * (WPREP_PAD // SUBLANES)) * SUBLANES
    return pl.pallas_call(
        _wprep_kernel,
        grid=(PROJ_COLS // INPROJ_TILE,),
        in_specs=[pl.BlockSpec((pl.Element(INPROJ_TILE), pl.Element(D_MODEL)), lambda j: (first_row(j), 0))],
        out_specs=pl.BlockSpec((D_MODEL, INPROJ_TILE), lambda j: (0, j)),
        out_shape=jax.ShapeDtypeStruct((D_MODEL, PROJ_COLS), _BF16),
        compiler_params=pltpu.CompilerParams(dimension_semantics=("arbitrary",)),
        name="weight_repack",
    )(w_t)


def _memkv_kernel(mem_ref, g_ref, w_ref, mk_ref, mv_ref, mk_rows_ref, mv_rows_ref):
    xn = _rms(mem_ref[0], g_ref[...]).astype(_BF16)
    kv = _dot(xn, w_ref[...])
    mk_ref[0] = kv[:, :XA_WIDTH]
    mv_ref[0] = kv[:, XA_WIDTH:]
    for h in range(XA_HEADS):
        head_rows = pl.ds(h, N_MEM, stride=XA_HEADS)
        mk_rows_ref[0, head_rows, :] = kv[:, h * XA_HD:(h + 1) * XA_HD]
        mv_rows_ref[0, head_rows, :] = kv[:, XA_WIDTH + h * XA_HD:XA_WIDTH + (h + 1) * XA_HD]


def _memkv_call(mem, g_mem, w_kv):
    b = mem.shape[0]
    const = lambda i: (0, 0)
    per_batch = lambda i: (i, 0, 0)
    return pl.pallas_call(
        _memkv_kernel,
        grid=(b,),
        in_specs=[
            pl.BlockSpec((1, N_MEM, D_MODEL), per_batch),
            pl.BlockSpec((1, D_MODEL), const),
            pl.BlockSpec((D_MODEL, 2 * XA_WIDTH), const),
        ],
        out_specs=[
            pl.BlockSpec((1, N_MEM, XA_WIDTH), per_batch),
            pl.BlockSpec((1, N_MEM, XA_WIDTH), per_batch),
            pl.BlockSpec((1, XA_KV_ROWS, XA_HD), per_batch),
            pl.BlockSpec((1, XA_KV_ROWS, XA_HD), per_batch),
        ],
        out_shape=[jax.ShapeDtypeStruct((b, N_MEM, XA_WIDTH), _F32)] * 2
        + [jax.ShapeDtypeStruct((b, XA_KV_ROWS, XA_HD), _F32)] * 2,
        compiler_params=pltpu.CompilerParams(dimension_semantics=("arbitrary",)),
        name="memkv",
    )(mem, g_mem, w_kv)


N_FILLERS = 8


def _finish_block(proj, filler, x, mk_ref, mv_ref, bif_ref, ghead_ref, convw_ref, convb_ref, wout_ref, gfinal_ref,
                  y_ref, c_ref, n_ref, m_ref, conv_ref, ext_ref, acc_ref):
    tb = PROMPT_BLOCK

    def out_part(y_bf16, row0):
        return _dot(y_bf16, wout_ref[row0:row0 + y_bf16.shape[1], :])

    gates = proj(COL_IF, LANES)
    slab = (gates + bif_ref[...]).T[0:SUBLANES, :]
    m_prev8 = m_ref[0]
    m_prev = jnp.concatenate([m_prev8] * (tb // LANES), axis=1)
    a, r, m_t, w_inter, e_negm, w_s, w_c = _gate_rows(slab, m_prev, tb)
    cols = _gate_cols(r, w_inter, e_negm, w_s, w_c)
    m_ref[0] = jnp.broadcast_to(m_t[:, tb - 1:tb], (SUBLANES, LANES))

    row = lax.broadcasted_iota(jnp.int32, (tb, tb), 0)
    col = lax.broadcasted_iota(jnp.int32, (tb, tb), 1)
    causal = col <= row

    def head_first(h):
        q = proj(h * ML_HEAD_COLS + OFF_Q, ML_HD)
        k = proj(h * ML_HEAD_COLS + OFF_K, ML_HD) * (ML_HD ** -0.5)
        qb = q.astype(_BF16)
        c_old = c_ref[0, h]
        qk = _dot_nt(qb, k.astype(_BF16))
        qc = _dot(qb, c_old.astype(_BF16))
        return q, k, c_old, qk, qc

    def head_second(h, first):
        q, k, c_old, qk, qc = first
        vb = proj(h * ML_HEAD_COLS + OFF_V, ML_HD).astype(_BF16)
        r_col = cols[:, GC_R + h:GC_R + h + 1]
        wi_col = cols[:, GC_WI + h:GC_WI + h + 1]
        en_col = cols[:, GC_EN + h:GC_EN + h + 1]
        ws_col = cols[:, GC_WS + h:GC_WS + h + 1]
        wc = w_c[h:h + 1, 0:1]
        s_mat = qk * jnp.exp(jnp.where(causal, a[h:h + 1, :] - r_col, -jnp.inf))
        kw = k * ws_col
        sv = _dot(s_mat.astype(_BF16), vb)
        c_ref[0, h] = wc * c_old + _dot_tn(kw.astype(_BF16), vb)
        n_old = n_ref[0, h:h + 1, :]
        n_ref[0, h:h + 1, :] = wc * n_old + jnp.sum(kw, axis=0, keepdims=True)
        num = wi_col * qc + sv
        den = wi_col * jnp.sum(q * n_old, axis=1, keepdims=True) + jnp.sum(s_mat, axis=1, keepdims=True)
        hh = num * (1.0 / jnp.maximum(jnp.abs(den), en_col))
        return jax.nn.sigmoid(proj(h * ML_HEAD_COLS + OFF_O, ML_HD)) * hh

    def head_out(h, gated, z_all):
        hs = h * ML_HD
        y_ml = _rms(gated, ghead_ref[:, hs:hs + ML_HD]) * _silu(z_all[:, hs:hs + ML_HD])
        return out_part(y_ml.astype(_BF16), hs)

    def xa_scores(h):
        hs = h * XA_HD
        kh = mk_ref[0, :, hs:hs + XA_HD].astype(_BF16)
        return _dot_nt(xa_in[:, hs:hs + XA_HD].astype(_BF16), kh) * (XA_HD ** -0.5)

    def xa_out(h, s):
        hs = h * XA_HD
        vh = mv_ref[0, :, hs:hs + XA_HD].astype(_BF16)
        e = jnp.exp(s - jnp.max(s, axis=1, keepdims=True))
        p = e * (1.0 / jnp.sum(e, axis=1, keepdims=True))
        y_xa = _dot(p.astype(_BF16), vh) * _silu(xa_in[:, XA_WIDTH + hs:XA_WIDTH + hs + XA_HD])
        return y_xa.astype(_BF16)

    xa_in = proj(COL_QX, 2 * XA_WIDTH)
    conv_in = proj(COL_BG, 4 * CONV_CH)
    z_all = proj(COL_ZML, ML_WIDTH)

    s0, s1, s2, s3 = (xa_scores(h) for h in range(XA_HEADS))
    filler()
    first0 = head_first(0)
    first1 = head_first(1)
    y_xa0 = xa_out(0, s0)
    y_xa1 = xa_out(1, s1)
    filler()
    first2 = head_first(2)
    first3 = head_first(3)
    y_xa2 = xa_out(2, s2)
    y_xa3 = xa_out(3, s3)
    filler()

    b_g = conv_in[:, 0:CONV_CH]
    u = conv_in[:, CONV_CH:2 * CONV_CH] * conv_in[:, 2 * CONV_CH:3 * CONV_CH]
    ext_ref[SUBLANES:SUBLANES + tb, :] = u
    u1 = ext_ref[SUBLANES - 1:SUBLANES - 1 + tb, :]
    u2 = ext_ref[SUBLANES - 2:SUBLANES - 2 + tb, :]
    yc = convb_ref[...] + convw_ref[0:1, :] * u2 + convw_ref[1:2, :] * u1 + convw_ref[2:3, :] * u
    y_cv = b_g * yc * _silu(conv_in[:, 3 * CONV_CH:4 * CONV_CH])
    tail = u[tb - (CONV_K - 1):tb, :]
    ext_ref[SUBLANES - (CONV_K - 1):SUBLANES, :] = tail
    conv_ref[0] = tail
    gated0 = head_second(0, first0)
    filler()
    gated1 = head_second(1, first1)
    acc_ref[...] = out_part(jnp.concatenate([y_cv.astype(_BF16), y_xa0, y_xa1, y_xa2, y_xa3], axis=1), ML_WIDTH)
    filler()
    gated2 = head_second(2, first2)
    acc = head_out(0, gated0, z_all)
    filler()
    gated3 = head_second(3, first3)
    acc = acc + head_out(1, gated1, z_all)
    filler()
    acc = acc + head_out(2, gated2, z_all)
    acc = acc + head_out(3, gated3, z_all)
    filler()

    y_ref[0] = _rms(acc_ref[...] + acc + x, gfinal_ref[...])


def _prompt_kernel(xnext_ref, x_ref, mk_ref, mv_ref, gnorm_ref, w_ref, bif_ref, ghead_ref, convw_ref, convb_ref,
                   wout_ref, gfinal_ref,
                   y_ref, c_ref, n_ref, m_ref, conv_ref,
                   proj_even_ref, proj_odd_ref, ext_ref, acc_ref, *, blocks_per_seq):
    s = pl.program_id(0)

    @pl.when(s == 0)
    def _():
        proj_odd_ref[...] = jnp.zeros_like(proj_odd_ref)

    @pl.when((s == 0) | (lax.rem(s + blocks_per_seq - 1, blocks_per_seq) == 0))
    def _():
        c_ref[...] = jnp.zeros_like(c_ref)
        n_ref[...] = jnp.zeros_like(n_ref)
        m_ref[...] = jnp.zeros_like(m_ref)
        ext_ref[0:SUBLANES, :] = jnp.zeros((SUBLANES, CONV_CH), _F32)

    def step(proj_w_ref, proj_r_ref):
        xb_next = _rms(xnext_ref[0], gnorm_ref[...]).astype(_BF16)
        chunk = PROJ_COLS // LANES // N_FILLERS * LANES
        starts = iter(range(0, chunk * N_FILLERS, chunk))

        def project_chunk():
            c0 = next(starts)
            c1 = PROJ_COLS if c0 == chunk * (N_FILLERS - 1) else c0 + chunk
            proj_w_ref[:, c0:c1] = _dot(xb_next, w_ref[:, c0:c1])

        _finish_block(lambda col, width: proj_r_ref[:, col:col + width], project_chunk, x_ref[0],
                      mk_ref, mv_ref, bif_ref, ghead_ref, convw_ref, convb_ref, wout_ref, gfinal_ref,
                      y_ref, c_ref, n_ref, m_ref, conv_ref, ext_ref, acc_ref)

    @pl.when(lax.rem(s, 2) == 0)
    def _():
        step(proj_even_ref, proj_odd_ref)

    @pl.when(lax.rem(s, 2) == 1)
    def _():
        step(proj_odd_ref, proj_even_ref)


def _prompt_call(x, mk, mv, g_norm, w_all, b_if, g_head, conv_w, conv_b, w_out, g_final):
    b, t, _ = x.shape
    tb = PROMPT_BLOCK
    nt = t // tb
    n_blocks = b * nt
    const2 = lambda s: (0, 0)
    resident = functools.partial(pl.BlockSpec, index_map=const2, pipeline_mode=pl.Buffered(1))
    nxt = lambda s: jnp.minimum(s, n_blocks - 1)
    cur = lambda s: jnp.maximum(s - 1, 0)
    cur_batch3 = lambda s: (cur(s) // nt, 0, 0)
    return pl.pallas_call(
        functools.partial(_prompt_kernel, blocks_per_seq=nt),
        grid=(n_blocks + 1,),
        in_specs=[
            pl.BlockSpec((1, tb, D_MODEL), lambda s: (nxt(s) // nt, nxt(s) % nt, 0)),
            pl.BlockSpec((1, tb, D_MODEL), lambda s: (cur(s) // nt, cur(s) % nt, 0)),
            pl.BlockSpec((1, N_MEM, XA_WIDTH), cur_batch3),
            pl.BlockSpec((1, N_MEM, XA_WIDTH), cur_batch3),
            pl.BlockSpec((1, D_MODEL), const2),
            resident((D_MODEL, PROJ_COLS)),
            pl.BlockSpec((1, LANES), const2),
            pl.BlockSpec((1, ML_WIDTH), const2),
            pl.BlockSpec((CONV_K, CONV_CH), const2),
            pl.BlockSpec((1, CONV_CH), const2),
            resident((MIX_WIDTH, D_MODEL)),
            pl.BlockSpec((1, D_MODEL), const2),
        ],
        out_specs=[
            pl.BlockSpec((1, tb, D_MODEL), lambda s: (cur(s) // nt, cur(s) % nt, 0)),
            pl.BlockSpec((1, ML_HEADS, ML_HD, ML_HD), lambda s: (cur(s) // nt, 0, 0, 0)),
            pl.BlockSpec((1, ML_HEADS, ML_HD), cur_batch3),
            pl.BlockSpec((1, SUBLANES, LANES), cur_batch3),
            pl.BlockSpec((1, CONV_K - 1, CONV_CH), cur_batch3),
        ],
        out_shape=[
            jax.ShapeDtypeStruct((b, t, D_MODEL), _F32),
            jax.ShapeDtypeStruct((b, ML_HEADS, ML_HD, ML_HD), _F32),
            jax.ShapeDtypeStruct((b, ML_HEADS, ML_HD), _F32),
            jax.ShapeDtypeStruct((b, SUBLANES, LANES), _F32),
            jax.ShapeDtypeStruct((b, CONV_K - 1, CONV_CH), _F32),
        ],
        scratch_shapes=[
            pltpu.VMEM((tb, PROJ_COLS), _F32),
            pltpu.VMEM((tb, PROJ_COLS), _F32),
            pltpu.VMEM((SUBLANES + tb, CONV_CH), _F32),
            pltpu.VMEM((tb, D_MODEL), _F32),
        ],
        compiler_params=pltpu.CompilerParams(
            dimension_semantics=("arbitrary",), vmem_limit_bytes=VMEM_LIMIT),
        name="prompt_layer",
    )(x, x, mk, mv, g_norm, w_all, b_if, g_head, conv_w, conv_b, w_out, g_final)


def _sample_kernel(x_ref, gnorm_ref, w_ref, bif_ref, m0_ref, c0_ref, n0_ref, conv0_ref, mk_ref, mv_ref,
                   ghead_ref, convw_ref, convb_ref, wout_ref, gfinal_ref,
                   y_ref, c_ref, n_ref, conv_ref, mrows_ref,
                   proj_ref, cols_ref, sv_ref, rs_ref, ext_ref, ymix_ref):
    nt = SAMPLE_BLOCK_SEQS * DEC_SEQ_LEN
    j = pl.program_id(1)

    @pl.when(j == 0)
    def _():
        proj_ref[...] = _dot(_rms(x_ref[...], gnorm_ref[...]).astype(_BF16), w_ref[...])
        slab = (proj_ref[:, COL_IF:COL_IF + LANES] + bif_ref[...]).T[0:SUBLANES, :]
        a, r, m_t, w_inter, e_negm, w_s, w_c = _gate_rows(slab, m0_ref[...], DEC_SEQ_LEN)
        cols_ref[...] = _gate_cols(r, w_inter, e_negm, w_s, w_c)
        mrows_ref[...] = m_t
        row = lax.broadcasted_iota(jnp.int32, (nt, nt), 0)
        col = lax.broadcasted_iota(jnp.int32, (nt, nt), 1)
        mask = (col <= row) & ((row // DEC_SEQ_LEN) == (col // DEC_SEQ_LEN))
        for h in range(ML_HEADS):
            hs = h * ML_HD
            hc = h * ML_HEAD_COLS
            qb = proj_ref[:, hc + OFF_Q:hc + OFF_Q + ML_HD].astype(_BF16)
            kb = (proj_ref[:, hc + OFF_K:hc + OFF_K + ML_HD] * (ML_HD ** -0.5)).astype(_BF16)
            vb = proj_ref[:, hc + OFF_V:hc + OFF_V + ML_HD].astype(_BF16)
            r_col = cols_ref[:, GC_R + h:GC_R + h + 1]
            decay = jnp.exp(jnp.where(mask, a[h:h + 1, :] - r_col, -jnp.inf))
            s_mat = _dot_nt(qb, kb) * decay
            sv_ref[:, hs:hs + ML_HD] = _dot(s_mat.astype(_BF16), vb)
            rs_ref[:, h:h + 1] = jnp.sum(s_mat, axis=1, keepdims=True)

    xa_rows = XA_HEADS * DEC_SEQ_LEN
    own_head = (lax.broadcasted_iota(jnp.int32, (xa_rows, XA_KV_ROWS), 0) // DEC_SEQ_LEN
                == lax.broadcasted_iota(jnp.int32, (xa_rows, XA_KV_ROWS), 1) % XA_HEADS)

    seqs = range(SAMPLE_STEP_SEQS)
    pairs = [(sl, h) for sl in seqs for h in range(ML_HEADS)]
    rows_of = [pl.ds(pl.multiple_of((j * SAMPLE_STEP_SEQS + sl) * DEC_SEQ_LEN, DEC_SEQ_LEN), DEC_SEQ_LEN)
               for sl in seqs]

    def ml(sl, h, off):
        return proj_ref[rows_of[sl], h * ML_HEAD_COLS + off:h * ML_HEAD_COLS + off + ML_HD]

    scores = []
    for sl in seqs:
        q4 = jnp.concatenate(
            [proj_ref[rows_of[sl], COL_QX + h * XA_HD:COL_QX + (h + 1) * XA_HD] for h in range(XA_HEADS)], axis=0)
        s = _dot_nt(q4.astype(_BF16), mk_ref[sl].astype(_BF16)) * (XA_HD ** -0.5)
        scores.append(jnp.where(own_head, s, -jnp.inf))

    qc = {(sl, h): _dot(ml(sl, h, OFF_Q).astype(_BF16), c0_ref[sl, h].astype(_BF16)) for sl, h in pairs}

    y4 = []
    for sl in seqs:
        e = jnp.exp(scores[sl] - jnp.max(scores[sl], axis=1, keepdims=True))
        p = e * (1.0 / jnp.sum(e, axis=1, keepdims=True))
        y4.append(_dot(p.astype(_BF16), mv_ref[sl].astype(_BF16)))

    for sl, h in pairs:
        gc = cols_ref[rows_of[sl], :]
        wc = gc[0:1, GC_WC + h:GC_WC + h + 1]
        kw = ml(sl, h, OFF_K) * (ML_HD ** -0.5) * gc[:, GC_WS + h:GC_WS + h + 1]
        c_ref[sl, h] = wc * c0_ref[sl, h] + _dot_tn(kw.astype(_BF16), ml(sl, h, OFF_V).astype(_BF16))
        n_ref[sl, h:h + 1, :] = wc * n0_ref[sl, h:h + 1, :] + jnp.sum(kw, axis=0, keepdims=True)

    step_tokens = SAMPLE_STEP_SEQS * DEC_SEQ_LEN
    step_rows = pl.ds(pl.multiple_of(j * step_tokens, step_tokens), step_tokens)
    gc_step = cols_ref[step_rows, :]
    for h in range(ML_HEADS):
        hs = h * ML_HD
        hc = h * ML_HEAD_COLS
        wi_col = gc_step[:, GC_WI + h:GC_WI + h + 1]
        en_col = gc_step[:, GC_EN + h:GC_EN + h + 1]
        qc_step = jnp.concatenate([qc[sl, h] for sl in seqs], axis=0)
        n_step = jnp.concatenate(
            [jnp.broadcast_to(n0_ref[sl, h:h + 1, :], (DEC_SEQ_LEN, ML_HD)) for sl in seqs], axis=0)
        q_step = proj_ref[step_rows, hc + OFF_Q:hc + OFF_Q + ML_HD]
        num = wi_col * qc_step + sv_ref[step_rows, hs:hs + ML_HD]
        den = wi_col * jnp.sum(q_step * n_step, axis=1, keepdims=True) + rs_ref[step_rows, h:h + 1]
        hh = num * (1.0 / jnp.maximum(jnp.abs(den), en_col))
        ymix_ref[step_rows, hs:hs + ML_HD] = _ml_head_out(
            hh, proj_ref[step_rows, hc + OFF_O:hc + OFF_O + ML_HD],
            proj_ref[step_rows, COL_ZML + hs:COL_ZML + hs + ML_HD], ghead_ref[:, hs:hs + ML_HD]).astype(_BF16)

    y_cv_parts, y_xa_parts = [], []
    for sl in seqs:
        rows = rows_of[sl]
        u = proj_ref[rows, COL_CG:COL_CG + CONV_CH] * proj_ref[rows, COL_XC:COL_XC + CONV_CH]
        ext_ref[sl, SUBLANES - (CONV_K - 1):SUBLANES, :] = conv0_ref[sl]
        ext_ref[sl, SUBLANES:2 * SUBLANES, :] = u
        u1 = ext_ref[sl, SUBLANES - 1:2 * SUBLANES - 1, :]
        u2 = ext_ref[sl, SUBLANES - 2:2 * SUBLANES - 2, :]
        yc = convb_ref[...] + convw_ref[0:1, :] * u2 + convw_ref[1:2, :] * u1 + convw_ref[2:3, :] * u
        y_cv = (proj_ref[rows, COL_BG:COL_BG + CONV_CH] * yc
                * _silu(proj_ref[rows, COL_ZCV:COL_ZCV + CONV_CH]))
        y_cv_parts.append(y_cv)
        conv_ref[sl] = u[DEC_SEQ_LEN - (CONV_K - 1):DEC_SEQ_LEN, :]
        y_xa_parts.append(jnp.concatenate(
            [y4[sl][h * DEC_SEQ_LEN:(h + 1) * DEC_SEQ_LEN, :] for h in range(XA_HEADS)], axis=1)
            * _silu(proj_ref[rows, COL_ZX:COL_ZX + XA_WIDTH]))
    ymix_ref[step_rows, ML_WIDTH:ML_WIDTH + CONV_CH] = jnp.concatenate(y_cv_parts, axis=0).astype(_BF16)
    ymix_ref[step_rows, ML_WIDTH + CONV_CH:MIX_WIDTH] = jnp.concatenate(y_xa_parts, axis=0).astype(_BF16)

    @pl.when(j == SAMPLE_BLOCK_SEQS // SAMPLE_STEP_SEQS - 1)
    def _():
        y_ref[...] = _out_tail(ymix_ref[...], wout_ref[...], x_ref[...], gfinal_ref[...])


def _sample_call(x, g_norm, w_all, b_if, m0_rows, c0, n0, conv0, mk, mv, g_head, conv_w, conv_b, w_out, g_final):
    n_seq = c0.shape[0]
    nt = SAMPLE_BLOCK_SEQS * DEC_SEQ_LEN
    steps = SAMPLE_BLOCK_SEQS // SAMPLE_STEP_SEQS
    ss = SAMPLE_STEP_SEQS
    const2 = lambda i, j: (0, 0)
    resident = functools.partial(pl.BlockSpec, index_map=const2, pipeline_mode=pl.Buffered(1))
    blk2 = lambda i, j: (i, 0)
    seq3 = lambda i, j: (i * steps + j, 0, 0)
    seq4 = lambda i, j: (i * steps + j, 0, 0, 0)
    return pl.pallas_call(
        _sample_kernel,
        grid=(n_seq // SAMPLE_BLOCK_SEQS, steps),
        in_specs=[
            pl.BlockSpec((nt, D_MODEL), blk2),
            pl.BlockSpec((1, D_MODEL), const2),
            resident((D_MODEL, PROJ_COLS)),
            pl.BlockSpec((1, LANES), const2),
            pl.BlockSpec((SUBLANES, nt), lambda i, j: (0, i)),
            pl.BlockSpec((ss, ML_HEADS, ML_HD, ML_HD), seq4),
            pl.BlockSpec((ss, ML_HEADS, ML_HD), seq3),
            pl.BlockSpec((ss, CONV_K - 1, CONV_CH), seq3),
            pl.BlockSpec((ss, XA_KV_ROWS, XA_HD), seq3),
            pl.BlockSpec((ss, XA_KV_ROWS, XA_HD), seq3),
            pl.BlockSpec((1, ML_WIDTH), const2),
            pl.BlockSpec((CONV_K, CONV_CH), const2),
            pl.BlockSpec((1, CONV_CH), const2),
            resident((MIX_WIDTH, D_MODEL)),
            pl.BlockSpec((1, D_MODEL), const2),
        ],
        out_specs=[
            pl.BlockSpec((nt, D_MODEL), blk2),
            pl.BlockSpec((ss, ML_HEADS, ML_HD, ML_HD), seq4),
            pl.BlockSpec((ss, ML_HEADS, ML_HD), seq3),
            pl.BlockSpec((ss, CONV_K - 1, CONV_CH), seq3),
            pl.BlockSpec((SUBLANES, nt), lambda i, j: (0, i)),
        ],
        out_shape=[
            jax.ShapeDtypeStruct((n_seq * DEC_SEQ_LEN, D_MODEL), _F32),
            jax.ShapeDtypeStruct(c0.shape, _F32),
            jax.ShapeDtypeStruct(n0.shape, _F32),
            jax.ShapeDtypeStruct(conv0.shape, _F32),
            jax.ShapeDtypeStruct((SUBLANES, n_seq * DEC_SEQ_LEN), _F32),
        ],
        scratch_shapes=[
            pltpu.VMEM((nt, PROJ_COLS), _F32),
            pltpu.VMEM((nt, LANES), _F32),
            pltpu.VMEM((nt, ML_WIDTH), _F32),
            pltpu.VMEM((nt, LANES), _F32),
            pltpu.VMEM((ss, 2 * SUBLANES, CONV_CH), _F32),
            pltpu.VMEM((nt, MIX_WIDTH), _BF16),
        ],
        compiler_params=pltpu.CompilerParams(
            dimension_semantics=("arbitrary", "arbitrary"), vmem_limit_bytes=VMEM_LIMIT),
        name="sample_layer",
    )(x, g_norm, w_all, b_if, m0_rows, c0, n0, conv0, mk, mv, g_head, conv_w, conv_b, w_out, g_final)


def kernel(x_prompt, x_sample, state_mlstm_C, state_mlstm_n, state_mlstm_m, state_conv, cache_mem_k, cache_mem_v, mem_prompt, g_norm, w_in, b_if, g_head, conv_w, conv_b, g_mem, w_mem_kv, w_out, g_final):
    assert w_in.shape[0] == 1, "single-layer trunk"
    bp, tp, _ = x_prompt.shape
    bs, ts, _ = x_sample.shape
    assert ts == DEC_SEQ_LEN and tp % PROMPT_BLOCK == 0 and bs % SAMPLE_BLOCK_SEQS == 0

    w_all = _wprep_call(jnp.swapaxes(w_in[0], 0, 1))
    w_out_b = w_out[0].astype(_BF16)
    w_kv_b = w_mem_kv[0].astype(_BF16)
    b_if_p = jnp.pad(b_if[0][None, :], ((0, 0), (0, LANES - N_IF)))
    row = lambda v: v.reshape(1, -1)

    mk_p, mv_p, mk_rows, mv_rows = _memkv_call(mem_prompt, row(g_mem[0]), w_kv_b)
    y_p, c_p, n_p, m_p, conv_p = _prompt_call(
        x_prompt, mk_p, mv_p, row(g_norm[0]), w_all, b_if_p, row(g_head[0]), conv_w[0], row(conv_b[0]),
        w_out_b, row(g_final))

    n_tok = bs * ts
    xs = x_sample.reshape(n_tok, D_MODEL)
    m0_rows = jnp.pad(jnp.repeat(state_mlstm_m[0].T, ts, axis=1), ((0, SUBLANES - ML_HEADS), (0, 0)))
    y_s, c_s, n_s, conv_s, m_rows = _sample_call(
        xs, row(g_norm[0]), w_all, b_if_p, m0_rows, state_mlstm_C[0], state_mlstm_n[0], state_conv[0],
        cache_mem_k[0].reshape(bs, XA_KV_ROWS, XA_HD), cache_mem_v[0].reshape(bs, XA_KV_ROWS, XA_HD),
        row(g_head[0]), conv_w[0], row(conv_b[0]), w_out_b, row(g_final))
    m_s = m_rows[:ML_HEADS, ts - 1::ts].T

    kv_shape = (1, bp, N_MEM, XA_HEADS, XA_HD)
    return (y_p, y_s.reshape(bs, ts, D_MODEL),
            c_p[None], n_p[None], m_p[:, :ML_HEADS, 0][None], conv_p[None],
            mk_rows.reshape(kv_shape), mv_rows.reshape(kv_shape),
            c_s[None], n_s[None], m_s[None], conv_s[None])
```

```python
import functools

import jax
import jax.numpy as jnp
from jax import lax
from jax.experimental import pallas as pl
from jax.experimental.pallas import tpu as pltpu

D_MODEL = 1024
ML_HEADS = 4
ML_HD = 256
ML_WIDTH = ML_HEADS * ML_HD
CONV_CH = 512
CONV_K = 3
XA_HEADS = 4
XA_HD = 128
XA_WIDTH = XA_HEADS * XA_HD
N_MEM = 256
XA_KV_ROWS = N_MEM * XA_HEADS
DEC_SEQ_LEN = 8
MIX_WIDTH = ML_WIDTH + CONV_CH + XA_WIDTH
EPS = 1e-6

SUBLANES = 8
LANES = 128

ML_HEAD_COLS = ML_HD
OFF_Q = 0
OFF_K = ML_WIDTH
OFF_V = 2 * ML_WIDTH
OFF_O = 3 * ML_WIDTH
COL_ZML = 4 * ML_WIDTH
COL_IF = 5 * ML_WIDTH
COL_BG = COL_IF + LANES
COL_CG = COL_BG + CONV_CH
COL_XC = COL_CG + CONV_CH
COL_ZCV = COL_XC + CONV_CH
COL_QX = COL_ZCV + CONV_CH
COL_ZX = COL_QX + XA_WIDTH
PROJ_COLS = COL_ZX + XA_WIDTH
N_IF = 2 * ML_HEADS

GC_R = 0 * SUBLANES
GC_WI = 1 * SUBLANES
GC_EN = 2 * SUBLANES
GC_WS = 3 * SUBLANES
GC_WC = 4 * SUBLANES
GC_USED = 5 * SUBLANES

PROMPT_BLOCK = 256
SAMPLE_BLOCK_SEQS = 16
SAMPLE_STEP_SEQS = 4
INPROJ_TILE = 640
VMEM_LIMIT = 60 * 1024 * 1024

_F32 = jnp.float32
_BF16 = jnp.bfloat16


def _dot(a, b):
    return jnp.dot(a, b, preferred_element_type=_F32)


def _dot_nt(a, b):
    return lax.dot_general(a, b, (((1,), (1,)), ((), ())), preferred_element_type=_F32)


def _dot_tn(a, b):
    return lax.dot_general(a, b, (((0,), (0,)), ((), ())), preferred_element_type=_F32)


def _rms(x, g):
    r = lax.rsqrt(jnp.mean(x * x, axis=-1, keepdims=True) + EPS)
    return (x * r) * g


def _silu(x):
    return x * jax.nn.sigmoid(x)


def _log_sigmoid(x):
    return jnp.minimum(x, 0.0) - jnp.log1p(jnp.exp(-jnp.abs(x)))


def _gate_rows(slab, m_prev, seg):
    n = slab.shape[1]
    pos = lax.broadcasted_iota(jnp.int32, slab.shape, 1) & (seg - 1)

    def scan(x, op, fill, reverse=False):
        k = 1
        while k < seg:
            if reverse:
                shifted, ok = pltpu.roll(x, n - k, axis=1), pos < seg - k
            else:
                shifted, ok = pltpu.roll(x, k, axis=1), pos >= k
            x = op(x, jnp.where(ok, shifted, fill))
            k *= 2
        return x

    logf = _log_sigmoid(pltpu.roll(slab, ML_HEADS, axis=0))
    f_cum = scan(logf, jnp.add, 0.0)
    a = slab - f_cum
    r = jnp.maximum(scan(a, jnp.maximum, -jnp.inf), m_prev)
    r_last = scan(r, jnp.maximum, -jnp.inf, reverse=True)
    m_t = f_cum + r
    w_inter = jnp.exp(m_prev - r)
    e_negm = jnp.exp(-m_t)
    w_s = jnp.exp(a - r_last)
    w_c = jnp.exp(m_prev - r_last)
    return a, r, m_t, w_inter, e_negm, w_s, w_c


def _gate_cols(r, w_inter, e_negm, w_s, w_c):
    n = r.shape[1]
    pad = jnp.zeros((LANES - GC_USED, n), _F32)
    return jnp.concatenate([r, w_inter, e_negm, w_s, w_c, pad], axis=0).T


def _ml_head_out(hh, o_pre, z, g_head):
    hh = jax.nn.sigmoid(o_pre) * hh
    return _rms(hh, g_head) * _silu(z)


def _out_tail(ymix_bf16, w_out, x, g_final):
    y = _dot(ymix_bf16, w_out) + x
    return _rms(y, g_final)


WPREP_PAD = LANES - N_IF


def _wprep_kernel(wt_ref, out_ref):
    j = pl.program_id(0)
    gate_tile = COL_IF // INPROJ_TILE

    @pl.when(j != gate_tile)
    def _():
        out_ref[...] = wt_ref[...].T.astype(_BF16)

    @pl.when(j == gate_tile)
    def _():
        rows = wt_ref[...]
        padded = jnp.concatenate(
            [rows[0:N_IF], jnp.zeros((WPREP_PAD, D_MODEL), _F32), rows[N_IF:INPROJ_TILE - WPREP_PAD]], axis=0)
        out_ref[...] = padded.T.astype(_BF16)


def _wprep_call(w_t):
    assert COL_IF % INPROJ_TILE == 0 and PROJ_COLS % INPROJ_TILE == 0 and N_IF % SUBLANES == 0
    gate_tile = COL_IF // INPROJ_TILE
    first_row = lambda j: (j * (INPROJ_TILE // SUBLANES)
                           - (j > gate_tile).astype(jnp.int32) * (WPREP_PAD // SUBLANES)) * SUBLANES
    return pl.pallas_call(
        _wprep_kernel,
        grid=(PROJ_COLS // INPROJ_TILE,),
        in_specs=[pl.BlockSpec((pl.Element(INPROJ_TILE), pl.Element(D_MODEL)), lambda j: (first_row(j), 0))],
        out_specs=pl.BlockSpec((D_MODEL, INPROJ_TILE), lambda j: (0, j)),
        out_shape=jax.ShapeDtypeStruct((D_MODEL, PROJ_COLS), _BF16),
        compiler_params=pltpu.CompilerParams(dimension_semantics=("arbitrary",)),
        name="weight_repack",
    )(w_t)


def _memkv_kernel(mem_ref, g_ref, w_ref, mk_ref, mv_ref, mk_rows_ref, mv_rows_ref):
    xn = _rms(mem_ref[0], g_ref[...]).astype(_BF16)
    kv = _dot(xn, w_ref[...])
    mk_ref[0] = kv[:, :XA_WIDTH]
    mv_ref[0] = kv[:, XA_WIDTH:]
    for h in range(XA_HEADS):
        head_rows = pl.ds(h, N_MEM, stride=XA_HEADS)
        mk_rows_ref[0, head_rows, :] = kv[:, h * XA_HD:(h + 1) * XA_HD]
        mv_rows_ref[0, head_rows, :] = kv[:, XA_WIDTH + h * XA_HD:XA_WIDTH + (h + 1) * XA_HD]


def _memkv_call(mem, g_mem, w_kv):
    b = mem.shape[0]
    const = lambda i: (0, 0)
    per_batch = lambda i: (i, 0, 0)
    return pl.pallas_call(
        _memkv_kernel,
        grid=(b,),
        in_specs=[
            pl.BlockSpec((1, N_MEM, D_MODEL), per_batch),
            pl.BlockSpec((1, D_MODEL), const),
            pl.BlockSpec((D_MODEL, 2 * XA_WIDTH), const),
        ],
        out_specs=[
            pl.BlockSpec((1, N_MEM, XA_WIDTH), per_batch),
            pl.BlockSpec((1, N_MEM, XA_WIDTH), per_batch),
            pl.BlockSpec((1, XA_KV_ROWS, XA_HD), per_batch),
            pl.BlockSpec((1, XA_KV_ROWS, XA_HD), per_batch),
        ],
        out_shape=[jax.ShapeDtypeStruct((b, N_MEM, XA_WIDTH), _F32)] * 2
        + [jax.ShapeDtypeStruct((b, XA_KV_ROWS, XA_HD), _F32)] * 2,
        compiler_params=pltpu.CompilerParams(dimension_semantics=("arbitrary",)),
        name="memkv",
    )(mem, g_mem, w_kv)


def _finish_block(proj, refill, x, mk_ref, mv_ref, bif_ref, ghead_ref, convw_ref, convb_ref, wout_ref, gfinal_ref,
                  y_ref, c_ref, n_ref, m_ref, conv_ref, ext_ref, acc_ref):
    tb = PROMPT_BLOCK

    def out_part(y_bf16, row0):
        return _dot(y_bf16, wout_ref[row0:row0 + y_bf16.shape[1], :])

    gates = proj(COL_IF, LANES)
    slab = (gates + bif_ref[...]).T[0:SUBLANES, :]
    m_prev8 = m_ref[0]
    m_prev = jnp.concatenate([m_prev8] * (tb // LANES), axis=1)
    a, r, m_t, w_inter, e_negm, w_s, w_c = _gate_rows(slab, m_prev, tb)
    cols = _gate_cols(r, w_inter, e_negm, w_s, w_c)
    m_ref[0] = jnp.broadcast_to(m_t[:, tb - 1:tb], (SUBLANES, LANES))

    row = lax.broadcasted_iota(jnp.int32, (tb, tb), 0)
    col = lax.broadcasted_iota(jnp.int32, (tb, tb), 1)
    causal = col <= row

    def ml_bf16(off, scale=None):
        out = []
        for h in range(ML_HEADS):
            v = proj(h * ML_HEAD_COLS + off, ML_HD)
            out.append((v if scale is None else v * scale).astype(_BF16))
        return out

    q_bf16 = ml_bf16(OFF_Q)
    k_bf16 = ml_bf16(OFF_K, ML_HD ** -0.5)
    v_bf16 = ml_bf16(OFF_V)

    def head_first(h):
        qb = q_bf16[h]
        c_old = c_ref[0, h]
        qk = _dot_nt(qb, k_bf16[h])
        qc = _dot(qb, c_old.astype(_BF16))
        return qb.astype(_F32), k_bf16[h].astype(_F32), c_old, qk, qc

    def head_second(h, first):
        q, k, c_old, qk, qc = first
        vb = v_bf16[h]
        r_col = cols[:, GC_R + h:GC_R + h + 1]
        wi_col = cols[:, GC_WI + h:GC_WI + h + 1]
        en_col = cols[:, GC_EN + h:GC_EN + h + 1]
        ws_col = cols[:, GC_WS + h:GC_WS + h + 1]
        wc = w_c[h:h + 1, 0:1]
        s_mat = qk * jnp.exp(jnp.where(causal, a[h:h + 1, :] - r_col, -jnp.inf))
        kw = k * ws_col
        sv = _dot(s_mat.astype(_BF16), vb)
        c_ref[0, h] = wc * c_old + _dot_tn(kw.astype(_BF16), vb)
        n_old = n_ref[0, h:h + 1, :]
        n_ref[0, h:h + 1, :] = wc * n_old + jnp.sum(kw, axis=0, keepdims=True)
        num = wi_col * qc + sv
        den = wi_col * jnp.sum(q * n_old, axis=1, keepdims=True) + jnp.sum(s_mat, axis=1, keepdims=True)
        hh = num * (1.0 / jnp.maximum(jnp.abs(den), en_col))
        return jax.nn.sigmoid(proj(h * ML_HEAD_COLS + OFF_O, ML_HD)) * hh

    def head_out(h, gated):
        hs = h * ML_HD
        y_ml = _rms(gated, ghead_ref[:, hs:hs + ML_HD]) * _silu(proj(COL_ZML + hs, ML_HD))
        return out_part(y_ml.astype(_BF16), hs)

    def xa_scores(h):
        hs = h * XA_HD
        kh = mk_ref[0, :, hs:hs + XA_HD].astype(_BF16)
        return _dot_nt(proj(COL_QX + hs, XA_HD).astype(_BF16), kh) * (XA_HD ** -0.5)

    def xa_out(h, s):
        hs = h * XA_HD
        vh = mv_ref[0, :, hs:hs + XA_HD].astype(_BF16)
        e = jnp.exp(s - jnp.max(s, axis=1, keepdims=True))
        p = e * (1.0 / jnp.sum(e, axis=1, keepdims=True))
        y_xa = _dot(p.astype(_BF16), vh) * _silu(proj(COL_ZX + hs, XA_HD))
        return y_xa.astype(_BF16)

    s0, s1, s2, s3 = (xa_scores(h) for h in range(XA_HEADS))
    refill(0, ML_WIDTH)
    first0 = head_first(0)
    first1 = head_first(1)
    y_xa0 = xa_out(0, s0)
    y_xa1 = xa_out(1, s1)
    refill(OFF_K, ML_WIDTH)
    first2 = head_first(2)
    first3 = head_first(3)
    y_xa2 = xa_out(2, s2)
    y_xa3 = xa_out(3, s3)
    refill(OFF_V, ML_WIDTH)

    u = proj(COL_CG, CONV_CH) * proj(COL_XC, CONV_CH)
    ext_ref[SUBLANES:SUBLANES + tb, :] = u
    u1 = ext_ref[SUBLANES - 1:SUBLANES - 1 + tb, :]
    u2 = ext_ref[SUBLANES - 2:SUBLANES - 2 + tb, :]
    yc = convb_ref[...] + convw_ref[0:1, :] * u2 + convw_ref[1:2, :] * u1 + convw_ref[2:3, :] * u
    y_cv = proj(COL_BG, CONV_CH) * yc * _silu(proj(COL_ZCV, CONV_CH))
    tail = u[tb - (CONV_K - 1):tb, :]
    ext_ref[SUBLANES - (CONV_K - 1):SUBLANES, :] = tail
    conv_ref[0] = tail
    gated0 = head_second(0, first0)
    refill(COL_QX, 2 * XA_WIDTH)
    gated1 = head_second(1, first1)
    acc_ref[...] = out_part(jnp.concatenate([y_cv.astype(_BF16), y_xa0, y_xa1, y_xa2, y_xa3], axis=1), ML_WIDTH)
    refill(COL_IF, COL_XC - COL_IF)
    gated2 = head_second(2, first2)
    acc = head_out(0, gated0)
    refill(COL_XC, COL_QX - COL_XC)
    gated3 = head_second(3, first3)
    acc = acc + head_out(1, gated1)
    refill(OFF_O, ML_WIDTH)
    acc = acc + head_out(2, gated2)
    acc = acc + head_out(3, gated3)
    refill(COL_ZML, ML_WIDTH)

    y_ref[0] = _rms(acc_ref[...] + acc + x, gfinal_ref[...])


def _prompt_kernel(xnext_ref, x_ref, mk_ref, mv_ref, gnorm_ref, w_ref, bif_ref, ghead_ref, convw_ref, convb_ref,
                   wout_ref, gfinal_ref,
                   y_ref, c_ref, n_ref, m_ref, conv_ref,
                   proj_ref, ext_ref, acc_ref, *, blocks_per_seq):
    s = pl.program_id(0)

    @pl.when(s == 0)
    def _():
        proj_ref[...] = jnp.zeros_like(proj_ref)

    @pl.when((s == 0) | (lax.rem(s + blocks_per_seq - 1, blocks_per_seq) == 0))
    def _():
        c_ref[...] = jnp.zeros_like(c_ref)
        n_ref[...] = jnp.zeros_like(n_ref)
        m_ref[...] = jnp.zeros_like(m_ref)
        ext_ref[0:SUBLANES, :] = jnp.zeros((SUBLANES, CONV_CH), _F32)

    xb_next = _rms(xnext_ref[0], gnorm_ref[...]).astype(_BF16)

    def refill(col, width):
        proj_ref[:, col:col + width] = _dot(xb_next, w_ref[:, col:col + width])

    _finish_block(lambda col, width: proj_ref[:, col:col + width], refill, x_ref[0],
                  mk_ref, mv_ref, bif_ref, ghead_ref, convw_ref, convb_ref, wout_ref, gfinal_ref,
                  y_ref, c_ref, n_ref, m_ref, conv_ref, ext_ref, acc_ref)


def _prompt_call(x, mk, mv, g_norm, w_all, b_if, g_head, conv_w, conv_b, w_out, g_final):
    b, t, _ = x.shape
    tb = PROMPT_BLOCK
    nt = t // tb
    n_blocks = b * nt
    const2 = lambda s: (0, 0)
    resident = functools.partial(pl.BlockSpec, index_map=const2, pipeline_mode=pl.Buffered(1))
    nxt = lambda s: jnp.minimum(s, n_blocks - 1)
    cur = lambda s: jnp.maximum(s - 1, 0)
    cur_batch3 = lambda s: (cur(s) // nt, 0, 0)
    return pl.pallas_call(
        functools.partial(_prompt_kernel, blocks_per_seq=nt),
        grid=(n_blocks + 1,),
        in_specs=[
            pl.BlockSpec((1, tb, D_MODEL), lambda s: (nxt(s) // nt, nxt(s) % nt, 0)),
            pl.BlockSpec((1, tb, D_MODEL), lambda s: (cur(s) // nt, cur(s) % nt, 0)),
            pl.BlockSpec((1, N_MEM, XA_WIDTH), cur_batch3),
            pl.BlockSpec((1, N_MEM, XA_WIDTH), cur_batch3),
            pl.BlockSpec((1, D_MODEL), const2),
            resident((D_MODEL, PROJ_COLS)),
            pl.BlockSpec((1, LANES), const2),
            pl.BlockSpec((1, ML_WIDTH), const2),
            pl.BlockSpec((CONV_K, CONV_CH), const2),
            pl.BlockSpec((1, CONV_CH), const2),
            resident((MIX_WIDTH, D_MODEL)),
            pl.BlockSpec((1, D_MODEL), const2),
        ],
        out_specs=[
            pl.BlockSpec((1, tb, D_MODEL), lambda s: (cur(s) // nt, cur(s) % nt, 0)),
            pl.BlockSpec((1, ML_HEADS, ML_HD, ML_HD), lambda s: (cur(s) // nt, 0, 0, 0)),
            pl.BlockSpec((1, ML_HEADS, ML_HD), cur_batch3),
            pl.BlockSpec((1, SUBLANES, LANES), cur_batch3),
            pl.BlockSpec((1, CONV_K - 1, CONV_CH), cur_batch3),
        ],
        out_shape=[
            jax.ShapeDtypeStruct((b, t, D_MODEL), _F32),
            jax.ShapeDtypeStruct((b, ML_HEADS, ML_HD, ML_HD), _F32),
            jax.ShapeDtypeStruct((b, ML_HEADS, ML_HD), _F32),
            jax.ShapeDtypeStruct((b, SUBLANES, LANES), _F32),
            jax.ShapeDtypeStruct((b, CONV_K - 1, CONV_CH), _F32),
        ],
        scratch_shapes=[
            pltpu.VMEM((tb, PROJ_COLS), _F32),
            pltpu.VMEM((SUBLANES + tb, CONV_CH), _F32),
            pltpu.VMEM((tb, D_MODEL), _F32),
        ],
        compiler_params=pltpu.CompilerParams(
            dimension_semantics=("arbitrary",), vmem_limit_bytes=VMEM_LIMIT),
        name="prompt_layer",
    )(x, x, mk, mv, g_norm, w_all, b_if, g_head, conv_w, conv_b, w_out, g_final)


def _sample_kernel(x_ref, gnorm_ref, w_ref, bif_ref, m0_ref, c0_ref, n0_ref, conv0_ref, mk_ref, mv_ref,
                   ghead_ref, convw_ref, convb_ref, wout_ref, gfinal_ref,
                   y_ref, c_ref, n_ref, conv_ref, mrows_ref,
                   proj_ref, cols_ref, sv_ref, rs_ref, ext_ref, ymix_ref):
    nt = SAMPLE_BLOCK_SEQS * DEC_SEQ_LEN
    j = pl.program_id(1)

    @pl.when(j == 0)
    def _():
        proj_ref[...] = _dot(_rms(x_ref[...], gnorm_ref[...]).astype(_BF16), w_ref[...])
        slab = (proj_ref[:, COL_IF:COL_IF + LANES] + bif_ref[...]).T[0:SUBLANES, :]
        a, r, m_t, w_inter, e_negm, w_s, w_c = _gate_rows(slab, m0_ref[...], DEC_SEQ_LEN)
        cols_ref[...] = _gate_cols(r, w_inter, e_negm, w_s, w_c)
        mrows_ref[...] = m_t
        row = lax.broadcasted_iota(jnp.int32, (nt, nt), 0)
        col = lax.broadcasted_iota(jnp.int32, (nt, nt), 1)
        mask = (col <= row) & ((row // DEC_SEQ_LEN) == (col // DEC_SEQ_LEN))
        for h in range(ML_HEADS):
            hs = h * ML_HD
            hc = h * ML_HEAD_COLS
            qb = proj_ref[:, hc + OFF_Q:hc + OFF_Q + ML_HD].astype(_BF16)
            kb = (proj_ref[:, hc + OFF_K:hc + OFF_K + ML_HD] * (ML_HD ** -0.5)).astype(_BF16)
            vb = proj_ref[:, hc + OFF_V:hc + OFF_V + ML_HD].astype(_BF16)
            r_col = cols_ref[:, GC_R + h:GC_R + h + 1]
            decay = jnp.exp(jnp.where(mask, a[h:h + 1, :] - r_col, -jnp.inf))
            s_mat = _dot_nt(qb, kb) * decay
            sv_ref[:, hs:hs + ML_HD] = _dot(s_mat.astype(_BF16), vb)
            rs_ref[:, h:h + 1] = jnp.sum(s_mat, axis=1, keepdims=True)

    xa_rows = XA_HEADS * DEC_SEQ_LEN
    own_head = (lax.broadcasted_iota(jnp.int32, (xa_rows, XA_KV_ROWS), 0) // DEC_SEQ_LEN
                == lax.broadcasted_iota(jnp.int32, (xa_rows, XA_KV_ROWS), 1) % XA_HEADS)

    seqs = range(SAMPLE_STEP_SEQS)
    pairs = [(sl, h) for sl in seqs for h in range(ML_HEADS)]
    rows_of = [pl.ds(pl.multiple_of((j * SAMPLE_STEP_SEQS + sl) * DEC_SEQ_LEN, DEC_SEQ_LEN), DEC_SEQ_LEN)
               for sl in seqs]

    def ml(sl, h, off):
        return proj_ref[rows_of[sl], h * ML_HEAD_COLS + off:h * ML_HEAD_COLS + off + ML_HD]

    scores = []
    for sl in seqs:
        q4 = jnp.concatenate(
            [proj_ref[rows_of[sl], COL_QX + h * XA_HD:COL_QX + (h + 1) * XA_HD] for h in range(XA_HEADS)], axis=0)
        s = _dot_nt(q4.astype(_BF16), mk_ref[sl].astype(_BF16)) * (XA_HD ** -0.5)
        scores.append(jnp.where(own_head, s, -jnp.inf))

    qc = {(sl, h): _dot(ml(sl, h, OFF_Q).astype(_BF16), c0_ref[sl, h].astype(_BF16)) for sl, h in pairs}

    y4 = []
    for sl in seqs:
        e = jnp.exp(scores[sl] - jnp.max(scores[sl], axis=1, keepdims=True))
        p = e * (1.0 / jnp.sum(e, axis=1, keepdims=True))
        y4.append(_dot(p.astype(_BF16), mv_ref[sl].astype(_BF16)))

    for sl, h in pairs:
        gc = cols_ref[rows_of[sl], :]
        wc = gc[0:1, GC_WC + h:GC_WC + h + 1]
        kw = ml(sl, h, OFF_K) * (ML_HD ** -0.5) * gc[:, GC_WS + h:GC_WS + h + 1]
        c_ref[sl, h] = wc * c0_ref[sl, h] + _dot_tn(kw.astype(_BF16), ml(sl, h, OFF_V).astype(_BF16))
        n_ref[sl, h:h + 1, :] = wc * n0_ref[sl, h:h + 1, :] + jnp.sum(kw, axis=0, keepdims=True)

    step_tokens = SAMPLE_STEP_SEQS * DEC_SEQ_LEN
    step_rows = pl.ds(pl.multiple_of(j * step_tokens, step_tokens), step_tokens)
    gc_step = cols_ref[step_rows, :]
    for h in range(ML_HEADS):
        hs = h * ML_HD
        hc = h * ML_HEAD_COLS
        wi_col = gc_step[:, GC_WI + h:GC_WI + h + 1]
        en_col = gc_step[:, GC_EN + h:GC_EN + h + 1]
        qc_step = jnp.concatenate([qc[sl, h] for sl in seqs], axis=0)
        n_step = jnp.concatenate(
            [jnp.broadcast_to(n0_ref[sl, h:h + 1, :], (DEC_SEQ_LEN, ML_HD)) for sl in seqs], axis=0)
        q_step = proj_ref[step_rows, hc + OFF_Q:hc + OFF_Q + ML_HD]
        num = wi_col * qc_step + sv_ref[step_rows, hs:hs + ML_HD]
        den = wi_col * jnp.sum(q_step * n_step, axis=1, keepdims=True) + rs_ref[step_rows, h:h + 1]
        hh = num * (1.0 / jnp.maximum(jnp.abs(den), en_col))
        ymix_ref[step_rows, hs:hs + ML_HD] = _ml_head_out(
            hh, proj_ref[step_rows, hc + OFF_O:hc + OFF_O + ML_HD],
            proj_ref[step_rows, COL_ZML + hs:COL_ZML + hs + ML_HD], ghead_ref[:, hs:hs + ML_HD]).astype(_BF16)

    y_cv_parts, y_xa_parts = [], []
    for sl in seqs:
        rows = rows_of[sl]
        u = proj_ref[rows, COL_CG:COL_CG + CONV_CH] * proj_ref[rows, COL_XC:COL_XC + CONV_CH]
        ext_ref[sl, SUBLANES - (CONV_K - 1):SUBLANES, :] = conv0_ref[sl]
        ext_ref[sl, SUBLANES:2 * SUBLANES, :] = u
        u1 = ext_ref[sl, SUBLANES - 1:2 * SUBLANES - 1, :]
        u2 = ext_ref[sl, SUBLANES - 2:2 * SUBLANES - 2, :]
        yc = convb_ref[...] + convw_ref[0:1, :] * u2 + convw_ref[1:2, :] * u1 + convw_ref[2:3, :] * u
        y_cv = (proj_ref[rows, COL_BG:COL_BG + CONV_CH] * yc
                * _silu(proj_ref[rows, COL_ZCV:COL_ZCV + CONV_CH]))
        y_cv_parts.append(y_cv)
        conv_ref[sl] = u[DEC_SEQ_LEN - (CONV_K - 1):DEC_SEQ_LEN, :]
        y_xa_parts.append(jnp.concatenate(
            [y4[sl][h * DEC_SEQ_LEN:(h + 1) * DEC_SEQ_LEN, :] for h in range(XA_HEADS)], axis=1)
            * _silu(proj_ref[rows, COL_ZX:COL_ZX + XA_WIDTH]))
    ymix_ref[step_rows, ML_WIDTH:ML_WIDTH + CONV_CH] = jnp.concatenate(y_cv_parts, axis=0).astype(_BF16)
    ymix_ref[step_rows, ML_WIDTH + CONV_CH:MIX_WIDTH] = jnp.concatenate(y_xa_parts, axis=0).astype(_BF16)

    @pl.when(j == SAMPLE_BLOCK_SEQS // SAMPLE_STEP_SEQS - 1)
    def _():
        y_ref[...] = _out_tail(ymix_ref[...], wout_ref[...], x_ref[...], gfinal_ref[...])


def _sample_call(x, g_norm, w_all, b_if, m0_rows, c0, n0, conv0, mk, mv, g_head, conv_w, conv_b, w_out, g_final):
    n_seq = c0.shape[0]
    nt = SAMPLE_BLOCK_SEQS * DEC_SEQ_LEN
    steps = SAMPLE_BLOCK_SEQS // SAMPLE_STEP_SEQS
    ss = SAMPLE_STEP_SEQS
    const2 = lambda i, j: (0, 0)
    resident = functools.partial(pl.BlockSpec, index_map=const2, pipeline_mode=pl.Buffered(1))
    blk2 = lambda i, j: (i, 0)
    seq3 = lambda i, j: (i * steps + j, 0, 0)
    seq4 = lambda i, j: (i * steps + j, 0, 0, 0)
    return pl.pallas_call(
        _sample_kernel,
        grid=(n_seq // SAMPLE_BLOCK_SEQS, steps),
        in_specs=[
            pl.BlockSpec((nt, D_MODEL), blk2),
            pl.BlockSpec((1, D_MODEL), const2),
            resident((D_MODEL, PROJ_COLS)),
            pl.BlockSpec((1, LANES), const2),
            pl.BlockSpec((SUBLANES, nt), lambda i, j: (0, i)),
            pl.BlockSpec((ss, ML_HEADS, ML_HD, ML_HD), seq4),
            pl.BlockSpec((ss, ML_HEADS, ML_HD), seq3),
            pl.BlockSpec((ss, CONV_K - 1, CONV_CH), seq3),
            pl.BlockSpec((ss, XA_KV_ROWS, XA_HD), seq3),
            pl.BlockSpec((ss, XA_KV_ROWS, XA_HD), seq3),
            pl.BlockSpec((1, ML_WIDTH), const2),
            pl.BlockSpec((CONV_K, CONV_CH), const2),
            pl.BlockSpec((1, CONV_CH), const2),
            resident((MIX_WIDTH, D_MODEL)),
            pl.BlockSpec((1, D_MODEL), const2),
        ],
        out_specs=[
            pl.BlockSpec((nt, D_MODEL), blk2),
            pl.BlockSpec((ss, ML_HEADS, ML_HD, ML_HD), seq4),
            pl.BlockSpec((ss, ML_HEADS, ML_HD), seq3),
            pl.BlockSpec((ss, CONV_K - 1, CONV_CH), seq3),
            pl.BlockSpec((SUBLANES, nt), lambda i, j: (0, i)),
        ],
        out_shape=[
            jax.ShapeDtypeStruct((n_seq * DEC_SEQ_LEN, D_MODEL), _F32),
            jax.ShapeDtypeStruct(c0.shape, _F32),
            jax.ShapeDtypeStruct(n0.shape, _F32),
            jax.ShapeDtypeStruct(conv0.shape, _F32),
            jax.ShapeDtypeStruct((SUBLANES, n_seq * DEC_SEQ_LEN), _F32),
        ],
        scratch_shapes=[
            pltpu.VMEM((nt, PROJ_COLS), _F32),
            pltpu.VMEM((nt, LANES), _F32),
            pltpu.VMEM((nt, ML_WIDTH), _F32),
            pltpu.VMEM((nt, LANES), _F32),
            pltpu.VMEM((ss, 2 * SUBLANES, CONV_CH), _F32),
            pltpu.VMEM((nt, MIX_WIDTH), _BF16),
        ],
        compiler_params=pltpu.CompilerParams(
            dimension_semantics=("arbitrary", "arbitrary"), vmem_limit_bytes=VMEM_LIMIT),
        name="sample_layer",
    )(x, g_norm, w_all, b_if, m0_rows, c0, n0, conv0, mk, mv, g_head, conv_w, conv_b, w_out, g_final)


def kernel(x_prompt, x_sample, state_mlstm_C, state_mlstm_n, state_mlstm_m, state_conv, cache_mem_k, cache_mem_v, mem_prompt, g_norm, w_in, b_if, g_head, conv_w, conv_b, g_mem, w_mem_kv, w_out, g_final):
    assert w_in.shape[0] == 1, "single-layer trunk"
    bp, tp, _ = x_prompt.shape
    bs, ts, _ = x_sample.shape
    assert ts == DEC_SEQ_LEN and tp % PROMPT_BLOCK == 0 and bs % SAMPLE_BLOCK_SEQS == 0

    w_all = _wprep_call(jnp.swapaxes(w_in[0], 0, 1))
    w_out_b = w_out[0].astype(_BF16)
    w_kv_b = w_mem_kv[0].astype(_BF16)
    b_if_p = jnp.pad(b_if[0][None, :], ((0, 0), (0, LANES - N_IF)))
    row = lambda v: v.reshape(1, -1)

    mk_p, mv_p, mk_rows, mv_rows = _memkv_call(mem_prompt, row(g_mem[0]), w_kv_b)
    y_p, c_p, n_p, m_p, conv_p = _prompt_call(
        x_prompt, mk_p, mv_p, row(g_norm[0]), w_all, b_if_p, row(g_head[0]), conv_w[0], row(conv_b[0]),
        w_out_b, row(g_final))

    n_tok = bs * ts
    xs = x_sample.reshape(n_tok, D_MODEL)
    m0_rows = jnp.pad(jnp.repeat(state_mlstm_m[0].T, ts, axis=1), ((0, SUBLANES - ML_HEADS), (0, 0)))
    y_s, c_s, n_s, conv_s, m_rows = _sample_call(
        xs, row(g_norm[0]), w_all, b_if_p, m0_rows, state_mlstm_C[0], state_mlstm_n[0], state_conv[0],
        cache_mem_k[0].reshape(bs, XA_KV_ROWS, XA_HD), cache_mem_v[0].reshape(bs, XA_KV_ROWS, XA_HD),
        row(g_head[0]), conv_w[0], row(conv_b[0]), w_out_b, row(g_final))
    m_s = m_rows[:ML_HEADS, ts - 1::ts].T

    kv_shape = (1, bp, N_MEM, XA_HEADS, XA_HD)
    return (y_p, y_s.reshape(bs, ts, D_MODEL),
            c_p[None], n_p[None], m_p[:, :ML_HEADS, 0][None], conv_p[None],
            mk_rows.reshape(kv_shape), mv_rows.reshape(kv_shape),
            c_s[None], n_s[None], m_s[None], conv_s[None])
```

```python
import functools

import jax
import jax.numpy as jnp
from jax import lax
from jax.experimental import pallas as pl
from jax.experimental.pallas import tpu as pltpu

D_MODEL = 1024
ML_HEADS = 4
ML_HD = 256
ML_WIDTH = ML_HEADS * ML_HD
CONV_CH = 512
CONV_K = 3
XA_HEADS = 4
XA_HD = 128
XA_WIDTH = XA_HEADS * XA_HD
N_MEM = 256
XA_KV_ROWS = N_MEM * XA_HEADS
DEC_SEQ_LEN = 8
MIX_WIDTH = ML_WIDTH + CONV_CH + XA_WIDTH
EPS = 1e-6

SUBLANES = 8
LANES = 128

ML_HEAD_COLS = ML_HD
OFF_Q = 0
OFF_K = ML_WIDTH
OFF_V = 2 * ML_WIDTH
OFF_O = 3 * ML_WIDTH
COL_ZML = 4 * ML_WIDTH
COL_IF = 5 * ML_WIDTH
COL_BG = COL_IF + LANES
COL_CG = COL_BG + CONV_CH
COL_XC = COL_CG + CONV_CH
COL_ZCV = COL_XC + CONV_CH
COL_QX = COL_ZCV + CONV_CH
COL_ZX = COL_QX + XA_WIDTH
PROJ_COLS = COL_ZX + XA_WIDTH
N_IF = 2 * ML_HEADS

GC_R = 0 * SUBLANES
GC_WI = 1 * SUBLANES
GC_EN = 2 * SUBLANES
GC_WS = 3 * SUBLANES
GC_WC = 4 * SUBLANES
GC_USED = 5 * SUBLANES

PROMPT_BLOCK = 256
SAMPLE_BLOCK_SEQS = 16
SAMPLE_STEP_SEQS = 4
INPROJ_TILE = 640
VMEM_LIMIT = 63 * 1024 * 1024

_F32 = jnp.float32
_BF16 = jnp.bfloat16


def _dot(a, b):
    return jnp.dot(a, b, preferred_element_type=_F32)


def _dot_nt(a, b):
    return lax.dot_general(a, b, (((1,), (1,)), ((), ())), preferred_element_type=_F32)


def _dot_tn(a, b):
    return lax.dot_general(a, b, (((0,), (0,)), ((), ())), preferred_element_type=_F32)


def _rms(x, g):
    r = lax.rsqrt(jnp.mean(x * x, axis=-1, keepdims=True) + EPS)
    return (x * r) * g


def _silu(x):
    return x * jax.nn.sigmoid(x)


def _log_sigmoid(x):
    return jnp.minimum(x, 0.0) - jnp.log1p(jnp.exp(-jnp.abs(x)))


def _gate_rows(slab, m_prev, seg):
    n = slab.shape[1]
    pos = lax.broadcasted_iota(jnp.int32, slab.shape, 1) & (seg - 1)

    def scan(x, op, fill, reverse=False):
        k = 1
        while k < seg:
            if reverse:
                shifted, ok = pltpu.roll(x, n - k, axis=1), pos < seg - k
            else:
                shifted, ok = pltpu.roll(x, k, axis=1), pos >= k
            x = op(x, jnp.where(ok, shifted, fill))
            k *= 2
        return x

    logf = _log_sigmoid(pltpu.roll(slab, ML_HEADS, axis=0))
    f_cum = scan(logf, jnp.add, 0.0)
    a = slab - f_cum
    r = jnp.maximum(scan(a, jnp.maximum, -jnp.inf), m_prev)
    r_last = scan(r, jnp.maximum, -jnp.inf, reverse=True)
    m_t = f_cum + r
    w_inter = jnp.exp(m_prev - r)
    e_negm = jnp.exp(-m_t)
    w_s = jnp.exp(a - r_last)
    w_c = jnp.exp(m_prev - r_last)
    return a, r, m_t, w_inter, e_negm, w_s, w_c


def _gate_cols(r, w_inter, e_negm, w_s, w_c):
    n = r.shape[1]
    pad = jnp.zeros((LANES - GC_USED, n), _F32)
    return jnp.concatenate([r, w_inter, e_negm, w_s, w_c, pad], axis=0).T


def _ml_head_out(hh, o_pre, z, g_head):
    hh = jax.nn.sigmoid(o_pre) * hh
    return _rms(hh, g_head) * _silu(z)


def _out_tail(ymix_bf16, w_out, x, g_final):
    y = _dot(ymix_bf16, w_out) + x
    return _rms(y, g_final)


WPREP_PAD = LANES - N_IF


def _wprep_kernel(wt_ref, out_ref):
    j = pl.program_id(0)
    gate_tile = COL_IF // INPROJ_TILE

    @pl.when(j != gate_tile)
    def _():
        out_ref[...] = wt_ref[...].T.astype(_BF16)

    @pl.when(j == gate_tile)
    def _():
        rows = wt_ref[...]
        padded = jnp.concatenate(
            [rows[0:N_IF], jnp.zeros((WPREP_PAD, D_MODEL), _F32), rows[N_IF:INPROJ_TILE - WPREP_PAD]], axis=0)
        out_ref[...] = padded.T.astype(_BF16)


def _wprep_call(w_t):
    assert COL_IF % INPROJ_TILE == 0 and PROJ_COLS % INPROJ_TILE == 0 and N_IF % SUBLANES == 0
    gate_tile = COL_IF // INPROJ_TILE
    first_row = lambda j: (j * (INPROJ_TILE // SUBLANES)
                           - (j > gate_tile).astype(jnp.int32) * (WPREP_PAD // SUBLANES)) * SUBLANES
    return pl.pallas_call(
        _wprep_kernel,
        grid=(PROJ_COLS // INPROJ_TILE,),
        in_specs=[pl.BlockSpec((pl.Element(INPROJ_TILE), pl.Element(D_MODEL)), lambda j: (first_row(j), 0))],
        out_specs=pl.BlockSpec((D_MODEL, INPROJ_TILE), lambda j: (0, j)),
        out_shape=jax.ShapeDtypeStruct((D_MODEL, PROJ_COLS), _BF16),
        compiler_params=pltpu.CompilerParams(dimension_semantics=("arbitrary",)),
        name="weight_repack",
    )(w_t)


def _memkv_kernel(mem_ref, g_ref, w_ref, mk_ref, mv_ref, mk_rows_ref, mv_rows_ref):
    xn = _rms(mem_ref[0], g_ref[...]).astype(_BF16)
    kv = _dot(xn, w_ref[...])
    mk_ref[0] = kv[:, :XA_WIDTH]
    mv_ref[0] = kv[:, XA_WIDTH:]
    for h in range(XA_HEADS):
        head_rows = pl.ds(h, N_MEM, stride=XA_HEADS)
        mk_rows_ref[0, head_rows, :] = kv[:, h * XA_HD:(h + 1) * XA_HD]
        mv_rows_ref[0, head_rows, :] = kv[:, XA_WIDTH + h * XA_HD:XA_WIDTH + (h + 1) * XA_HD]


def _memkv_call(mem, g_mem, w_kv):
    b = mem.shape[0]
    const = lambda i: (0, 0)
    per_batch = lambda i: (i, 0, 0)
    return pl.pallas_call(
        _memkv_kernel,
        grid=(b,),
        in_specs=[
            pl.BlockSpec((1, N_MEM, D_MODEL), per_batch),
            pl.BlockSpec((1, D_MODEL), const),
            pl.BlockSpec((D_MODEL, 2 * XA_WIDTH), const),
        ],
        out_specs=[
            pl.BlockSpec((1, N_MEM, XA_WIDTH), per_batch),
            pl.BlockSpec((1, N_MEM, XA_WIDTH), per_batch),
            pl.BlockSpec((1, XA_KV_ROWS, XA_HD), per_batch),
            pl.BlockSpec((1, XA_KV_ROWS, XA_HD), per_batch),
        ],
        out_shape=[jax.ShapeDtypeStruct((b, N_MEM, XA_WIDTH), _F32)] * 2
        + [jax.ShapeDtypeStruct((b, XA_KV_ROWS, XA_HD), _F32)] * 2,
        compiler_params=pltpu.CompilerParams(dimension_semantics=("arbitrary",)),
        name="memkv",
    )(mem, g_mem, w_kv)


def _finish_block(proj, refill, side, x, mk_ref, mv_ref, bif_ref, ghead_ref, convw_ref, convb_ref, wout_ref,
                  gfinal_ref, y_ref, c_ref, n_ref, m_ref, conv_ref, ext_ref, acc_ref):
    tb = PROMPT_BLOCK

    def out_part(y_bf16, row0):
        return _dot(y_bf16, wout_ref[row0:row0 + y_bf16.shape[1], :])

    gates = proj(COL_IF, LANES)
    slab = (gates + bif_ref[...]).T[0:SUBLANES, :]
    m_prev8 = m_ref[0]
    m_prev = jnp.concatenate([m_prev8] * (tb // LANES), axis=1)
    a, r, m_t, w_inter, e_negm, w_s, w_c = _gate_rows(slab, m_prev, tb)
    cols = _gate_cols(r, w_inter, e_negm, w_s, w_c)
    m_ref[0] = jnp.broadcast_to(m_t[:, tb - 1:tb], (SUBLANES, LANES))

    row = lax.broadcasted_iota(jnp.int32, (tb, tb), 0)
    col = lax.broadcasted_iota(jnp.int32, (tb, tb), 1)
    causal = col <= row

    def ml_bf16(off, scale=None):
        out = []
        for h in range(ML_HEADS):
            v = proj(h * ML_HEAD_COLS + off, ML_HD)
            out.append((v if scale is None else v * scale).astype(_BF16))
        return out

    q_bf16 = ml_bf16(OFF_Q)
    k_bf16 = ml_bf16(OFF_K, ML_HD ** -0.5)
    v_bf16 = ml_bf16(OFF_V)

    def head_first(h):
        qb = q_bf16[h]
        c_old = c_ref[0, h]
        qk = _dot_nt(qb, k_bf16[h])
        qc = _dot(qb, c_old.astype(_BF16))
        return qb.astype(_F32), k_bf16[h].astype(_F32), c_old, qk, qc

    def head_second(h, first):
        q, k, c_old, qk, qc = first
        vb = v_bf16[h]
        r_col = cols[:, GC_R + h:GC_R + h + 1]
        wi_col = cols[:, GC_WI + h:GC_WI + h + 1]
        en_col = cols[:, GC_EN + h:GC_EN + h + 1]
        ws_col = cols[:, GC_WS + h:GC_WS + h + 1]
        wc = w_c[h:h + 1, 0:1]
        s_mat = qk * jnp.exp(jnp.where(causal, a[h:h + 1, :] - r_col, -jnp.inf))
        kw = k * ws_col
        sv = _dot(s_mat.astype(_BF16), vb)
        c_ref[0, h] = wc * c_old + _dot_tn(kw.astype(_BF16), vb)
        n_old = n_ref[0, h:h + 1, :]
        n_ref[0, h:h + 1, :] = wc * n_old + jnp.sum(kw, axis=0, keepdims=True)
        num = wi_col * qc + sv
        den = wi_col * jnp.sum(q * n_old, axis=1, keepdims=True) + jnp.sum(s_mat, axis=1, keepdims=True)
        hh = num * (1.0 / jnp.maximum(jnp.abs(den), en_col))
        return jax.nn.sigmoid(proj(h * ML_HEAD_COLS + OFF_O, ML_HD)) * hh

    def head_out(h, gated):
        hs = h * ML_HD
        y_ml = _rms(gated, ghead_ref[:, hs:hs + ML_HD]) * _silu(proj(COL_ZML + hs, ML_HD))
        return out_part(y_ml.astype(_BF16), hs)

    def xa_scores(h):
        hs = h * XA_HD
        kh = mk_ref[0, :, hs:hs + XA_HD].astype(_BF16)
        return _dot_nt(proj(COL_QX + hs, XA_HD).astype(_BF16), kh) * (XA_HD ** -0.5)

    def xa_out(h, s):
        hs = h * XA_HD
        vh = mv_ref[0, :, hs:hs + XA_HD].astype(_BF16)
        e = jnp.exp(s - jnp.max(s, axis=1, keepdims=True))
        p = e * (1.0 / jnp.sum(e, axis=1, keepdims=True))
        y_xa = _dot(p.astype(_BF16), vh) * _silu(proj(COL_ZX + hs, XA_HD))
        return y_xa.astype(_BF16)

    s0, s1, s2, s3 = (xa_scores(h) for h in range(XA_HEADS))
    side()
    refill(0, ML_WIDTH)
    first0 = head_first(0)
    first1 = head_first(1)
    side()
    y_xa0 = xa_out(0, s0)
    y_xa1 = xa_out(1, s1)
    refill(OFF_K, ML_WIDTH)
    side()
    first2 = head_first(2)
    first3 = head_first(3)
    y_xa2 = xa_out(2, s2)
    y_xa3 = xa_out(3, s3)
    refill(OFF_V, ML_WIDTH)
    side()

    u = proj(COL_CG, CONV_CH) * proj(COL_XC, CONV_CH)
    ext_ref[SUBLANES:SUBLANES + tb, :] = u
    u1 = ext_ref[SUBLANES - 1:SUBLANES - 1 + tb, :]
    u2 = ext_ref[SUBLANES - 2:SUBLANES - 2 + tb, :]
    yc = convb_ref[...] + convw_ref[0:1, :] * u2 + convw_ref[1:2, :] * u1 + convw_ref[2:3, :] * u
    y_cv = proj(COL_BG, CONV_CH) * yc * _silu(proj(COL_ZCV, CONV_CH))
    tail = u[tb - (CONV_K - 1):tb, :]
    ext_ref[SUBLANES - (CONV_K - 1):SUBLANES, :] = tail
    conv_ref[0] = tail
    side()
    gated0 = head_second(0, first0)
    refill(COL_QX, 2 * XA_WIDTH)
    gated1 = head_second(1, first1)
    acc_ref[...] = out_part(jnp.concatenate([y_cv.astype(_BF16), y_xa0, y_xa1, y_xa2, y_xa3], axis=1), ML_WIDTH)
    refill(COL_IF, COL_XC - COL_IF)
    side()
    gated2 = head_second(2, first2)
    acc = head_out(0, gated0)
    refill(COL_XC, COL_QX - COL_XC)
    gated3 = head_second(3, first3)
    acc = acc + head_out(1, gated1)
    refill(OFF_O, ML_WIDTH)
    acc = acc + head_out(2, gated2)
    acc = acc + head_out(3, gated3)
    refill(COL_ZML, ML_WIDTH)

    y_ref[0] = _rms(acc_ref[...] + acc + x, gfinal_ref[...])


def _prompt_kernel(xnext_ref, x_ref, mk_ref, mv_ref, gnorm_ref, w_ref, bif_ref, ghead_ref, convw_ref, convb_ref,
                   wout_ref, gfinal_ref,
                   sproj_ref, scols_ref, ssv_ref, srs_ref, sc0_ref, sn0_ref, sconv0_ref, smk_ref, smv_ref,
                   y_ref, c_ref, n_ref, m_ref, conv_ref,
                   symix_ref, sc_ref, sn_ref, sconv_ref,
                   proj_ref, ext_ref, acc_ref, sext_ref, *, blocks_per_seq, sample_seqs):
    s = pl.program_id(0)

    @pl.when(s == 0)
    def _():
        proj_ref[...] = jnp.zeros_like(proj_ref)

    @pl.when((s == 0) | (lax.rem(s + blocks_per_seq - 1, blocks_per_seq) == 0))
    def _():
        c_ref[...] = jnp.zeros_like(c_ref)
        n_ref[...] = jnp.zeros_like(n_ref)
        m_ref[...] = jnp.zeros_like(m_ref)
        ext_ref[0:SUBLANES, :] = jnp.zeros((SUBLANES, CONV_CH), _F32)

    sample_stages = _sample_seqs(
        sample_seqs, sproj_ref, scols_ref, ssv_ref, srs_ref, sc0_ref, sn0_ref, sconv0_ref, smk_ref, smv_ref,
        ghead_ref, convw_ref, convb_ref, symix_ref, sc_ref, sn_ref, sconv_ref, sext_ref)

    xb_next = _rms(xnext_ref[0], gnorm_ref[...]).astype(_BF16)

    def refill(col, width):
        proj_ref[:, col:col + width] = _dot(xb_next, w_ref[:, col:col + width])

    _finish_block(lambda col, width: proj_ref[:, col:col + width], refill, lambda: next(sample_stages, None),
                  x_ref[0], mk_ref, mv_ref, bif_ref, ghead_ref, convw_ref, convb_ref, wout_ref, gfinal_ref,
                  y_ref, c_ref, n_ref, m_ref, conv_ref, ext_ref, acc_ref)
    for _ in sample_stages:
        pass


def _prompt_call(x, mk, mv, g_norm, w_all, b_if, g_head, conv_w, conv_b, w_out, g_final,
                 s_proj, s_cols, s_sv, s_rs, s_c0, s_n0, s_conv0, s_mk, s_mv):
    b, t, _ = x.shape
    tb = PROMPT_BLOCK
    nt = t // tb
    n_blocks = b * nt
    n_seq = s_c0.shape[0]
    assert n_seq % n_blocks == 0
    ss = n_seq // n_blocks
    st = ss * DEC_SEQ_LEN
    grp = lambda s: jnp.minimum(s, n_blocks - 1)
    grp2 = lambda s: (grp(s), 0)
    grp3 = lambda s: (grp(s), 0, 0)
    grp4 = lambda s: (grp(s), 0, 0, 0)
    const2 = lambda s: (0, 0)
    resident = functools.partial(pl.BlockSpec, index_map=const2, pipeline_mode=pl.Buffered(1))
    nxt = lambda s: jnp.minimum(s, n_blocks - 1)
    cur = lambda s: jnp.maximum(s - 1, 0)
    cur_batch3 = lambda s: (cur(s) // nt, 0, 0)
    return pl.pallas_call(
        functools.partial(_prompt_kernel, blocks_per_seq=nt, sample_seqs=ss),
        grid=(n_blocks + 1,),
        in_specs=[
            pl.BlockSpec((1, tb, D_MODEL), lambda s: (nxt(s) // nt, nxt(s) % nt, 0)),
            pl.BlockSpec((1, tb, D_MODEL), lambda s: (cur(s) // nt, cur(s) % nt, 0)),
            pl.BlockSpec((1, N_MEM, XA_WIDTH), cur_batch3),
            pl.BlockSpec((1, N_MEM, XA_WIDTH), cur_batch3),
            pl.BlockSpec((1, D_MODEL), const2),
            resident((D_MODEL, PROJ_COLS)),
            pl.BlockSpec((1, LANES), const2),
            pl.BlockSpec((1, ML_WIDTH), const2),
            pl.BlockSpec((CONV_K, CONV_CH), const2),
            pl.BlockSpec((1, CONV_CH), const2),
            resident((MIX_WIDTH, D_MODEL)),
            pl.BlockSpec((1, D_MODEL), const2),
            pl.BlockSpec((st, PROJ_COLS), grp2),
            pl.BlockSpec((st, LANES), grp2),
            pl.BlockSpec((st, ML_WIDTH), grp2),
            pl.BlockSpec((st, LANES), grp2),
            pl.BlockSpec((ss, ML_HEADS, ML_HD, ML_HD), grp4),
            pl.BlockSpec((ss, ML_HEADS, ML_HD), grp3),
            pl.BlockSpec((ss, CONV_K - 1, CONV_CH), grp3),
            pl.BlockSpec((ss, XA_KV_ROWS, XA_HD), grp3),
            pl.BlockSpec((ss, XA_KV_ROWS, XA_HD), grp3),
        ],
        out_specs=[
            pl.BlockSpec((1, tb, D_MODEL), lambda s: (cur(s) // nt, cur(s) % nt, 0)),
            pl.BlockSpec((1, ML_HEADS, ML_HD, ML_HD), lambda s: (cur(s) // nt, 0, 0, 0)),
            pl.BlockSpec((1, ML_HEADS, ML_HD), cur_batch3),
            pl.BlockSpec((1, SUBLANES, LANES), cur_batch3),
            pl.BlockSpec((1, CONV_K - 1, CONV_CH), cur_batch3),
            pl.BlockSpec((st, MIX_WIDTH), grp2),
            pl.BlockSpec((ss, ML_HEADS, ML_HD, ML_HD), grp4),
            pl.BlockSpec((ss, ML_HEADS, ML_HD), grp3),
            pl.BlockSpec((ss, CONV_K - 1, CONV_CH), grp3),
        ],
        out_shape=[
            jax.ShapeDtypeStruct((b, t, D_MODEL), _F32),
            jax.ShapeDtypeStruct((b, ML_HEADS, ML_HD, ML_HD), _F32),
            jax.ShapeDtypeStruct((b, ML_HEADS, ML_HD), _F32),
            jax.ShapeDtypeStruct((b, SUBLANES, LANES), _F32),
            jax.ShapeDtypeStruct((b, CONV_K - 1, CONV_CH), _F32),
            jax.ShapeDtypeStruct((n_seq * DEC_SEQ_LEN, MIX_WIDTH), _F32),
            jax.ShapeDtypeStruct(s_c0.shape, _F32),
            jax.ShapeDtypeStruct(s_n0.shape, _F32),
            jax.ShapeDtypeStruct(s_conv0.shape, _F32),
        ],
        scratch_shapes=[
            pltpu.VMEM((tb, PROJ_COLS), _F32),
            pltpu.VMEM((SUBLANES + tb, CONV_CH), _F32),
            pltpu.VMEM((tb, D_MODEL), _F32),
            pltpu.VMEM((ss, 2 * SUBLANES, CONV_CH), _F32),
        ],
        compiler_params=pltpu.CompilerParams(
            dimension_semantics=("arbitrary",), vmem_limit_bytes=VMEM_LIMIT),
        name="prompt_layer",
    )(x, x, mk, mv, g_norm, w_all, b_if, g_head, conv_w, conv_b, w_out, g_final,
      s_proj, s_cols, s_sv, s_rs, s_c0, s_n0, s_conv0, s_mk, s_mv)


def _sample_kernel(x_ref, gnorm_ref, w_ref, bif_ref, m0_ref, c0_ref, n0_ref, conv0_ref, mk_ref, mv_ref,
                   ghead_ref, convw_ref, convb_ref, wout_ref, gfinal_ref,
                   y_ref, c_ref, n_ref, conv_ref, mrows_ref,
                   proj_ref, cols_ref, sv_ref, rs_ref, ext_ref, ymix_ref):
    nt = SAMPLE_BLOCK_SEQS * DEC_SEQ_LEN
    j = pl.program_id(1)

    @pl.when(j == 0)
    def _():
        proj_ref[...] = _dot(_rms(x_ref[...], gnorm_ref[...]).astype(_BF16), w_ref[...])
        slab = (proj_ref[:, COL_IF:COL_IF + LANES] + bif_ref[...]).T[0:SUBLANES, :]
        a, r, m_t, w_inter, e_negm, w_s, w_c = _gate_rows(slab, m0_ref[...], DEC_SEQ_LEN)
        cols_ref[...] = _gate_cols(r, w_inter, e_negm, w_s, w_c)
        mrows_ref[...] = m_t
        row = lax.broadcasted_iota(jnp.int32, (nt, nt), 0)
        col = lax.broadcasted_iota(jnp.int32, (nt, nt), 1)
        mask = (col <= row) & ((row // DEC_SEQ_LEN) == (col // DEC_SEQ_LEN))
        for h in range(ML_HEADS):
            hs = h * ML_HD
            hc = h * ML_HEAD_COLS
            qb = proj_ref[:, hc + OFF_Q:hc + OFF_Q + ML_HD].astype(_BF16)
            kb = (proj_ref[:, hc + OFF_K:hc + OFF_K + ML_HD] * (ML_HD ** -0.5)).astype(_BF16)
            vb = proj_ref[:, hc + OFF_V:hc + OFF_V + ML_HD].astype(_BF16)
            r_col = cols_ref[:, GC_R + h:GC_R + h + 1]
            decay = jnp.exp(jnp.where(mask, a[h:h + 1, :] - r_col, -jnp.inf))
            s_mat = _dot_nt(qb, kb) * decay
            sv_ref[:, hs:hs + ML_HD] = _dot(s_mat.astype(_BF16), vb)
            rs_ref[:, h:h + 1] = jnp.sum(s_mat, axis=1, keepdims=True)

    xa_rows = XA_HEADS * DEC_SEQ_LEN
    own_head = (lax.broadcasted_iota(jnp.int32, (xa_rows, XA_KV_ROWS), 0) // DEC_SEQ_LEN
                == lax.broadcasted_iota(jnp.int32, (xa_rows, XA_KV_ROWS), 1) % XA_HEADS)

    seqs = range(SAMPLE_STEP_SEQS)
    pairs = [(sl, h) for sl in seqs for h in range(ML_HEADS)]
    rows_of = [pl.ds(pl.multiple_of((j * SAMPLE_STEP_SEQS + sl) * DEC_SEQ_LEN, DEC_SEQ_LEN), DEC_SEQ_LEN)
               for sl in seqs]

    def ml(sl, h, off):
        return proj_ref[rows_of[sl], h * ML_HEAD_COLS + off:h * ML_HEAD_COLS + off + ML_HD]

    scores = []
    for sl in seqs:
        q4 = jnp.concatenate(
            [proj_ref[rows_of[sl], COL_QX + h * XA_HD:COL_QX + (h + 1) * XA_HD] for h in range(XA_HEADS)], axis=0)
        s = _dot_nt(q4.astype(_BF16), mk_ref[sl].astype(_BF16)) * (XA_HD ** -0.5)
        scores.append(jnp.where(own_head, s, -jnp.inf))

    qc = {(sl, h): _dot(ml(sl, h, OFF_Q).astype(_BF16), c0_ref[sl, h].astype(_BF16)) for sl, h in pairs}

    y4 = []
    for sl in seqs:
        e = jnp.exp(scores[sl] - jnp.max(scores[sl], axis=1, keepdims=True))
        p = e * (1.0 / jnp.sum(e, axis=1, keepdims=True))
        y4.append(_dot(p.astype(_BF16), mv_ref[sl].astype(_BF16)))

    for sl, h in pairs:
        gc = cols_ref[rows_of[sl], :]
        wc = gc[0:1, GC_WC + h:GC_WC + h + 1]
        kw = ml(sl, h, OFF_K) * (ML_HD ** -0.5) * gc[:, GC_WS + h:GC_WS + h + 1]
        c_ref[sl, h] = wc * c0_ref[sl, h] + _dot_tn(kw.astype(_BF16), ml(sl, h, OFF_V).astype(_BF16))
        n_ref[sl, h:h + 1, :] = wc * n0_ref[sl, h:h + 1, :] + jnp.sum(kw, axis=0, keepdims=True)

    step_tokens = SAMPLE_STEP_SEQS * DEC_SEQ_LEN
    step_rows = pl.ds(pl.multiple_of(j * step_tokens, step_tokens), step_tokens)
    gc_step = cols_ref[step_rows, :]
    for h in range(ML_HEADS):
        hs = h * ML_HD
        hc = h * ML_HEAD_COLS
        wi_col = gc_step[:, GC_WI + h:GC_WI + h + 1]
        en_col = gc_step[:, GC_EN + h:GC_EN + h + 1]
        qc_step = jnp.concatenate([qc[sl, h] for sl in seqs], axis=0)
        n_step = jnp.concatenate(
            [jnp.broadcast_to(n0_ref[sl, h:h + 1, :], (DEC_SEQ_LEN, ML_HD)) for sl in seqs], axis=0)
        q_step = proj_ref[step_rows, hc + OFF_Q:hc + OFF_Q + ML_HD]
        num = wi_col * qc_step + sv_ref[step_rows, hs:hs + ML_HD]
        den = wi_col * jnp.sum(q_step * n_step, axis=1, keepdims=True) + rs_ref[step_rows, h:h + 1]
        hh = num * (1.0 / jnp.maximum(jnp.abs(den), en_col))
        ymix_ref[step_rows, hs:hs + ML_HD] = _ml_head_out(
            hh, proj_ref[step_rows, hc + OFF_O:hc + OFF_O + ML_HD],
            proj_ref[step_rows, COL_ZML + hs:COL_ZML + hs + ML_HD], ghead_ref[:, hs:hs + ML_HD]).astype(_BF16)

    y_cv_parts, y_xa_parts = [], []
    for sl in seqs:
        rows = rows_of[sl]
        u = proj_ref[rows, COL_CG:COL_CG + CONV_CH] * proj_ref[rows, COL_XC:COL_XC + CONV_CH]
        ext_ref[sl, SUBLANES - (CONV_K - 1):SUBLANES, :] = conv0_ref[sl]
        ext_ref[sl, SUBLANES:2 * SUBLANES, :] = u
        u1 = ext_ref[sl, SUBLANES - 1:2 * SUBLANES - 1, :]
        u2 = ext_ref[sl, SUBLANES - 2:2 * SUBLANES - 2, :]
        yc = convb_ref[...] + convw_ref[0:1, :] * u2 + convw_ref[1:2, :] * u1 + convw_ref[2:3, :] * u
        y_cv = (proj_ref[rows, COL_BG:COL_BG + CONV_CH] * yc
                * _silu(proj_ref[rows, COL_ZCV:COL_ZCV + CONV_CH]))
        y_cv_parts.append(y_cv)
        conv_ref[sl] = u[DEC_SEQ_LEN - (CONV_K - 1):DEC_SEQ_LEN, :]
        y_xa_parts.append(jnp.concatenate(
            [y4[sl][h * DEC_SEQ_LEN:(h + 1) * DEC_SEQ_LEN, :] for h in range(XA_HEADS)], axis=1)
            * _silu(proj_ref[rows, COL_ZX:COL_ZX + XA_WIDTH]))
    ymix_ref[step_rows, ML_WIDTH:ML_WIDTH + CONV_CH] = jnp.concatenate(y_cv_parts, axis=0).astype(_BF16)
    ymix_ref[step_rows, ML_WIDTH + CONV_CH:MIX_WIDTH] = jnp.concatenate(y_xa_parts, axis=0).astype(_BF16)

    @pl.when(j == SAMPLE_BLOCK_SEQS // SAMPLE_STEP_SEQS - 1)
    def _():
        y_ref[...] = _out_tail(ymix_ref[...], wout_ref[...], x_ref[...], gfinal_ref[...])


def _sample_call(x, g_norm, w_all, b_if, m0_rows, c0, n0, conv0, mk, mv, g_head, conv_w, conv_b, w_out, g_final):
    n_seq = c0.shape[0]
    nt = SAMPLE_BLOCK_SEQS * DEC_SEQ_LEN
    steps = SAMPLE_BLOCK_SEQS // SAMPLE_STEP_SEQS
    ss = SAMPLE_STEP_SEQS
    const2 = lambda i, j: (0, 0)
    resident = functools.partial(pl.BlockSpec, index_map=const2, pipeline_mode=pl.Buffered(1))
    blk2 = lambda i, j: (i, 0)
    seq3 = lambda i, j: (i * steps + j, 0, 0)
    seq4 = lambda i, j: (i * steps + j, 0, 0, 0)
    return pl.pallas_call(
        _sample_kernel,
        grid=(n_seq // SAMPLE_BLOCK_SEQS, steps),
        in_specs=[
            pl.BlockSpec((nt, D_MODEL), blk2),
            pl.BlockSpec((1, D_MODEL), const2),
            resident((D_MODEL, PROJ_COLS)),
            pl.BlockSpec((1, LANES), const2),
            pl.BlockSpec((SUBLANES, nt), lambda i, j: (0, i)),
            pl.BlockSpec((ss, ML_HEADS, ML_HD, ML_HD), seq4),
            pl.BlockSpec((ss, ML_HEADS, ML_HD), seq3),
            pl.BlockSpec((ss, CONV_K - 1, CONV_CH), seq3),
            pl.BlockSpec((ss, XA_KV_ROWS, XA_HD), seq3),
            pl.BlockSpec((ss, XA_KV_ROWS, XA_HD), seq3),
            pl.BlockSpec((1, ML_WIDTH), const2),
            pl.BlockSpec((CONV_K, CONV_CH), const2),
            pl.BlockSpec((1, CONV_CH), const2),
            resident((MIX_WIDTH, D_MODEL)),
            pl.BlockSpec((1, D_MODEL), const2),
        ],
        out_specs=[
            pl.BlockSpec((nt, D_MODEL), blk2),
            pl.BlockSpec((ss, ML_HEADS, ML_HD, ML_HD), seq4),
            pl.BlockSpec((ss, ML_HEADS, ML_HD), seq3),
            pl.BlockSpec((ss, CONV_K - 1, CONV_CH), seq3),
            pl.BlockSpec((SUBLANES, nt), lambda i, j: (0, i)),
        ],
        out_shape=[
            jax.ShapeDtypeStruct((n_seq * DEC_SEQ_LEN, D_MODEL), _F32),
            jax.ShapeDtypeStruct(c0.shape, _F32),
            jax.ShapeDtypeStruct(n0.shape, _F32),
            jax.ShapeDtypeStruct(conv0.shape, _F32),
            jax.ShapeDtypeStruct((SUBLANES, n_seq * DEC_SEQ_LEN), _F32),
        ],
        scratch_shapes=[
            pltpu.VMEM((nt, PROJ_COLS), _F32),
            pltpu.VMEM((nt, LANES), _F32),
            pltpu.VMEM((nt, ML_WIDTH), _F32),
            pltpu.VMEM((nt, LANES), _F32),
            pltpu.VMEM((ss, 2 * SUBLANES, CONV_CH), _F32),
            pltpu.VMEM((nt, MIX_WIDTH), _BF16),
        ],
        compiler_params=pltpu.CompilerParams(
            dimension_semantics=("arbitrary", "arbitrary"), vmem_limit_bytes=VMEM_LIMIT),
        name="sample_layer",
    )(x, g_norm, w_all, b_if, m0_rows, c0, n0, conv0, mk, mv, g_head, conv_w, conv_b, w_out, g_final)


def _sample_pre_kernel(x_ref, gnorm_ref, w_ref, bif_ref, m0_ref, proj_ref, cols_ref, sv_ref, rs_ref, mrows_ref):
    nt = x_ref.shape[0]
    proj_ref[...] = _dot(_rms(x_ref[...], gnorm_ref[...]).astype(_BF16), w_ref[...])
    slab = (proj_ref[:, COL_IF:COL_IF + LANES] + bif_ref[...]).T[0:SUBLANES, :]
    a, r, m_t, w_inter, e_negm, w_s, w_c = _gate_rows(slab, m0_ref[...], DEC_SEQ_LEN)
    cols = _gate_cols(r, w_inter, e_negm, w_s, w_c)
    cols_ref[...] = cols
    mrows_ref[...] = m_t
    row = lax.broadcasted_iota(jnp.int32, (nt, nt), 0)
    col = lax.broadcasted_iota(jnp.int32, (nt, nt), 1)
    mask = (col <= row) & ((row // DEC_SEQ_LEN) == (col // DEC_SEQ_LEN))
    rs_ref[...] = jnp.zeros_like(rs_ref)
    for h in range(ML_HEADS):
        hs = h * ML_HD
        hc = h * ML_HEAD_COLS
        qb = proj_ref[:, hc + OFF_Q:hc + OFF_Q + ML_HD].astype(_BF16)
        kb = (proj_ref[:, hc + OFF_K:hc + OFF_K + ML_HD] * (ML_HD ** -0.5)).astype(_BF16)
        vb = proj_ref[:, hc + OFF_V:hc + OFF_V + ML_HD].astype(_BF16)
        decay = jnp.exp(jnp.where(mask, a[h:h + 1, :] - cols[:, GC_R + h:GC_R + h + 1], -jnp.inf))
        s_mat = _dot_nt(qb, kb) * decay
        sv_ref[:, hs:hs + ML_HD] = _dot(s_mat.astype(_BF16), vb)
        rs_ref[:, h:h + 1] = jnp.sum(s_mat, axis=1, keepdims=True)


def _sample_pre_call(x, g_norm, w_all, b_if, m0_rows):
    n_tok = x.shape[0]
    nt = SAMPLE_BLOCK_SEQS * DEC_SEQ_LEN
    const = lambda i: (0, 0)
    blk = lambda i: (i, 0)
    rows_blk = lambda i: (0, i)
    return pl.pallas_call(
        _sample_pre_kernel,
        grid=(n_tok // nt,),
        in_specs=[
            pl.BlockSpec((nt, D_MODEL), blk),
            pl.BlockSpec((1, D_MODEL), const),
            pl.BlockSpec((D_MODEL, PROJ_COLS), const, pipeline_mode=pl.Buffered(1)),
            pl.BlockSpec((1, LANES), const),
            pl.BlockSpec((SUBLANES, nt), rows_blk),
        ],
        out_specs=[
            pl.BlockSpec((nt, PROJ_COLS), blk),
            pl.BlockSpec((nt, LANES), blk),
            pl.BlockSpec((nt, ML_WIDTH), blk),
            pl.BlockSpec((nt, LANES), blk),
            pl.BlockSpec((SUBLANES, nt), rows_blk),
        ],
        out_shape=[
            jax.ShapeDtypeStruct((n_tok, PROJ_COLS), _F32),
            jax.ShapeDtypeStruct((n_tok, LANES), _F32),
            jax.ShapeDtypeStruct((n_tok, ML_WIDTH), _F32),
            jax.ShapeDtypeStruct((n_tok, LANES), _F32),
            jax.ShapeDtypeStruct((SUBLANES, n_tok), _F32),
        ],
        compiler_params=pltpu.CompilerParams(
            dimension_semantics=("arbitrary",), vmem_limit_bytes=VMEM_LIMIT),
        name="sample_pre",
    )(x, g_norm, w_all, b_if, m0_rows)


def _sample_seqs(n_seqs, proj_ref, cols_ref, sv_ref, rs_ref, c0_ref, n0_ref, conv0_ref, mk_ref, mv_ref,
                 ghead_ref, convw_ref, convb_ref, ymix_ref, c_ref, n_ref, conv_ref, ext_ref):
    seqs = range(n_seqs)
    pairs = [(sl, h) for sl in seqs for h in range(ML_HEADS)]
    rows_of = [slice(sl * DEC_SEQ_LEN, (sl + 1) * DEC_SEQ_LEN) for sl in seqs]
    xa_rows = XA_HEADS * DEC_SEQ_LEN
    own_head = (lax.broadcasted_iota(jnp.int32, (xa_rows, XA_KV_ROWS), 0) // DEC_SEQ_LEN
                == lax.broadcasted_iota(jnp.int32, (xa_rows, XA_KV_ROWS), 1) % XA_HEADS)

    def ml(sl, h, off):
        return proj_ref[rows_of[sl], h * ML_HEAD_COLS + off:h * ML_HEAD_COLS + off + ML_HD]

    scores = []
    for sl in seqs:
        q4 = jnp.concatenate(
            [proj_ref[rows_of[sl], COL_QX + h * XA_HD:COL_QX + (h + 1) * XA_HD] for h in range(XA_HEADS)], axis=0)
        s = _dot_nt(q4.astype(_BF16), mk_ref[sl].astype(_BF16)) * (XA_HD ** -0.5)
        scores.append(jnp.where(own_head, s, -jnp.inf))

    yield
    qc = {(sl, h): _dot(ml(sl, h, OFF_Q).astype(_BF16), c0_ref[sl, h].astype(_BF16)) for sl, h in pairs}

    yield
    y4 = []
    for sl in seqs:
        e = jnp.exp(scores[sl] - jnp.max(scores[sl], axis=1, keepdims=True))
        p = e * (1.0 / jnp.sum(e, axis=1, keepdims=True))
        y4.append(_dot(p.astype(_BF16), mv_ref[sl].astype(_BF16)))

    yield
    for sl, h in pairs:
        gc = cols_ref[rows_of[sl], :]
        wc = gc[0:1, GC_WC + h:GC_WC + h + 1]
        kw = ml(sl, h, OFF_K) * (ML_HD ** -0.5) * gc[:, GC_WS + h:GC_WS + h + 1]
        c_ref[sl, h] = wc * c0_ref[sl, h] + _dot_tn(kw.astype(_BF16), ml(sl, h, OFF_V).astype(_BF16))
        n_ref[sl, h:h + 1, :] = wc * n0_ref[sl, h:h + 1, :] + jnp.sum(kw, axis=0, keepdims=True)
        if h == ML_HEADS - 1:
            yield

    gc_all = cols_ref[...]
    for h in range(ML_HEADS):
        hs = h * ML_HD
        hc = h * ML_HEAD_COLS
        wi_col = gc_all[:, GC_WI + h:GC_WI + h + 1]
        en_col = gc_all[:, GC_EN + h:GC_EN + h + 1]
        qc_all = jnp.concatenate([qc[sl, h] for sl in seqs], axis=0)
        n_all = jnp.concatenate(
            [jnp.broadcast_to(n0_ref[sl, h:h + 1, :], (DEC_SEQ_LEN, ML_HD)) for sl in seqs], axis=0)
        num = wi_col * qc_all + sv_ref[:, hs:hs + ML_HD]
        den = (wi_col * jnp.sum(proj_ref[:, hc + OFF_Q:hc + OFF_Q + ML_HD] * n_all, axis=1, keepdims=True)
               + rs_ref[:, h:h + 1])
        hh = num * (1.0 / jnp.maximum(jnp.abs(den), en_col))
        ymix_ref[:, hs:hs + ML_HD] = _ml_head_out(
            hh, proj_ref[:, hc + OFF_O:hc + OFF_O + ML_HD], proj_ref[:, COL_ZML + hs:COL_ZML + hs + ML_HD],
            ghead_ref[:, hs:hs + ML_HD])

    for sl in seqs:
        rows = rows_of[sl]
        u = proj_ref[rows, COL_CG:COL_CG + CONV_CH] * proj_ref[rows, COL_XC:COL_XC + CONV_CH]
        ext_ref[sl, SUBLANES - (CONV_K - 1):SUBLANES, :] = conv0_ref[sl]
        ext_ref[sl, SUBLANES:2 * SUBLANES, :] = u
        u1 = ext_ref[sl, SUBLANES - 1:2 * SUBLANES - 1, :]
        u2 = ext_ref[sl, SUBLANES - 2:2 * SUBLANES - 2, :]
        yc = convb_ref[...] + convw_ref[0:1, :] * u2 + convw_ref[1:2, :] * u1 + convw_ref[2:3, :] * u
        ymix_ref[rows, ML_WIDTH:ML_WIDTH + CONV_CH] = (
            proj_ref[rows, COL_BG:COL_BG + CONV_CH] * yc * _silu(proj_ref[rows, COL_ZCV:COL_ZCV + CONV_CH]))
        conv_ref[sl] = u[DEC_SEQ_LEN - (CONV_K - 1):DEC_SEQ_LEN, :]
        ymix_ref[rows, ML_WIDTH + CONV_CH:MIX_WIDTH] = jnp.concatenate(
            [y4[sl][h * DEC_SEQ_LEN:(h + 1) * DEC_SEQ_LEN, :] for h in range(XA_HEADS)], axis=1
        ) * _silu(proj_ref[rows, COL_ZX:COL_ZX + XA_WIDTH])


def _sample_post_kernel(ymix_ref, x_ref, wout_ref, gfinal_ref, y_ref):
    y_ref[...] = _out_tail(ymix_ref[...].astype(_BF16), wout_ref[...], x_ref[...], gfinal_ref[...])


def _sample_post_call(ymix, x, w_out, g_final):
    n_tok = x.shape[0]
    rb = 256
    const = lambda i: (0, 0)
    blk = lambda i: (i, 0)
    return pl.pallas_call(
        _sample_post_kernel,
        grid=(n_tok // rb,),
        in_specs=[
            pl.BlockSpec((rb, MIX_WIDTH), blk),
            pl.BlockSpec((rb, D_MODEL), blk),
            pl.BlockSpec((MIX_WIDTH, D_MODEL), const),
            pl.BlockSpec((1, D_MODEL), const),
        ],
        out_specs=pl.BlockSpec((rb, D_MODEL), blk),
        out_shape=jax.ShapeDtypeStruct((n_tok, D_MODEL), _F32),
        compiler_params=pltpu.CompilerParams(dimension_semantics=("arbitrary",)),
        name="sample_post",
    )(ymix, x, w_out, g_final)


def kernel(x_prompt, x_sample, state_mlstm_C, state_mlstm_n, state_mlstm_m, state_conv, cache_mem_k, cache_mem_v, mem_prompt, g_norm, w_in, b_if, g_head, conv_w, conv_b, g_mem, w_mem_kv, w_out, g_final):
    assert w_in.shape[0] == 1, "single-layer trunk"
    bp, tp, _ = x_prompt.shape
    bs, ts, _ = x_sample.shape
    assert ts == DEC_SEQ_LEN and tp % PROMPT_BLOCK == 0 and bs % SAMPLE_BLOCK_SEQS == 0

    w_all = _wprep_call(jnp.swapaxes(w_in[0], 0, 1))
    w_out_b = w_out[0].astype(_BF16)
    w_kv_b = w_mem_kv[0].astype(_BF16)
    b_if_p = jnp.pad(b_if[0][None, :], ((0, 0), (0, LANES - N_IF)))
    row = lambda v: v.reshape(1, -1)

    n_tok = bs * ts
    xs = x_sample.reshape(n_tok, D_MODEL)
    m0_rows = jnp.pad(jnp.repeat(state_mlstm_m[0].T, ts, axis=1), ((0, SUBLANES - ML_HEADS), (0, 0)))
    s_proj, s_cols, s_sv, s_rs, m_rows = _sample_pre_call(xs, row(g_norm[0]), w_all, b_if_p, m0_rows)

    mk_p, mv_p, mk_rows, mv_rows = _memkv_call(mem_prompt, row(g_mem[0]), w_kv_b)
    y_p, c_p, n_p, m_p, conv_p, s_ymix, c_s, n_s, conv_s = _prompt_call(
        x_prompt, mk_p, mv_p, row(g_norm[0]), w_all, b_if_p, row(g_head[0]), conv_w[0], row(conv_b[0]),
        w_out_b, row(g_final),
        s_proj, s_cols, s_sv, s_rs, state_mlstm_C[0], state_mlstm_n[0], state_conv[0],
        cache_mem_k[0].reshape(bs, XA_KV_ROWS, XA_HD), cache_mem_v[0].reshape(bs, XA_KV_ROWS, XA_HD))

    y_s = _sample_post_call(s_ymix, xs, w_out_b, row(g_final))
    m_s = m_rows[:ML_HEADS, ts - 1::ts].T

    kv_shape = (1, bp, N_MEM, XA_HEADS, XA_HD)
    return (y_p, y_s.reshape(bs, ts, D_MODEL),
            c_p[None], n_p[None], m_p[:, :ML_HEADS, 0][None], conv_p[None],
            mk_rows.reshape(kv_shape), mv_rows.reshape(kv_shape),
            c_s[None], n_s[None], m_s[None], conv_s[None])
```

```python
import functools

import jax
import jax.numpy as jnp
from jax import lax
from jax.experimental import pallas as pl
from jax.experimental.pallas import tpu as pltpu

D_MODEL = 1024
ML_HEADS = 4
ML_HD = 256
ML_WIDTH = ML_HEADS * ML_HD
CONV_CH = 512
CONV_K = 3
XA_HEADS = 4
XA_HD = 128
XA_WIDTH = XA_HEADS * XA_HD
N_MEM = 256
XA_KV_ROWS = N_MEM * XA_HEADS
DEC_SEQ_LEN = 8
MIX_WIDTH = ML_WIDTH + CONV_CH + XA_WIDTH
EPS = 1e-6

SUBLANES = 8
LANES = 128

ML_HEAD_COLS = ML_HD
OFF_Q = 0
OFF_K = ML_WIDTH
OFF_V = 2 * ML_WIDTH
OFF_O = 3 * ML_WIDTH
COL_ZML = 4 * ML_WIDTH
COL_IF = 5 * ML_WIDTH
COL_BG = COL_IF + LANES
COL_CG = COL_BG + CONV_CH
COL_XC = COL_CG + CONV_CH
COL_ZCV = COL_XC + CONV_CH
COL_QX = COL_ZCV + CONV_CH
COL_ZX = COL_QX + XA_WIDTH
PROJ_COLS = COL_ZX + XA_WIDTH
N_IF = 2 * ML_HEADS

GC_R = 0 * SUBLANES
GC_WI = 1 * SUBLANES
GC_EN = 2 * SUBLANES
GC_WS = 3 * SUBLANES
GC_WC = 4 * SUBLANES
GC_USED = 5 * SUBLANES

PROMPT_BLOCK = 256
REFILL_PIECE = 512
SAMPLE_BLOCK_SEQS = 16
INPROJ_TILE = 640
VMEM_LIMIT = 63 * 1024 * 1024

_F32 = jnp.float32
_BF16 = jnp.bfloat16


def _dot(a, b):
    return jnp.dot(a, b, preferred_element_type=_F32)


def _dot_nt(a, b):
    return lax.dot_general(a, b, (((1,), (1,)), ((), ())), preferred_element_type=_F32)


def _dot_tn(a, b):
    return lax.dot_general(a, b, (((0,), (0,)), ((), ())), preferred_element_type=_F32)


def _rms(x, g):
    r = lax.rsqrt(jnp.mean(x * x, axis=-1, keepdims=True) + EPS)
    return (x * r) * g


def _silu(x):
    return x * jax.nn.sigmoid(x)


def _log_sigmoid(x):
    return jnp.minimum(x, 0.0) - jnp.log1p(jnp.exp(-jnp.abs(x)))


def _gate_rows(slab, m_prev, seg):
    n = slab.shape[1]
    pos = lax.broadcasted_iota(jnp.int32, slab.shape, 1) & (seg - 1)

    def scan(x, op, fill, reverse=False):
        k = 1
        while k < seg:
            if reverse:
                shifted, ok = pltpu.roll(x, n - k, axis=1), pos < seg - k
            else:
                shifted, ok = pltpu.roll(x, k, axis=1), pos >= k
            x = op(x, jnp.where(ok, shifted, fill))
            k *= 2
        return x

    logf = _log_sigmoid(pltpu.roll(slab, ML_HEADS, axis=0))
    f_cum = scan(logf, jnp.add, 0.0)
    a = slab - f_cum
    r = jnp.maximum(scan(a, jnp.maximum, -jnp.inf), m_prev)
    r_last = scan(r, jnp.maximum, -jnp.inf, reverse=True)
    m_t = f_cum + r
    w_inter = jnp.exp(m_prev - r)
    e_negm = jnp.exp(-m_t)
    w_s = jnp.exp(a - r_last)
    w_c = jnp.exp(m_prev - r_last)
    return a, r, m_t, w_inter, e_negm, w_s, w_c


def _gate_cols(r, w_inter, e_negm, w_s, w_c):
    n = r.shape[1]
    pad = jnp.zeros((LANES - GC_USED, n), _F32)
    return jnp.concatenate([r, w_inter, e_negm, w_s, w_c, pad], axis=0).T


def _ml_head_out(hh, o_pre, z, g_head):
    hh = jax.nn.sigmoid(o_pre) * hh
    return _rms(hh, g_head) * _silu(z)


def _out_tail(ymix_bf16, w_out, x, g_final):
    y = _dot(ymix_bf16, w_out) + x
    return _rms(y, g_final)


WPREP_PAD = LANES - N_IF


WPREP_CAST_STEPS = 8


def _wprep_kernel(wt_ref, wout_ref, wkv_ref, out_ref, wout_bf16_ref, wkv_bf16_ref):
    j = pl.program_id(0)
    gate_tile = COL_IF // INPROJ_TILE

    @pl.when(j < WPREP_CAST_STEPS)
    def _():
        wout_bf16_ref[...] = wout_ref[...].astype(_BF16)
        wkv_bf16_ref[...] = wkv_ref[...].astype(_BF16)

    @pl.when(j != gate_tile)
    def _():
        out_ref[...] = wt_ref[...].T.astype(_BF16)

    @pl.when(j == gate_tile)
    def _():
        rows = wt_ref[...]
        padded = jnp.concatenate(
            [rows[0:N_IF], jnp.zeros((WPREP_PAD, D_MODEL), _F32), rows[N_IF:INPROJ_TILE - WPREP_PAD]], axis=0)
        out_ref[...] = padded.T.astype(_BF16)


def _wprep_call(w_t, w_out, w_kv):
    steps = PROJ_COLS // INPROJ_TILE
    assert COL_IF % INPROJ_TILE == 0 and PROJ_COLS % INPROJ_TILE == 0 and N_IF % SUBLANES == 0
    assert steps >= WPREP_CAST_STEPS and w_out.shape[0] % WPREP_CAST_STEPS == 0 and w_kv.shape[0] % WPREP_CAST_STEPS == 0
    gate_tile = COL_IF // INPROJ_TILE
    first_row = lambda j: (j * (INPROJ_TILE // SUBLANES)
                           - (j > gate_tile).astype(jnp.int32) * (WPREP_PAD // SUBLANES)) * SUBLANES
    cast_blk = lambda j: (jnp.minimum(j, WPREP_CAST_STEPS - 1), 0)
    out_rows = w_out.shape[0] // WPREP_CAST_STEPS
    kv_rows = w_kv.shape[0] // WPREP_CAST_STEPS
    return pl.pallas_call(
        _wprep_kernel,
        grid=(steps,),
        in_specs=[
            pl.BlockSpec((pl.Element(INPROJ_TILE), pl.Element(D_MODEL)), lambda j: (first_row(j), 0)),
            pl.BlockSpec((out_rows, w_out.shape[1]), cast_blk),
            pl.BlockSpec((kv_rows, w_kv.shape[1]), cast_blk),
        ],
        out_specs=[
            pl.BlockSpec((D_MODEL, INPROJ_TILE), lambda j: (0, j)),
            pl.BlockSpec((out_rows, w_out.shape[1]), cast_blk),
            pl.BlockSpec((kv_rows, w_kv.shape[1]), cast_blk),
        ],
        out_shape=[
            jax.ShapeDtypeStruct((D_MODEL, PROJ_COLS), _BF16),
            jax.ShapeDtypeStruct(w_out.shape, _BF16),
            jax.ShapeDtypeStruct(w_kv.shape, _BF16),
        ],
        compiler_params=pltpu.CompilerParams(dimension_semantics=("arbitrary",)),
        name="weight_repack",
    )(w_t, w_out, w_kv)


def _memkv_kernel(mem_ref, g_ref, w_ref, mk_ref, mv_ref, mk_rows_ref, mv_rows_ref):
    xn = _rms(mem_ref[0], g_ref[...]).astype(_BF16)
    kv = _dot(xn, w_ref[...])
    mk_ref[0] = kv[:, :XA_WIDTH]
    mv_ref[0] = kv[:, XA_WIDTH:]
    for h in range(XA_HEADS):
        head_rows = pl.ds(h, N_MEM, stride=XA_HEADS)
        mk_rows_ref[0, head_rows, :] = kv[:, h * XA_HD:(h + 1) * XA_HD]
        mv_rows_ref[0, head_rows, :] = kv[:, XA_WIDTH + h * XA_HD:XA_WIDTH + (h + 1) * XA_HD]


def _memkv_call(mem, g_mem, w_kv):
    b = mem.shape[0]
    const = lambda i: (0, 0)
    per_batch = lambda i: (i, 0, 0)
    return pl.pallas_call(
        _memkv_kernel,
        grid=(b,),
        in_specs=[
            pl.BlockSpec((1, N_MEM, D_MODEL), per_batch),
            pl.BlockSpec((1, D_MODEL), const),
            pl.BlockSpec((D_MODEL, 2 * XA_WIDTH), const),
        ],
        out_specs=[
            pl.BlockSpec((1, N_MEM, XA_WIDTH), per_batch),
            pl.BlockSpec((1, N_MEM, XA_WIDTH), per_batch),
            pl.BlockSpec((1, XA_KV_ROWS, XA_HD), per_batch),
            pl.BlockSpec((1, XA_KV_ROWS, XA_HD), per_batch),
        ],
        out_shape=[jax.ShapeDtypeStruct((b, N_MEM, XA_WIDTH), _F32)] * 2
        + [jax.ShapeDtypeStruct((b, XA_KV_ROWS, XA_HD), _F32)] * 2,
        compiler_params=pltpu.CompilerParams(dimension_semantics=("arbitrary",)),
        name="memkv",
    )(mem, g_mem, w_kv)


def _finish_block(proj, refill, side, x, mk_ref, mv_ref, bif_ref, ghead_ref, convw_ref, convb_ref, wout_ref,
                  gfinal_ref, y_ref, c_ref, n_ref, m_ref, conv_ref, ext_ref, acc_ref):
    tb = PROMPT_BLOCK

    def out_part(y_bf16, row0):
        return _dot(y_bf16, wout_ref[row0:row0 + y_bf16.shape[1], :])

    gates = proj(COL_IF, LANES)
    slab = (gates + bif_ref[...]).T[0:SUBLANES, :]
    m_prev8 = m_ref[0]
    m_prev = jnp.concatenate([m_prev8] * (tb // LANES), axis=1)
    a, r, m_t, w_inter, e_negm, w_s, w_c = _gate_rows(slab, m_prev, tb)
    cols = _gate_cols(r, w_inter, e_negm, w_s, w_c)
    m_ref[0] = jnp.broadcast_to(m_t[:, tb - 1:tb], (SUBLANES, LANES))

    row = lax.broadcasted_iota(jnp.int32, (tb, tb), 0)
    col = lax.broadcasted_iota(jnp.int32, (tb, tb), 1)
    causal = col <= row

    def ml_bf16(off, scale=None):
        out = []
        for h in range(ML_HEADS):
            v = proj(h * ML_HEAD_COLS + off, ML_HD)
            out.append((v if scale is None else v * scale).astype(_BF16))
        return out

    q_bf16 = ml_bf16(OFF_Q)
    k_bf16 = ml_bf16(OFF_K, ML_HD ** -0.5)
    v_bf16 = ml_bf16(OFF_V)

    def head_first(h):
        qb = q_bf16[h]
        c_old = c_ref[0, h]
        qk = _dot_nt(qb, k_bf16[h])
        qc = _dot(qb, c_old.astype(_BF16))
        return qb.astype(_F32), k_bf16[h].astype(_F32), c_old, qk, qc

    def head_second(h, first):
        q, k, c_old, qk, qc = first
        vb = v_bf16[h]
        r_col = cols[:, GC_R + h:GC_R + h + 1]
        wi_col = cols[:, GC_WI + h:GC_WI + h + 1]
        en_col = cols[:, GC_EN + h:GC_EN + h + 1]
        ws_col = cols[:, GC_WS + h:GC_WS + h + 1]
        wc = w_c[h:h + 1, 0:1]
        s_mat = qk * jnp.exp(jnp.where(causal, a[h:h + 1, :] - r_col, -jnp.inf))
        kw = k * ws_col
        sv = _dot(s_mat.astype(_BF16), vb)
        c_ref[0, h] = wc * c_old + _dot_tn(kw.astype(_BF16), vb)
        n_old = n_ref[0, h:h + 1, :]
        n_ref[0, h:h + 1, :] = wc * n_old + jnp.sum(kw, axis=0, keepdims=True)
        num = wi_col * qc + sv
        den = wi_col * jnp.sum(q * n_old, axis=1, keepdims=True) + jnp.sum(s_mat, axis=1, keepdims=True)
        hh = num * (1.0 / jnp.maximum(jnp.abs(den), en_col))
        return jax.nn.sigmoid(proj(h * ML_HEAD_COLS + OFF_O, ML_HD)) * hh

    def head_out(h, gated):
        hs = h * ML_HD
        y_ml = _rms(gated, ghead_ref[:, hs:hs + ML_HD]) * _silu(proj(COL_ZML + hs, ML_HD))
        return out_part(y_ml.astype(_BF16), hs)

    def xa_scores(h):
        hs = h * XA_HD
        kh = mk_ref[0, :, hs:hs + XA_HD].astype(_BF16)
        return _dot_nt(proj(COL_QX + hs, XA_HD).astype(_BF16), kh) * (XA_HD ** -0.5)

    def xa_out(h, s):
        hs = h * XA_HD
        vh = mv_ref[0, :, hs:hs + XA_HD].astype(_BF16)
        e = jnp.exp(s - jnp.max(s, axis=1, keepdims=True))
        p = e * (1.0 / jnp.sum(e, axis=1, keepdims=True))
        y_xa = _dot(p.astype(_BF16), vh) * _silu(proj(COL_ZX + hs, XA_HD))
        return y_xa.astype(_BF16)

    side()
    s0, s1, s2, s3 = (xa_scores(h) for h in range(XA_HEADS))
    side()
    refill(0, ML_WIDTH)
    first0 = head_first(0)
    first1 = head_first(1)
    side()
    y_xa0 = xa_out(0, s0)
    y_xa1 = xa_out(1, s1)
    refill(OFF_K, ML_WIDTH)
    side()
    first2 = head_first(2)
    first3 = head_first(3)
    y_xa2 = xa_out(2, s2)
    y_xa3 = xa_out(3, s3)
    refill(OFF_V, ML_WIDTH)
    side()

    u = proj(COL_CG, CONV_CH) * proj(COL_XC, CONV_CH)
    ext_ref[SUBLANES:SUBLANES + tb, :] = u
    u1 = ext_ref[SUBLANES - 1:SUBLANES - 1 + tb, :]
    u2 = ext_ref[SUBLANES - 2:SUBLANES - 2 + tb, :]
    yc = convb_ref[...] + convw_ref[0:1, :] * u2 + convw_ref[1:2, :] * u1 + convw_ref[2:3, :] * u
    y_cv = proj(COL_BG, CONV_CH) * yc * _silu(proj(COL_ZCV, CONV_CH))
    tail = u[tb - (CONV_K - 1):tb, :]
    ext_ref[SUBLANES - (CONV_K - 1):SUBLANES, :] = tail
    conv_ref[0] = tail
    side()
    gated0 = head_second(0, first0)
    refill(COL_QX, 2 * XA_WIDTH)
    gated1 = head_second(1, first1)
    acc_ref[...] = out_part(jnp.concatenate([y_cv.astype(_BF16), y_xa0, y_xa1, y_xa2, y_xa3], axis=1), ML_WIDTH)
    refill(COL_IF, COL_XC - COL_IF)
    side()
    gated2 = head_second(2, first2)
    acc = head_out(0, gated0)
    refill(COL_XC, COL_QX - COL_XC)
    gated3 = head_second(3, first3)
    acc = acc + head_out(1, gated1)
    refill(OFF_O, ML_WIDTH)
    acc = acc + head_out(2, gated2)
    acc = acc + head_out(3, gated3)
    refill(COL_ZML, ML_WIDTH)

    y_ref[0] = _rms(acc_ref[...] + acc + x, gfinal_ref[...])


def _prompt_kernel(xnext_ref, x_ref, mk_ref, mv_ref, gnorm_ref, w_ref, bif_ref, ghead_ref, convw_ref, convb_ref,
                   wout_ref, gfinal_ref,
                   sproj_ref, scols_ref, ssv_ref, srs_ref, sc0_ref, sn0_ref, sconv0_ref, smk_ref, smv_ref,
                   y_ref, c_ref, n_ref, m_ref, conv_ref,
                   symix_ref, sc_ref, sn_ref, sconv_ref,
                   proj_ref, ext_ref, acc_ref, sext_ref, *, blocks_per_seq, sample_seqs):
    s = pl.program_id(0)

    @pl.when(s == 0)
    def _():
        proj_ref[...] = jnp.zeros_like(proj_ref)

    @pl.when((s == 0) | (lax.rem(s + blocks_per_seq - 1, blocks_per_seq) == 0))
    def _():
        c_ref[...] = jnp.zeros_like(c_ref)
        n_ref[...] = jnp.zeros_like(n_ref)
        m_ref[...] = jnp.zeros_like(m_ref)
        ext_ref[0:SUBLANES, :] = jnp.zeros((SUBLANES, CONV_CH), _F32)

    sample_stages = _sample_seqs(
        sample_seqs, sproj_ref, scols_ref, ssv_ref, srs_ref, sc0_ref, sn0_ref, sconv0_ref, smk_ref, smv_ref,
        ghead_ref, convw_ref, convb_ref, symix_ref, sc_ref, sn_ref, sconv_ref, sext_ref)

    xb_next = _rms(xnext_ref[0], gnorm_ref[...]).astype(_BF16)

    def side():
        next(sample_stages, None)

    def refill(col, width):
        n_pieces = max(width // REFILL_PIECE, 1)
        for p in range(n_pieces):
            c0 = col + p * REFILL_PIECE
            c1 = col + width if p == n_pieces - 1 else c0 + REFILL_PIECE
            proj_ref[:, c0:c1] = _dot(xb_next, w_ref[:, c0:c1])
            side()

    _finish_block(lambda col, width: proj_ref[:, col:col + width], refill, side,
                  x_ref[0], mk_ref, mv_ref, bif_ref, ghead_ref, convw_ref, convb_ref, wout_ref, gfinal_ref,
                  y_ref, c_ref, n_ref, m_ref, conv_ref, ext_ref, acc_ref)
    for _ in sample_stages:
        pass


def _prompt_call(x, mk, mv, g_norm, w_all, b_if, g_head, conv_w, conv_b, w_out, g_final,
                 s_proj, s_cols, s_sv, s_rs, s_c0, s_n0, s_conv0, s_mk, s_mv):
    b, t, _ = x.shape
    tb = PROMPT_BLOCK
    nt = t // tb
    n_blocks = b * nt
    n_seq = s_c0.shape[0]
    assert n_seq % n_blocks == 0
    ss = n_seq // n_blocks
    st = ss * DEC_SEQ_LEN
    grp = lambda s: jnp.minimum(s, n_blocks - 1)
    grp2 = lambda s: (grp(s), 0)
    grp3 = lambda s: (grp(s), 0, 0)
    grp4 = lambda s: (grp(s), 0, 0, 0)
    const2 = lambda s: (0, 0)
    resident = functools.partial(pl.BlockSpec, index_map=const2, pipeline_mode=pl.Buffered(1))
    nxt = lambda s: jnp.minimum(s, n_blocks - 1)
    cur = lambda s: jnp.maximum(s - 1, 0)
    cur_batch3 = lambda s: (cur(s) // nt, 0, 0)
    return pl.pallas_call(
        functools.partial(_prompt_kernel, blocks_per_seq=nt, sample_seqs=ss),
        grid=(n_blocks + 1,),
        in_specs=[
            pl.BlockSpec((1, tb, D_MODEL), lambda s: (nxt(s) // nt, nxt(s) % nt, 0)),
            pl.BlockSpec((1, tb, D_MODEL), lambda s: (cur(s) // nt, cur(s) % nt, 0)),
            pl.BlockSpec((1, N_MEM, XA_WIDTH), cur_batch3),
            pl.BlockSpec((1, N_MEM, XA_WIDTH), cur_batch3),
            pl.BlockSpec((1, D_MODEL), const2),
            resident((D_MODEL, PROJ_COLS)),
            pl.BlockSpec((1, LANES), const2),
            pl.BlockSpec((1, ML_WIDTH), const2),
            pl.BlockSpec((CONV_K, CONV_CH), const2),
            pl.BlockSpec((1, CONV_CH), const2),
            resident((MIX_WIDTH, D_MODEL)),
            pl.BlockSpec((1, D_MODEL), const2),
            pl.BlockSpec((st, PROJ_COLS), grp2),
            pl.BlockSpec((st, LANES), grp2),
            pl.BlockSpec((st, ML_WIDTH), grp2),
            pl.BlockSpec((st, LANES), grp2),
            pl.BlockSpec((ss, ML_HEADS, ML_HD, ML_HD), grp4),
            pl.BlockSpec((ss, ML_HEADS, ML_HD), grp3),
            pl.BlockSpec((ss, CONV_K - 1, CONV_CH), grp3),
            pl.BlockSpec((ss, XA_KV_ROWS, XA_HD), grp3),
            pl.BlockSpec((ss, XA_KV_ROWS, XA_HD), grp3),
        ],
        out_specs=[
            pl.BlockSpec((1, tb, D_MODEL), lambda s: (cur(s) // nt, cur(s) % nt, 0)),
            pl.BlockSpec((1, ML_HEADS, ML_HD, ML_HD), lambda s: (cur(s) // nt, 0, 0, 0)),
            pl.BlockSpec((1, ML_HEADS, ML_HD), cur_batch3),
            pl.BlockSpec((1, SUBLANES, LANES), cur_batch3),
            pl.BlockSpec((1, CONV_K - 1, CONV_CH), cur_batch3),
            pl.BlockSpec((st, MIX_WIDTH), grp2),
            pl.BlockSpec((ss, ML_HEADS, ML_HD, ML_HD), grp4),
            pl.BlockSpec((ss, ML_HEADS, ML_HD), grp3),
            pl.BlockSpec((ss, CONV_K - 1, CONV_CH), grp3),
        ],
        out_shape=[
            jax.ShapeDtypeStruct((b, t, D_MODEL), _F32),
            jax.ShapeDtypeStruct((b, ML_HEADS, ML_HD, ML_HD), _F32),
            jax.ShapeDtypeStruct((b, ML_HEADS, ML_HD), _F32),
            jax.ShapeDtypeStruct((b, SUBLANES, LANES), _F32),
            jax.ShapeDtypeStruct((b, CONV_K - 1, CONV_CH), _F32),
            jax.ShapeDtypeStruct((n_seq * DEC_SEQ_LEN, MIX_WIDTH), _F32),
            jax.ShapeDtypeStruct(s_c0.shape, _F32),
            jax.ShapeDtypeStruct(s_n0.shape, _F32),
            jax.ShapeDtypeStruct(s_conv0.shape, _F32),
        ],
        scratch_shapes=[
            pltpu.VMEM((tb, PROJ_COLS), _F32),
            pltpu.VMEM((SUBLANES + tb, CONV_CH), _F32),
            pltpu.VMEM((tb, D_MODEL), _F32),
            pltpu.VMEM((ss, 2 * SUBLANES, CONV_CH), _F32),
        ],
        compiler_params=pltpu.CompilerParams(
            dimension_semantics=("arbitrary",), vmem_limit_bytes=VMEM_LIMIT),
        name="prompt_layer",
    )(x, x, mk, mv, g_norm, w_all, b_if, g_head, conv_w, conv_b, w_out, g_final,
      s_proj, s_cols, s_sv, s_rs, s_c0, s_n0, s_conv0, s_mk, s_mv)


def _sample_pre_kernel(x_ref, gnorm_ref, w_ref, bif_ref, m0_ref, proj_ref, cols_ref, sv_ref, rs_ref, mrows_ref):
    nt = x_ref.shape[0]
    proj_ref[...] = _dot(_rms(x_ref[...], gnorm_ref[...]).astype(_BF16), w_ref[...])
    slab = (proj_ref[:, COL_IF:COL_IF + LANES] + bif_ref[...]).T[0:SUBLANES, :]
    a, r, m_t, w_inter, e_negm, w_s, w_c = _gate_rows(slab, m0_ref[...], DEC_SEQ_LEN)
    cols = _gate_cols(r, w_inter, e_negm, w_s, w_c)
    cols_ref[...] = cols
    mrows_ref[...] = m_t
    row = lax.broadcasted_iota(jnp.int32, (nt, nt), 0)
    col = lax.broadcasted_iota(jnp.int32, (nt, nt), 1)
    mask = (col <= row) & ((row // DEC_SEQ_LEN) == (col // DEC_SEQ_LEN))
    rs_ref[...] = jnp.zeros_like(rs_ref)
    for h in range(ML_HEADS):
        hs = h * ML_HD
        hc = h * ML_HEAD_COLS
        qb = proj_ref[:, hc + OFF_Q:hc + OFF_Q + ML_HD].astype(_BF16)
        kb = (proj_ref[:, hc + OFF_K:hc + OFF_K + ML_HD] * (ML_HD ** -0.5)).astype(_BF16)
        vb = proj_ref[:, hc + OFF_V:hc + OFF_V + ML_HD].astype(_BF16)
        decay = jnp.exp(jnp.where(mask, a[h:h + 1, :] - cols[:, GC_R + h:GC_R + h + 1], -jnp.inf))
        s_mat = _dot_nt(qb, kb) * decay
        sv_ref[:, hs:hs + ML_HD] = _dot(s_mat.astype(_BF16), vb)
        rs_ref[:, h:h + 1] = jnp.sum(s_mat, axis=1, keepdims=True)


def _sample_pre_call(x, g_norm, w_all, b_if, m0_rows):
    n_tok = x.shape[0]
    nt = SAMPLE_BLOCK_SEQS * DEC_SEQ_LEN
    const = lambda i: (0, 0)
    blk = lambda i: (i, 0)
    rows_blk = lambda i: (0, i)
    return pl.pallas_call(
        _sample_pre_kernel,
        grid=(n_tok // nt,),
        in_specs=[
            pl.BlockSpec((nt, D_MODEL), blk),
            pl.BlockSpec((1, D_MODEL), const),
            pl.BlockSpec((D_MODEL, PROJ_COLS), const, pipeline_mode=pl.Buffered(1)),
            pl.BlockSpec((1, LANES), const),
            pl.BlockSpec((SUBLANES, nt), rows_blk),
        ],
        out_specs=[
            pl.BlockSpec((nt, PROJ_COLS), blk),
            pl.BlockSpec((nt, LANES), blk),
            pl.BlockSpec((nt, ML_WIDTH), blk),
            pl.BlockSpec((nt, LANES), blk),
            pl.BlockSpec((SUBLANES, nt), rows_blk),
        ],
        out_shape=[
            jax.ShapeDtypeStruct((n_tok, PROJ_COLS), _F32),
            jax.ShapeDtypeStruct((n_tok, LANES), _F32),
            jax.ShapeDtypeStruct((n_tok, ML_WIDTH), _F32),
            jax.ShapeDtypeStruct((n_tok, LANES), _F32),
            jax.ShapeDtypeStruct((SUBLANES, n_tok), _F32),
        ],
        compiler_params=pltpu.CompilerParams(
            dimension_semantics=("arbitrary",), vmem_limit_bytes=VMEM_LIMIT),
        name="sample_pre",
    )(x, g_norm, w_all, b_if, m0_rows)


def _sample_seqs(n_seqs, proj_ref, cols_ref, sv_ref, rs_ref, c0_ref, n0_ref, conv0_ref, mk_ref, mv_ref,
                 ghead_ref, convw_ref, convb_ref, ymix_ref, c_ref, n_ref, conv_ref, ext_ref):
    seqs = range(n_seqs)
    pairs = [(sl, h) for sl in seqs for h in range(ML_HEADS)]
    rows_of = [slice(sl * DEC_SEQ_LEN, (sl + 1) * DEC_SEQ_LEN) for sl in seqs]
    xa_rows = XA_HEADS * DEC_SEQ_LEN
    own_head = (lax.broadcasted_iota(jnp.int32, (xa_rows, XA_KV_ROWS), 0) // DEC_SEQ_LEN
                == lax.broadcasted_iota(jnp.int32, (xa_rows, XA_KV_ROWS), 1) % XA_HEADS)

    def ml(sl, h, off):
        return proj_ref[rows_of[sl], h * ML_HEAD_COLS + off:h * ML_HEAD_COLS + off + ML_HD]

    q4 = [jnp.concatenate(
        [proj_ref[rows_of[sl], COL_QX + h * XA_HD:COL_QX + (h + 1) * XA_HD] for h in range(XA_HEADS)], axis=0
    ).astype(_BF16) for sl in seqs]
    k_bf16 = [mk_ref[sl].astype(_BF16) for sl in seqs]
    yield

    scores = []
    for sl in seqs:
        halves = []
        for half in range(2):
            kv_rows = slice(half * XA_KV_ROWS // 2, (half + 1) * XA_KV_ROWS // 2)
            halves.append(_dot_nt(q4[sl], k_bf16[sl][kv_rows, :]))
            yield
        scores.append(jnp.where(own_head, jnp.concatenate(halves, axis=1) * (XA_HD ** -0.5), -jnp.inf))
    q_bf16 = {(sl, h): ml(sl, h, OFF_Q).astype(_BF16) for sl, h in pairs}
    c_bf16 = {(sl, h): c0_ref[sl, h].astype(_BF16) for sl, h in pairs}

    qc = {}
    for sl, h in pairs:
        qc[sl, h] = _dot(q_bf16[sl, h], c_bf16[sl, h])
        if h % 2 == 1:
            yield
    probs = []
    for sl in seqs:
        e = jnp.exp(scores[sl] - jnp.max(scores[sl], axis=1, keepdims=True))
        probs.append((e * (1.0 / jnp.sum(e, axis=1, keepdims=True))).astype(_BF16))
    v_bf16 = [mv_ref[sl].astype(_BF16) for sl in seqs]

    y4 = []
    for sl in seqs:
        y4.append(_dot(probs[sl], v_bf16[sl]))
        yield
    kw, kw_bf16, vml_bf16, wc = {}, {}, {}, {}
    for sl, h in pairs:
        gc = cols_ref[rows_of[sl], :]
        wc[sl, h] = gc[0:1, GC_WC + h:GC_WC + h + 1]
        kw[sl, h] = ml(sl, h, OFF_K) * (ML_HD ** -0.5) * gc[:, GC_WS + h:GC_WS + h + 1]
        kw_bf16[sl, h] = kw[sl, h].astype(_BF16)
        vml_bf16[sl, h] = ml(sl, h, OFF_V).astype(_BF16)

    for sl, h in pairs:
        c_ref[sl, h] = wc[sl, h] * c0_ref[sl, h] + _dot_tn(kw_bf16[sl, h], vml_bf16[sl, h])
        n_ref[sl, h:h + 1, :] = wc[sl, h] * n0_ref[sl, h:h + 1, :] + jnp.sum(kw[sl, h], axis=0, keepdims=True)
        if h % 2 == 1:
            yield

    gc_all = cols_ref[...]
    for h in range(ML_HEADS):
        hs = h * ML_HD
        hc = h * ML_HEAD_COLS
        wi_col = gc_all[:, GC_WI + h:GC_WI + h + 1]
        en_col = gc_all[:, GC_EN + h:GC_EN + h + 1]
        qc_all = jnp.concatenate([qc[sl, h] for sl in seqs], axis=0)
        n_all = jnp.concatenate(
            [jnp.broadcast_to(n0_ref[sl, h:h + 1, :], (DEC_SEQ_LEN, ML_HD)) for sl in seqs], axis=0)
        num = wi_col * qc_all + sv_ref[:, hs:hs + ML_HD]
        den = (wi_col * jnp.sum(proj_ref[:, hc + OFF_Q:hc + OFF_Q + ML_HD] * n_all, axis=1, keepdims=True)
               + rs_ref[:, h:h + 1])
        hh = num * (1.0 / jnp.maximum(jnp.abs(den), en_col))
        ymix_ref[:, hs:hs + ML_HD] = _ml_head_out(
            hh, proj_ref[:, hc + OFF_O:hc + OFF_O + ML_HD], proj_ref[:, COL_ZML + hs:COL_ZML + hs + ML_HD],
            ghead_ref[:, hs:hs + ML_HD])

    for sl in seqs:
        rows = rows_of[sl]
        u = proj_ref[rows, COL_CG:COL_CG + CONV_CH] * proj_ref[rows, COL_XC:COL_XC + CONV_CH]
        ext_ref[sl, SUBLANES - (CONV_K - 1):SUBLANES, :] = conv0_ref[sl]
        ext_ref[sl, SUBLANES:2 * SUBLANES, :] = u
        u1 = ext_ref[sl, SUBLANES - 1:2 * SUBLANES - 1, :]
        u2 = ext_ref[sl, SUBLANES - 2:2 * SUBLANES - 2, :]
        yc = convb_ref[...] + convw_ref[0:1, :] * u2 + convw_ref[1:2, :] * u1 + convw_ref[2:3, :] * u
        ymix_ref[rows, ML_WIDTH:ML_WIDTH + CONV_CH] = (
            proj_ref[rows, COL_BG:COL_BG + CONV_CH] * yc * _silu(proj_ref[rows, COL_ZCV:COL_ZCV + CONV_CH]))
        conv_ref[sl] = u[DEC_SEQ_LEN - (CONV_K - 1):DEC_SEQ_LEN, :]
        ymix_ref[rows, ML_WIDTH + CONV_CH:MIX_WIDTH] = jnp.concatenate(
            [y4[sl][h * DEC_SEQ_LEN:(h + 1) * DEC_SEQ_LEN, :] for h in range(XA_HEADS)], axis=1
        ) * _silu(proj_ref[rows, COL_ZX:COL_ZX + XA_WIDTH])


def _sample_post_kernel(ymix_ref, x_ref, wout_ref, gfinal_ref, y_ref):
    y_ref[...] = _out_tail(ymix_ref[...].astype(_BF16), wout_ref[...], x_ref[...], gfinal_ref[...])


def _sample_post_call(ymix, x, w_out, g_final):
    n_tok = x.shape[0]
    rb = 256
    const = lambda i: (0, 0)
    blk = lambda i: (i, 0)
    return pl.pallas_call(
        _sample_post_kernel,
        grid=(n_tok // rb,),
        in_specs=[
            pl.BlockSpec((rb, MIX_WIDTH), blk),
            pl.BlockSpec((rb, D_MODEL), blk),
            pl.BlockSpec((MIX_WIDTH, D_MODEL), const),
            pl.BlockSpec((1, D_MODEL), const),
        ],
        out_specs=pl.BlockSpec((rb, D_MODEL), blk),
        out_shape=jax.ShapeDtypeStruct((n_tok, D_MODEL), _F32),
        compiler_params=pltpu.CompilerParams(dimension_semantics=("arbitrary",)),
        name="sample_post",
    )(ymix, x, w_out, g_final)


def kernel(x_prompt, x_sample, state_mlstm_C, state_mlstm_n, state_mlstm_m, state_conv, cache_mem_k, cache_mem_v, mem_prompt, g_norm, w_in, b_if, g_head, conv_w, conv_b, g_mem, w_mem_kv, w_out, g_final):
    assert w_in.shape[0] == 1, "single-layer trunk"
    bp, tp, _ = x_prompt.shape
    bs, ts, _ = x_sample.shape
    assert ts == DEC_SEQ_LEN and tp % PROMPT_BLOCK == 0 and bs % SAMPLE_BLOCK_SEQS == 0

    w_all, w_out_b, w_kv_b = _wprep_call(jnp.swapaxes(w_in[0], 0, 1), w_out[0], w_mem_kv[0])
    b_if_p = jnp.pad(b_if[0][None, :], ((0, 0), (0, LANES - N_IF)))
    row = lambda v: v.reshape(1, -1)

    n_tok = bs * ts
    xs = x_sample.reshape(n_tok, D_MODEL)
    m0_rows = jnp.pad(jnp.repeat(state_mlstm_m[0].T, ts, axis=1), ((0, SUBLANES - ML_HEADS), (0, 0)))
    s_proj, s_cols, s_sv, s_rs, m_rows = _sample_pre_call(xs, row(g_norm[0]), w_all, b_if_p, m0_rows)

    mk_p, mv_p, mk_rows, mv_rows = _memkv_call(mem_prompt, row(g_mem[0]), w_kv_b)
    y_p, c_p, n_p, m_p, conv_p, s_ymix, c_s, n_s, conv_s = _prompt_call(
        x_prompt, mk_p, mv_p, row(g_norm[0]), w_all, b_if_p, row(g_head[0]), conv_w[0], row(conv_b[0]),
        w_out_b, row(g_final),
        s_proj, s_cols, s_sv, s_rs, state_mlstm_C[0], state_mlstm_n[0], state_conv[0],
        cache_mem_k[0].reshape(bs, XA_KV_ROWS, XA_HD), cache_mem_v[0].reshape(bs, XA_KV_ROWS, XA_HD))

    y_s = _sample_post_call(s_ymix, xs, w_out_b, row(g_final))
    m_s = m_rows[:ML_HEADS, ts - 1::ts].T

    kv_shape = (1, bp, N_MEM, XA_HEADS, XA_HD)
    return (y_p, y_s.reshape(bs, ts, D_MODEL),
            c_p[None], n_p[None], m_p[:, :ML_HEADS, 0][None], conv_p[None],
            mk_rows.reshape(kv_shape), mv_rows.reshape(kv_shape),
            c_s[None], n_s[None], m_s[None], conv_s[None])
```

```python
import functools

import jax
import jax.numpy as jnp
from jax import lax
from jax.experimental import pallas as pl
from jax.experimental.pallas import tpu as pltpu

D_MODEL = 1024
ML_HEADS = 4
ML_HD = 256
ML_WIDTH = ML_HEADS * ML_HD
CONV_CH = 512
CONV_K = 3
XA_HEADS = 4
XA_HD = 128
XA_WIDTH = XA_HEADS * XA_HD
N_MEM = 256
XA_KV_ROWS = N_MEM * XA_HEADS
DEC_SEQ_LEN = 8
MIX_WIDTH = ML_WIDTH + CONV_CH + XA_WIDTH
EPS = 1e-6

SUBLANES = 8
LANES = 128

ML_HEAD_COLS = ML_HD
OFF_Q = 0
OFF_K = ML_WIDTH
OFF_V = 2 * ML_WIDTH
OFF_O = 3 * ML_WIDTH
COL_ZML = 4 * ML_WIDTH
COL_IF = 5 * ML_WIDTH
COL_BG = COL_IF + LANES
COL_CG = COL_BG + CONV_CH
COL_XC = COL_CG + CONV_CH
COL_ZCV = COL_XC + CONV_CH
COL_QX = COL_ZCV + CONV_CH
COL_ZX = COL_QX + XA_WIDTH
PROJ_COLS = COL_ZX + XA_WIDTH
N_IF = 2 * ML_HEADS

GC_R = 0 * SUBLANES
GC_WI = 1 * SUBLANES
GC_EN = 2 * SUBLANES
GC_WS = 3 * SUBLANES
GC_WC = 4 * SUBLANES
GC_USED = 5 * SUBLANES

PROMPT_BLOCK = 256
REFILL_PIECE = 512
SAMPLE_BLOCK_SEQS = 16
INPROJ_TILE = 640
VMEM_LIMIT = 63 * 1024 * 1024

_F32 = jnp.float32
_BF16 = jnp.bfloat16


def _dot(a, b):
    return jnp.dot(a, b, preferred_element_type=_F32)


def _dot_nt(a, b):
    return lax.dot_general(a, b, (((1,), (1,)), ((), ())), preferred_element_type=_F32)


def _dot_tn(a, b):
    return lax.dot_general(a, b, (((0,), (0,)), ((), ())), preferred_element_type=_F32)


def _rms(x, g):
    r = lax.rsqrt(jnp.mean(x * x, axis=-1, keepdims=True) + EPS)
    return (x * r) * g


def _silu(x):
    return x * jax.nn.sigmoid(x)


def _log_sigmoid(x):
    return jnp.minimum(x, 0.0) - jnp.log1p(jnp.exp(-jnp.abs(x)))


def _gate_rows(slab, m_prev, seg):
    n = slab.shape[1]
    pos = lax.broadcasted_iota(jnp.int32, slab.shape, 1) & (seg - 1)

    def scan(x, op, fill, reverse=False):
        k = 1
        while k < seg:
            if reverse:
                shifted, ok = pltpu.roll(x, n - k, axis=1), pos < seg - k
            else:
                shifted, ok = pltpu.roll(x, k, axis=1), pos >= k
            x = op(x, jnp.where(ok, shifted, fill))
            k *= 2
        return x

    logf = _log_sigmoid(pltpu.roll(slab, ML_HEADS, axis=0))
    f_cum = scan(logf, jnp.add, 0.0)
    a = slab - f_cum
    r = jnp.maximum(scan(a, jnp.maximum, -jnp.inf), m_prev)
    r_last = scan(r, jnp.maximum, -jnp.inf, reverse=True)
    m_t = f_cum + r
    w_inter = jnp.exp(m_prev - r)
    e_negm = jnp.exp(-m_t)
    w_s = jnp.exp(a - r_last)
    w_c = jnp.exp(m_prev - r_last)
    return a, r, m_t, w_inter, e_negm, w_s, w_c


def _gate_cols(r, w_inter, e_negm, w_s, w_c):
    n = r.shape[1]
    pad = jnp.zeros((LANES - GC_USED, n), _F32)
    return jnp.concatenate([r, w_inter, e_negm, w_s, w_c, pad], axis=0).T


def _ml_head_out(hh, o_pre, z, g_head):
    hh = jax.nn.sigmoid(o_pre) * hh
    return _rms(hh, g_head) * _silu(z)


def _out_tail(ymix_bf16, w_out, x, g_final):
    y = _dot(ymix_bf16, w_out) + x
    return _rms(y, g_final)


WPREP_PAD = LANES - N_IF


WPREP_CAST_STEPS = 8


def _wprep_kernel(wt_ref, wout_ref, wkv_ref, out_ref, wout_bf16_ref, wkv_bf16_ref):
    j = pl.program_id(0)
    gate_tile = COL_IF // INPROJ_TILE

    @pl.when(j < WPREP_CAST_STEPS)
    def _():
        wout_bf16_ref[...] = wout_ref[...].astype(_BF16)
        wkv_bf16_ref[...] = wkv_ref[...].astype(_BF16)

    @pl.when(j != gate_tile)
    def _():
        out_ref[...] = wt_ref[...].T.astype(_BF16)

    @pl.when(j == gate_tile)
    def _():
        rows = wt_ref[...]
        padded = jnp.concatenate(
            [rows[0:N_IF], jnp.zeros((WPREP_PAD, D_MODEL), _F32), rows[N_IF:INPROJ_TILE - WPREP_PAD]], axis=0)
        out_ref[...] = padded.T.astype(_BF16)


def _wprep_call(w_t, w_out, w_kv):
    steps = PROJ_COLS // INPROJ_TILE
    assert COL_IF % INPROJ_TILE == 0 and PROJ_COLS % INPROJ_TILE == 0 and N_IF % SUBLANES == 0
    assert steps >= WPREP_CAST_STEPS and w_out.shape[0] % WPREP_CAST_STEPS == 0 and w_kv.shape[0] % WPREP_CAST_STEPS == 0
    gate_tile = COL_IF // INPROJ_TILE
    first_row = lambda j: (j * (INPROJ_TILE // SUBLANES)
                           - (j > gate_tile).astype(jnp.int32) * (WPREP_PAD // SUBLANES)) * SUBLANES
    cast_blk = lambda j: (jnp.minimum(j, WPREP_CAST_STEPS - 1), 0)
    out_rows = w_out.shape[0] // WPREP_CAST_STEPS
    kv_rows = w_kv.shape[0] // WPREP_CAST_STEPS
    return pl.pallas_call(
        _wprep_kernel,
        grid=(steps,),
        in_specs=[
            pl.BlockSpec((pl.Element(INPROJ_TILE), pl.Element(D_MODEL)), lambda j: (first_row(j), 0)),
            pl.BlockSpec((out_rows, w_out.shape[1]), cast_blk),
            pl.BlockSpec((kv_rows, w_kv.shape[1]), cast_blk),
        ],
        out_specs=[
            pl.BlockSpec((D_MODEL, INPROJ_TILE), lambda j: (0, j)),
            pl.BlockSpec((out_rows, w_out.shape[1]), cast_blk),
            pl.BlockSpec((kv_rows, w_kv.shape[1]), cast_blk),
        ],
        out_shape=[
            jax.ShapeDtypeStruct((D_MODEL, PROJ_COLS), _BF16),
            jax.ShapeDtypeStruct(w_out.shape, _BF16),
            jax.ShapeDtypeStruct(w_kv.shape, _BF16),
        ],
        compiler_params=pltpu.CompilerParams(dimension_semantics=("arbitrary",)),
        name="weight_repack",
    )(w_t, w_out, w_kv)


def _memkv_kernel(mem_ref, g_ref, w_ref, mk_ref, mv_ref, mk_rows_ref, mv_rows_ref):
    xn = _rms(mem_ref[0], g_ref[...]).astype(_BF16)
    kv = _dot(xn, w_ref[...])
    mk_ref[0] = kv[:, :XA_WIDTH]
    mv_ref[0] = kv[:, XA_WIDTH:]
    for h in range(XA_HEADS):
        head_rows = pl.ds(h, N_MEM, stride=XA_HEADS)
        mk_rows_ref[0, head_rows, :] = kv[:, h * XA_HD:(h + 1) * XA_HD]
        mv_rows_ref[0, head_rows, :] = kv[:, XA_WIDTH + h * XA_HD:XA_WIDTH + (h + 1) * XA_HD]


def _finish_block(proj, refill, side, x, mk_ref, mv_ref, bif_ref, ghead_ref, convw_ref, convb_ref, wout_ref,
                  gfinal_ref, y_ref, c_ref, n_ref, m_ref, conv_ref, ext_ref, acc_ref):
    tb = PROMPT_BLOCK

    def out_part(y_bf16, row0):
        return _dot(y_bf16, wout_ref[row0:row0 + y_bf16.shape[1], :])

    gates = proj(COL_IF, LANES)
    slab = (gates + bif_ref[...]).T[0:SUBLANES, :]
    m_prev8 = m_ref[0]
    m_prev = jnp.concatenate([m_prev8] * (tb // LANES), axis=1)
    a, r, m_t, w_inter, e_negm, w_s, w_c = _gate_rows(slab, m_prev, tb)
    cols = _gate_cols(r, w_inter, e_negm, w_s, w_c)
    m_ref[0] = jnp.broadcast_to(m_t[:, tb - 1:tb], (SUBLANES, LANES))

    row = lax.broadcasted_iota(jnp.int32, (tb, tb), 0)
    col = lax.broadcasted_iota(jnp.int32, (tb, tb), 1)
    causal = col <= row

    def ml_bf16(off, scale=None):
        out = []
        for h in range(ML_HEADS):
            v = proj(h * ML_HEAD_COLS + off, ML_HD)
            out.append((v if scale is None else v * scale).astype(_BF16))
        return out

    q_bf16 = ml_bf16(OFF_Q)
    k_bf16 = ml_bf16(OFF_K, ML_HD ** -0.5)
    v_bf16 = ml_bf16(OFF_V)

    def head_first(h):
        qb = q_bf16[h]
        c_old = c_ref[0, h]
        qk = _dot_nt(qb, k_bf16[h])
        qc = _dot(qb, c_old.astype(_BF16))
        return qb.astype(_F32), k_bf16[h].astype(_F32), c_old, qk, qc

    def head_second(h, first):
        q, k, c_old, qk, qc = first
        vb = v_bf16[h]
        r_col = cols[:, GC_R + h:GC_R + h + 1]
        wi_col = cols[:, GC_WI + h:GC_WI + h + 1]
        en_col = cols[:, GC_EN + h:GC_EN + h + 1]
        ws_col = cols[:, GC_WS + h:GC_WS + h + 1]
        wc = w_c[h:h + 1, 0:1]
        s_mat = qk * jnp.exp(jnp.where(causal, a[h:h + 1, :] - r_col, -jnp.inf))
        kw = k * ws_col
        sv = _dot(s_mat.astype(_BF16), vb)
        c_ref[0, h] = wc * c_old + _dot_tn(kw.astype(_BF16), vb)
        n_old = n_ref[0, h:h + 1, :]
        n_ref[0, h:h + 1, :] = wc * n_old + jnp.sum(kw, axis=0, keepdims=True)
        num = wi_col * qc + sv
        den = wi_col * jnp.sum(q * n_old, axis=1, keepdims=True) + jnp.sum(s_mat, axis=1, keepdims=True)
        hh = num * (1.0 / jnp.maximum(jnp.abs(den), en_col))
        return jax.nn.sigmoid(proj(h * ML_HEAD_COLS + OFF_O, ML_HD)) * hh

    def head_out(h, gated):
        hs = h * ML_HD
        y_ml = _rms(gated, ghead_ref[:, hs:hs + ML_HD]) * _silu(proj(COL_ZML + hs, ML_HD))
        return out_part(y_ml.astype(_BF16), hs)

    def xa_scores(h):
        hs = h * XA_HD
        kh = mk_ref[0, :, hs:hs + XA_HD].astype(_BF16)
        return _dot_nt(proj(COL_QX + hs, XA_HD).astype(_BF16), kh) * (XA_HD ** -0.5)

    def xa_out(h, s):
        hs = h * XA_HD
        vh = mv_ref[0, :, hs:hs + XA_HD].astype(_BF16)
        e = jnp.exp(s - jnp.max(s, axis=1, keepdims=True))
        p = e * (1.0 / jnp.sum(e, axis=1, keepdims=True))
        y_xa = _dot(p.astype(_BF16), vh) * _silu(proj(COL_ZX + hs, XA_HD))
        return y_xa.astype(_BF16)

    side()
    s0, s1, s2, s3 = (xa_scores(h) for h in range(XA_HEADS))
    side()
    refill(0, ML_WIDTH)
    first0 = head_first(0)
    first1 = head_first(1)
    side()
    y_xa0 = xa_out(0, s0)
    y_xa1 = xa_out(1, s1)
    refill(OFF_K, ML_WIDTH)
    side()
    first2 = head_first(2)
    first3 = head_first(3)
    y_xa2 = xa_out(2, s2)
    y_xa3 = xa_out(3, s3)
    refill(OFF_V, ML_WIDTH)
    side()

    u = proj(COL_CG, CONV_CH) * proj(COL_XC, CONV_CH)
    ext_ref[SUBLANES:SUBLANES + tb, :] = u
    u1 = ext_ref[SUBLANES - 1:SUBLANES - 1 + tb, :]
    u2 = ext_ref[SUBLANES - 2:SUBLANES - 2 + tb, :]
    yc = convb_ref[...] + convw_ref[0:1, :] * u2 + convw_ref[1:2, :] * u1 + convw_ref[2:3, :] * u
    y_cv = proj(COL_BG, CONV_CH) * yc * _silu(proj(COL_ZCV, CONV_CH))
    tail = u[tb - (CONV_K - 1):tb, :]
    ext_ref[SUBLANES - (CONV_K - 1):SUBLANES, :] = tail
    conv_ref[0] = tail
    side()
    gated0 = head_second(0, first0)
    refill(COL_QX, 2 * XA_WIDTH)
    gated1 = head_second(1, first1)
    acc_ref[...] = out_part(jnp.concatenate([y_cv.astype(_BF16), y_xa0, y_xa1, y_xa2, y_xa3], axis=1), ML_WIDTH)
    refill(COL_IF, COL_XC - COL_IF)
    side()
    gated2 = head_second(2, first2)
    acc = head_out(0, gated0)
    refill(COL_XC, COL_QX - COL_XC)
    gated3 = head_second(3, first3)
    acc = acc + head_out(1, gated1)
    refill(OFF_O, ML_WIDTH)
    acc = acc + head_out(2, gated2)
    acc = acc + head_out(3, gated3)
    refill(COL_ZML, ML_WIDTH)

    y_ref[0] = _rms(acc_ref[...] + acc + x, gfinal_ref[...])


def _prompt_kernel(xnext_ref, x_ref, mk_ref, mv_ref, gnorm_ref, w_ref, bif_ref, ghead_ref, convw_ref, convb_ref,
                   wout_ref, gfinal_ref,
                   sproj_ref, scols_ref, ssv_ref, srs_ref, sc0_ref, sn0_ref, sconv0_ref, smk_ref, smv_ref,
                   y_ref, c_ref, n_ref, m_ref, conv_ref,
                   symix_ref, sc_ref, sn_ref, sconv_ref,
                   proj_ref, ext_ref, acc_ref, sext_ref, *, blocks_per_seq, sample_seqs):
    s = pl.program_id(0)

    @pl.when(s == 0)
    def _():
        proj_ref[...] = jnp.zeros_like(proj_ref)

    @pl.when((s == 0) | (lax.rem(s + blocks_per_seq - 1, blocks_per_seq) == 0))
    def _():
        c_ref[...] = jnp.zeros_like(c_ref)
        n_ref[...] = jnp.zeros_like(n_ref)
        m_ref[...] = jnp.zeros_like(m_ref)
        ext_ref[0:SUBLANES, :] = jnp.zeros((SUBLANES, CONV_CH), _F32)

    sample_stages = _sample_seqs(
        sample_seqs, sproj_ref, scols_ref, ssv_ref, srs_ref, sc0_ref, sn0_ref, sconv0_ref, smk_ref, smv_ref,
        ghead_ref, convw_ref, convb_ref, symix_ref, sc_ref, sn_ref, sconv_ref, sext_ref)

    xb_next = _rms(xnext_ref[0], gnorm_ref[...]).astype(_BF16)

    def side():
        next(sample_stages, None)

    def refill(col, width):
        n_pieces = max(width // REFILL_PIECE, 1)
        for p in range(n_pieces):
            c0 = col + p * REFILL_PIECE
            c1 = col + width if p == n_pieces - 1 else c0 + REFILL_PIECE
            proj_ref[:, c0:c1] = _dot(xb_next, w_ref[:, c0:c1])
            side()

    _finish_block(lambda col, width: proj_ref[:, col:col + width], refill, side,
                  x_ref[0], mk_ref, mv_ref, bif_ref, ghead_ref, convw_ref, convb_ref, wout_ref, gfinal_ref,
                  y_ref, c_ref, n_ref, m_ref, conv_ref, ext_ref, acc_ref)
    for _ in sample_stages:
        pass


def _prompt_call(x, mk, mv, g_norm, w_all, b_if, g_head, conv_w, conv_b, w_out, g_final,
                 s_proj, s_cols, s_sv, s_rs, s_c0, s_n0, s_conv0, s_mk, s_mv):
    b, t, _ = x.shape
    tb = PROMPT_BLOCK
    nt = t // tb
    n_blocks = b * nt
    n_seq = s_c0.shape[0]
    assert n_seq % n_blocks == 0
    ss = n_seq // n_blocks
    st = ss * DEC_SEQ_LEN
    grp = lambda s: jnp.minimum(s, n_blocks - 1)
    grp2 = lambda s: (grp(s), 0)
    grp3 = lambda s: (grp(s), 0, 0)
    grp4 = lambda s: (grp(s), 0, 0, 0)
    const2 = lambda s: (0, 0)
    resident = functools.partial(pl.BlockSpec, index_map=const2, pipeline_mode=pl.Buffered(1))
    nxt = lambda s: jnp.minimum(s, n_blocks - 1)
    cur = lambda s: jnp.maximum(s - 1, 0)
    cur_batch3 = lambda s: (cur(s) // nt, 0, 0)
    return pl.pallas_call(
        functools.partial(_prompt_kernel, blocks_per_seq=nt, sample_seqs=ss),
        grid=(n_blocks + 1,),
        in_specs=[
            pl.BlockSpec((1, tb, D_MODEL), lambda s: (nxt(s) // nt, nxt(s) % nt, 0)),
            pl.BlockSpec((1, tb, D_MODEL), lambda s: (cur(s) // nt, cur(s) % nt, 0)),
            pl.BlockSpec((1, N_MEM, XA_WIDTH), cur_batch3),
            pl.BlockSpec((1, N_MEM, XA_WIDTH), cur_batch3),
            pl.BlockSpec((1, D_MODEL), const2),
            resident((D_MODEL, PROJ_COLS)),
            pl.BlockSpec((1, LANES), const2),
            pl.BlockSpec((1, ML_WIDTH), const2),
            pl.BlockSpec((CONV_K, CONV_CH), const2),
            pl.BlockSpec((1, CONV_CH), const2),
            resident((MIX_WIDTH, D_MODEL)),
            pl.BlockSpec((1, D_MODEL), const2),
            pl.BlockSpec((st, PROJ_COLS), grp2),
            pl.BlockSpec((st, LANES), grp2),
            pl.BlockSpec((st, ML_WIDTH), grp2),
            pl.BlockSpec((st, LANES), grp2),
            pl.BlockSpec((ss, ML_HEADS, ML_HD, ML_HD), grp4),
            pl.BlockSpec((ss, ML_HEADS, ML_HD), grp3),
            pl.BlockSpec((ss, CONV_K - 1, CONV_CH), grp3),
            pl.BlockSpec((ss, XA_KV_ROWS, XA_HD), grp3),
            pl.BlockSpec((ss, XA_KV_ROWS, XA_HD), grp3),
        ],
        out_specs=[
            pl.BlockSpec((1, tb, D_MODEL), lambda s: (cur(s) // nt, cur(s) % nt, 0)),
            pl.BlockSpec((1, ML_HEADS, ML_HD, ML_HD), lambda s: (cur(s) // nt, 0, 0, 0)),
            pl.BlockSpec((1, ML_HEADS, ML_HD), cur_batch3),
            pl.BlockSpec((1, SUBLANES, LANES), cur_batch3),
            pl.BlockSpec((1, CONV_K - 1, CONV_CH), cur_batch3),
            pl.BlockSpec((st, MIX_WIDTH), grp2),
            pl.BlockSpec((ss, ML_HEADS, ML_HD, ML_HD), grp4),
            pl.BlockSpec((ss, ML_HEADS, ML_HD), grp3),
            pl.BlockSpec((ss, CONV_K - 1, CONV_CH), grp3),
        ],
        out_shape=[
            jax.ShapeDtypeStruct((b, t, D_MODEL), _F32),
            jax.ShapeDtypeStruct((b, ML_HEADS, ML_HD, ML_HD), _F32),
            jax.ShapeDtypeStruct((b, ML_HEADS, ML_HD), _F32),
            jax.ShapeDtypeStruct((b, SUBLANES, LANES), _F32),
            jax.ShapeDtypeStruct((b, CONV_K - 1, CONV_CH), _F32),
            jax.ShapeDtypeStruct((n_seq * DEC_SEQ_LEN, MIX_WIDTH), _F32),
            jax.ShapeDtypeStruct(s_c0.shape, _F32),
            jax.ShapeDtypeStruct(s_n0.shape, _F32),
            jax.ShapeDtypeStruct(s_conv0.shape, _F32),
        ],
        scratch_shapes=[
            pltpu.VMEM((tb, PROJ_COLS), _F32),
            pltpu.VMEM((SUBLANES + tb, CONV_CH), _F32),
            pltpu.VMEM((tb, D_MODEL), _F32),
            pltpu.VMEM((ss, 2 * SUBLANES, CONV_CH), _F32),
        ],
        compiler_params=pltpu.CompilerParams(
            dimension_semantics=("arbitrary",), vmem_limit_bytes=VMEM_LIMIT),
        name="prompt_layer",
    )(x, x, mk, mv, g_norm, w_all, b_if, g_head, conv_w, conv_b, w_out, g_final,
      s_proj, s_cols, s_sv, s_rs, s_c0, s_n0, s_conv0, s_mk, s_mv)


def _sample_pre_kernel(x_ref, gnorm_ref, w_ref, bif_ref, m0_ref, mem_ref, gmem_ref, wkv_ref,
                       proj_ref, cols_ref, sv_ref, rs_ref, mrows_ref, mk_ref, mv_ref, mk_rows_ref, mv_rows_ref):
    _memkv_kernel(mem_ref, gmem_ref, wkv_ref, mk_ref, mv_ref, mk_rows_ref, mv_rows_ref)
    nt = x_ref.shape[0]
    proj_ref[...] = _dot(_rms(x_ref[...], gnorm_ref[...]).astype(_BF16), w_ref[...])
    slab = (proj_ref[:, COL_IF:COL_IF + LANES] + bif_ref[...]).T[0:SUBLANES, :]
    a, r, m_t, w_inter, e_negm, w_s, w_c = _gate_rows(slab, m0_ref[...], DEC_SEQ_LEN)
    cols = _gate_cols(r, w_inter, e_negm, w_s, w_c)
    cols_ref[...] = cols
    mrows_ref[...] = m_t
    row = lax.broadcasted_iota(jnp.int32, (nt, nt), 0)
    col = lax.broadcasted_iota(jnp.int32, (nt, nt), 1)
    mask = (col <= row) & ((row // DEC_SEQ_LEN) == (col // DEC_SEQ_LEN))
    rs_ref[...] = jnp.zeros_like(rs_ref)
    for h in range(ML_HEADS):
        hs = h * ML_HD
        hc = h * ML_HEAD_COLS
        qb = proj_ref[:, hc + OFF_Q:hc + OFF_Q + ML_HD].astype(_BF16)
        kb = (proj_ref[:, hc + OFF_K:hc + OFF_K + ML_HD] * (ML_HD ** -0.5)).astype(_BF16)
        vb = proj_ref[:, hc + OFF_V:hc + OFF_V + ML_HD].astype(_BF16)
        decay = jnp.exp(jnp.where(mask, a[h:h + 1, :] - cols[:, GC_R + h:GC_R + h + 1], -jnp.inf))
        s_mat = _dot_nt(qb, kb) * decay
        sv_ref[:, hs:hs + ML_HD] = _dot(s_mat.astype(_BF16), vb)
        rs_ref[:, h:h + 1] = jnp.sum(s_mat, axis=1, keepdims=True)


def _sample_pre_call(x, g_norm, w_all, b_if, m0_rows, mem, g_mem, w_kv):
    n_tok = x.shape[0]
    nt = SAMPLE_BLOCK_SEQS * DEC_SEQ_LEN
    b = mem.shape[0]
    assert n_tok // nt == b, "one prompt batch element's memory K/V per sample block"
    const = lambda i: (0, 0)
    blk = lambda i: (i, 0)
    rows_blk = lambda i: (0, i)
    per_batch = lambda i: (i, 0, 0)
    resident = functools.partial(pl.BlockSpec, index_map=const, pipeline_mode=pl.Buffered(1))
    return pl.pallas_call(
        _sample_pre_kernel,
        grid=(b,),
        in_specs=[
            pl.BlockSpec((nt, D_MODEL), blk),
            pl.BlockSpec((1, D_MODEL), const),
            resident((D_MODEL, PROJ_COLS)),
            pl.BlockSpec((1, LANES), const),
            pl.BlockSpec((SUBLANES, nt), rows_blk),
            pl.BlockSpec((1, N_MEM, D_MODEL), per_batch),
            pl.BlockSpec((1, D_MODEL), const),
            resident((D_MODEL, 2 * XA_WIDTH)),
        ],
        out_specs=[
            pl.BlockSpec((nt, PROJ_COLS), blk),
            pl.BlockSpec((nt, LANES), blk),
            pl.BlockSpec((nt, ML_WIDTH), blk),
            pl.BlockSpec((nt, LANES), blk),
            pl.BlockSpec((SUBLANES, nt), rows_blk),
            pl.BlockSpec((1, N_MEM, XA_WIDTH), per_batch),
            pl.BlockSpec((1, N_MEM, XA_WIDTH), per_batch),
            pl.BlockSpec((1, XA_KV_ROWS, XA_HD), per_batch),
            pl.BlockSpec((1, XA_KV_ROWS, XA_HD), per_batch),
        ],
        out_shape=[
            jax.ShapeDtypeStruct((n_tok, PROJ_COLS), _F32),
            jax.ShapeDtypeStruct((n_tok, LANES), _F32),
            jax.ShapeDtypeStruct((n_tok, ML_WIDTH), _F32),
            jax.ShapeDtypeStruct((n_tok, LANES), _F32),
            jax.ShapeDtypeStruct((SUBLANES, n_tok), _F32),
        ] + [jax.ShapeDtypeStruct((b, N_MEM, XA_WIDTH), _F32)] * 2
        + [jax.ShapeDtypeStruct((b, XA_KV_ROWS, XA_HD), _F32)] * 2,
        compiler_params=pltpu.CompilerParams(
            dimension_semantics=("arbitrary",), vmem_limit_bytes=VMEM_LIMIT),
        name="sample_pre_memkv",
    )(x, g_norm, w_all, b_if, m0_rows, mem, g_mem, w_kv)


def _sample_seqs(n_seqs, proj_ref, cols_ref, sv_ref, rs_ref, c0_ref, n0_ref, conv0_ref, mk_ref, mv_ref,
                 ghead_ref, convw_ref, convb_ref, ymix_ref, c_ref, n_ref, conv_ref, ext_ref):
    seqs = range(n_seqs)
    pairs = [(sl, h) for sl in seqs for h in range(ML_HEADS)]
    rows_of = [slice(sl * DEC_SEQ_LEN, (sl + 1) * DEC_SEQ_LEN) for sl in seqs]
    xa_rows = XA_HEADS * DEC_SEQ_LEN
    own_head = (lax.broadcasted_iota(jnp.int32, (xa_rows, XA_KV_ROWS), 0) // DEC_SEQ_LEN
                == lax.broadcasted_iota(jnp.int32, (xa_rows, XA_KV_ROWS), 1) % XA_HEADS)

    def ml(sl, h, off):
        return proj_ref[rows_of[sl], h * ML_HEAD_COLS + off:h * ML_HEAD_COLS + off + ML_HD]

    q4 = [jnp.concatenate(
        [proj_ref[rows_of[sl], COL_QX + h * XA_HD:COL_QX + (h + 1) * XA_HD] for h in range(XA_HEADS)], axis=0
    ).astype(_BF16) for sl in seqs]
    k_bf16 = [mk_ref[sl].astype(_BF16) for sl in seqs]
    yield

    scores = []
    for sl in seqs:
        halves = []
        for half in range(2):
            kv_rows = slice(half * XA_KV_ROWS // 2, (half + 1) * XA_KV_ROWS // 2)
            halves.append(_dot_nt(q4[sl], k_bf16[sl][kv_rows, :]))
            yield
        scores.append(jnp.where(own_head, jnp.concatenate(halves, axis=1) * (XA_HD ** -0.5), -jnp.inf))
    q_bf16 = {(sl, h): ml(sl, h, OFF_Q).astype(_BF16) for sl, h in pairs}
    c_bf16 = {(sl, h): c0_ref[sl, h].astype(_BF16) for sl, h in pairs}

    qc = {}
    for sl, h in pairs:
        qc[sl, h] = _dot(q_bf16[sl, h], c_bf16[sl, h])
        if h % 2 == 1:
            yield
    probs = []
    for sl in seqs:
        e = jnp.exp(scores[sl] - jnp.max(scores[sl], axis=1, keepdims=True))
        probs.append((e * (1.0 / jnp.sum(e, axis=1, keepdims=True))).astype(_BF16))
    v_bf16 = [mv_ref[sl].astype(_BF16) for sl in seqs]

    y4 = []
    for sl in seqs:
        y4.append(_dot(probs[sl], v_bf16[sl]))
        yield
    kw, kw_bf16, vml_bf16, wc = {}, {}, {}, {}
    for sl, h in pairs:
        gc = cols_ref[rows_of[sl], :]
        wc[sl, h] = gc[0:1, GC_WC + h:GC_WC + h + 1]
        kw[sl, h] = ml(sl, h, OFF_K) * (ML_HD ** -0.5) * gc[:, GC_WS + h:GC_WS + h + 1]
        kw_bf16[sl, h] = kw[sl, h].astype(_BF16)
        vml_bf16[sl, h] = ml(sl, h, OFF_V).astype(_BF16)

    for sl, h in pairs:
        c_ref[sl, h] = wc[sl, h] * c0_ref[sl, h] + _dot_tn(kw_bf16[sl, h], vml_bf16[sl, h])
        n_ref[sl, h:h + 1, :] = wc[sl, h] * n0_ref[sl, h:h + 1, :] + jnp.sum(kw[sl, h], axis=0, keepdims=True)
        if h % 2 == 1:
            yield

    gc_all = cols_ref[...]
    for h in range(ML_HEADS):
        hs = h * ML_HD
        hc = h * ML_HEAD_COLS
        wi_col = gc_all[:, GC_WI + h:GC_WI + h + 1]
        en_col = gc_all[:, GC_EN + h:GC_EN + h + 1]
        qc_all = jnp.concatenate([qc[sl, h] for sl in seqs], axis=0)
        n_all = jnp.concatenate(
            [jnp.broadcast_to(n0_ref[sl, h:h + 1, :], (DEC_SEQ_LEN, ML_HD)) for sl in seqs], axis=0)
        num = wi_col * qc_all + sv_ref[:, hs:hs + ML_HD]
        den = (wi_col * jnp.sum(proj_ref[:, hc + OFF_Q:hc + OFF_Q + ML_HD] * n_all, axis=1, keepdims=True)
               + rs_ref[:, h:h + 1])
        hh = num * (1.0 / jnp.maximum(jnp.abs(den), en_col))
        ymix_ref[:, hs:hs + ML_HD] = _ml_head_out(
            hh, proj_ref[:, hc + OFF_O:hc + OFF_O + ML_HD], proj_ref[:, COL_ZML + hs:COL_ZML + hs + ML_HD],
            ghead_ref[:, hs:hs + ML_HD])

    for sl in seqs:
        rows = rows_of[sl]
        u = proj_ref[rows, COL_CG:COL_CG + CONV_CH] * proj_ref[rows, COL_XC:COL_XC + CONV_CH]
        ext_ref[sl, SUBLANES - (CONV_K - 1):SUBLANES, :] = conv0_ref[sl]
        ext_ref[sl, SUBLANES:2 * SUBLANES, :] = u
        u1 = ext_ref[sl, SUBLANES - 1:2 * SUBLANES - 1, :]
        u2 = ext_ref[sl, SUBLANES - 2:2 * SUBLANES - 2, :]
        yc = convb_ref[...] + convw_ref[0:1, :] * u2 + convw_ref[1:2, :] * u1 + convw_ref[2:3, :] * u
        ymix_ref[rows, ML_WIDTH:ML_WIDTH + CONV_CH] = (
            proj_ref[rows, COL_BG:COL_BG + CONV_CH] * yc * _silu(proj_ref[rows, COL_ZCV:COL_ZCV + CONV_CH]))
        conv_ref[sl] = u[DEC_SEQ_LEN - (CONV_K - 1):DEC_SEQ_LEN, :]
        ymix_ref[rows, ML_WIDTH + CONV_CH:MIX_WIDTH] = jnp.concatenate(
            [y4[sl][h * DEC_SEQ_LEN:(h + 1) * DEC_SEQ_LEN, :] for h in range(XA_HEADS)], axis=1
        ) * _silu(proj_ref[rows, COL_ZX:COL_ZX + XA_WIDTH])


def _sample_post_kernel(ymix_ref, x_ref, wout_ref, gfinal_ref, y_ref):
    y_ref[...] = _out_tail(ymix_ref[...].astype(_BF16), wout_ref[...], x_ref[...], gfinal_ref[...])


def _sample_post_call(ymix, x, w_out, g_final):
    n_tok = x.shape[0]
    rb = 256
    const = lambda i: (0, 0)
    blk = lambda i: (i, 0)
    return pl.pallas_call(
        _sample_post_kernel,
        grid=(n_tok // rb,),
        in_specs=[
            pl.BlockSpec((rb, MIX_WIDTH), blk),
            pl.BlockSpec((rb, D_MODEL), blk),
            pl.BlockSpec((MIX_WIDTH, D_MODEL), const),
            pl.BlockSpec((1, D_MODEL), const),
        ],
        out_specs=pl.BlockSpec((rb, D_MODEL), blk),
        out_shape=jax.ShapeDtypeStruct((n_tok, D_MODEL), _F32),
        compiler_params=pltpu.CompilerParams(dimension_semantics=("arbitrary",)),
        name="sample_post",
    )(ymix, x, w_out, g_final)


def kernel(x_prompt, x_sample, state_mlstm_C, state_mlstm_n, state_mlstm_m, state_conv, cache_mem_k, cache_mem_v, mem_prompt, g_norm, w_in, b_if, g_head, conv_w, conv_b, g_mem, w_mem_kv, w_out, g_final):
    assert w_in.shape[0] == 1, "single-layer trunk"
    bp, tp, _ = x_prompt.shape
    bs, ts, _ = x_sample.shape
    assert ts == DEC_SEQ_LEN and tp % PROMPT_BLOCK == 0 and bs % SAMPLE_BLOCK_SEQS == 0

    w_all, w_out_b, w_kv_b = _wprep_call(jnp.swapaxes(w_in[0], 0, 1), w_out[0], w_mem_kv[0])
    b_if_p = jnp.pad(b_if[0][None, :], ((0, 0), (0, LANES - N_IF)))
    row = lambda v: v.reshape(1, -1)

    n_tok = bs * ts
    xs = x_sample.reshape(n_tok, D_MODEL)
    m0_rows = jnp.pad(jnp.repeat(state_mlstm_m[0].T, ts, axis=1), ((0, SUBLANES - ML_HEADS), (0, 0)))
    s_proj, s_cols, s_sv, s_rs, m_rows, mk_p, mv_p, mk_rows, mv_rows = _sample_pre_call(
        xs, row(g_norm[0]), w_all, b_if_p, m0_rows, mem_prompt, row(g_mem[0]), w_kv_b)

    y_p, c_p, n_p, m_p, conv_p, s_ymix, c_s, n_s, conv_s = _prompt_call(
        x_prompt, mk_p, mv_p, row(g_norm[0]), w_all, b_if_p, row(g_head[0]), conv_w[0], row(conv_b[0]),
        w_out_b, row(g_final),
        s_proj, s_cols, s_sv, s_rs, state_mlstm_C[0], state_mlstm_n[0], state_conv[0],
        cache_mem_k[0].reshape(bs, XA_KV_ROWS, XA_HD), cache_mem_v[0].reshape(bs, XA_KV_ROWS, XA_HD))

    y_s = _sample_post_call(s_ymix, xs, w_out_b, row(g_final))
    m_s = m_rows[:ML_HEADS, ts - 1::ts].T

    kv_shape = (1, bp, N_MEM, XA_HEADS, XA_HD)
    return (y_p, y_s.reshape(bs, ts, D_MODEL),
            c_p[None], n_p[None], m_p[:, :ML_HEADS, 0][None], conv_p[None],
            mk_rows.reshape(kv_shape), mv_rows.reshape(kv_shape),
            c_s[None], n_s[None], m_s[None], conv_s[None])
```
